```python
import numpy as np
import jax
import jax.numpy as jnp
from jax import lax

D_MODEL = 2048
BATCH = 4
SEQ = 2048
DEPTH = 4
DEC_BATCH = 8
DEC_SEQ = 1
PAST_LEN = 16384
PAGE_SIZE = 128

N_A_LAYERS = DEPTH // 2
N_B_LAYERS = DEPTH - N_A_LAYERS
SSM_GROUP = 16
SSM_GROUPS = D_MODEL // SSM_GROUP
SSM_STATE = 64
DT_MIN = 0.001
DT_MAX = 0.1
N_HEADS = 16
HEAD_DIM = D_MODEL // N_HEADS
N_KV_HEADS = 4
GQA_REP = N_HEADS // N_KV_HEADS
CMP_BLOCK = 32
CMP_STRIDE = 16
CMP_PARTS = CMP_BLOCK // CMP_STRIDE
CMP_HIDDEN = HEAD_DIM
SLC_BLOCK = 64
N_SELECT = 16
WINDOW = 512
WIN_QBLOCK = 128
SLC_QBLOCK = 32
ROPE_THETA = 10000.0
ATTN_SCALE = HEAD_DIM ** -0.5
D_FF = -(-8 * D_MODEL // (3 * 256)) * 256
PLE_DIM = 256
NORM_EPS = 1e-6
NEG_INF = -1e30
FORCED_SCORE = 1e9

kernel_name = 'yoco_s5_nsa_decode_step'


def rmsnorm(x, g):
    xf = x.astype(jnp.float32)
    y = xf * lax.rsqrt(jnp.mean(xf * xf, axis=-1, keepdims=True) + NORM_EPS)
    return (y * g.astype(jnp.float32)).astype(x.dtype)


def rope(x, pos):
    half = HEAD_DIM // 2
    inv_freq = ROPE_THETA ** (-jnp.arange(half, dtype=jnp.float32) / half)
    ang = pos.astype(jnp.float32)[:, None] * inv_freq[None, :]
    cos = jnp.cos(ang)[:, None, :]
    sin = jnp.sin(ang)[:, None, :]
    xf = x.astype(jnp.float32)
    x1, x2 = xf[..., :half], xf[..., half:]
    return jnp.concatenate([x1 * cos - x2 * sin, x2 * cos + x1 * sin], axis=-1).astype(x.dtype)


def masked_softmax(s, mask):
    s = jnp.where(mask, s, NEG_INF)
    m = jnp.max(s, axis=-1, keepdims=True)
    p = jnp.exp(s - m) * mask
    return p / jnp.maximum(jnp.sum(p, axis=-1, keepdims=True), 1e-30)


def swiglu(h, wg, wu, wd):
    return (jax.nn.silu(h @ wg) * (h @ wu)) @ wd


def s5_mixer(u, h0_re, h0_im, a_re, a_im, log_dt, b_re, b_im, c_re, c_im, d, w_glu, b_glu):
    bsz, t = u.shape[:2]
    f32 = jnp.float32
    a_re = a_re.astype(f32)
    a_im = a_im.astype(f32)
    dt = jnp.exp(log_dt.astype(f32))[:, None]
    mag = jnp.exp(dt * a_re)
    ab_re, ab_im = mag * jnp.cos(dt * a_im), mag * jnp.sin(dt * a_im)
    den = a_re * a_re + a_im * a_im
    nr, ni = ab_re - 1.0, ab_im
    cf_re = (nr * a_re + ni * a_im) / den
    cf_im = (ni * a_re - nr * a_im) / den
    uf = u.astype(f32).reshape(bsz, t, SSM_GROUPS, SSM_GROUP)
    bu_re = jnp.einsum('btgh,gph->btgp', uf, b_re.astype(f32))
    bu_im = jnp.einsum('btgh,gph->btgp', uf, b_im.astype(f32))
    x_re = cf_re * bu_re - cf_im * bu_im
    x_im = cf_re * bu_im + cf_im * bu_re
    x_re = x_re.at[:, 0].add(ab_re * h0_re - ab_im * h0_im)
    x_im = x_im.at[:, 0].add(ab_re * h0_im + ab_im * h0_re)
    a_t_re = jnp.broadcast_to(ab_re, (1, t, SSM_GROUPS, SSM_STATE))
    a_t_im = jnp.broadcast_to(ab_im, (1, t, SSM_GROUPS, SSM_STATE))

    def combine(e1, e2):
        a1r, a1i, b1r, b1i = e1
        a2r, a2i, b2r, b2i = e2
        return (a2r * a1r - a2i * a1i, a2r * a1i + a2i * a1r,
                a2r * b1r - a2i * b1i + b2r, a2r * b1i + a2i * b1r + b2i)

    _, _, h_re, h_im = lax.associative_scan(combine, (a_t_re, a_t_im, x_re, x_im), axis=1)
    y = (jnp.einsum('btgp,ghp->btgh', h_re, c_re.astype(f32))
         - jnp.einsum('btgp,ghp->btgh', h_im, c_im.astype(f32)))
    y = y.reshape(bsz, t, D_MODEL) + d.astype(f32) * u.astype(f32)
    g = jax.nn.gelu(y)
    z = g @ w_glu.astype(f32) + b_glu.astype(f32)
    out = z[..., :D_MODEL] * jax.nn.sigmoid(z[..., D_MODEL:])
    return out.astype(u.dtype), h_re[:, -1], h_im[:, -1]


def shared_kv_rows(h, pos, norm_kv_g, kv_w):
    bsz, t = h.shape[:2]
    kv = (rmsnorm(h, norm_kv_g) @ kv_w).reshape(bsz, t, 6, N_KV_HEADS, HEAD_DIM)
    ck, cv, sk, sv, wk, wv = [kv[:, :, i] for i in range(6)]
    return ck, cv, rope(sk, pos), sv, rope(wk, pos), wv


def compress_rows(rows, w1, b1, w2):
    bsz, t = rows.shape[:2]
    n_units = t // CMP_STRIDE
    n_cmp = n_units - CMP_PARTS + 1
    units = rows[:, :n_units * CMP_STRIDE].reshape(bsz, n_units, CMP_STRIDE, N_KV_HEADS, HEAD_DIM)
    w1p = w1.reshape(CMP_PARTS, CMP_STRIDE, HEAD_DIM, CMP_HIDDEN)
    h = b1
    for j in range(CMP_PARTS):
        h = h + jnp.einsum('bnsgd,sdh->bngh', units[:, j:j + n_cmp], w1p[j])
    return jax.nn.gelu(h) @ w2


def compress_kv(ck, cv, w):
    k_cmp = compress_rows(ck, w['cmp_k_w1'], w['cmp_k_b1'], w['cmp_k_w2'])
    n_cmp = k_cmp.shape[1]
    k_cmp = rope(k_cmp, jnp.arange(n_cmp, dtype=jnp.int32) * CMP_STRIDE + (CMP_BLOCK - 1))
    v_cmp = compress_rows(cv, w['cmp_v_w1'], w['cmp_v_b1'], w['cmp_v_w2'])
    return k_cmp, v_cmp


def cmp_branch(q, qpos, k_cmp, v_cmp):
    n_cmp = k_cmp.shape[1]
    k_end = jnp.arange(n_cmp, dtype=jnp.int32) * CMP_STRIDE + (CMP_BLOCK - 1)
    s = jnp.einsum('btgrd,bngd->btgrn', q, k_cmp).astype(jnp.float32) * ATTN_SCALE
    mask = (k_end[None, :] <= qpos[:, None])[None, :, None, None, :]
    p = masked_softmax(s, mask)
    o = jnp.einsum('btgrn,bngd->btgrd', p.astype(v_cmp.dtype), v_cmp)
    return o, jnp.sum(p, axis=3)


def cmp_to_slc_matrix(n_cmp, n_slc):
    units = np.arange(n_cmp)[:, None] + np.arange(CMP_PARTS)[None, :]
    sblk = (units * CMP_STRIDE) // SLC_BLOCK
    m = np.zeros((n_cmp, n_slc), np.float32)
    np.add.at(m, (np.repeat(np.arange(n_cmp), CMP_PARTS), sblk.reshape(-1)), 1.0)
    return m


def select_blocks(p_grp, qpos, n_slc):
    n_cmp = p_grp.shape[-1]
    m = jnp.asarray(cmp_to_slc_matrix(n_cmp, n_slc))
    score = jnp.einsum('btgn,nj->btgj', p_grp, m)
    j = jnp.arange(n_slc, dtype=jnp.int32)[None, :]
    cur = (qpos // SLC_BLOCK)[:, None]
    valid = j * SLC_BLOCK <= qpos[:, None]
    forced = (j == 0) | (j == cur) | (j == cur - 1)
    score = jnp.where(valid[None, :, None, :],
                      jnp.where(forced[None, :, None, :], FORCED_SCORE, score), NEG_INF)
    vals, idx = lax.top_k(score, min(N_SELECT, n_slc))
    return idx, vals > 0.5 * NEG_INF


def block_positions(idx):
    pos = idx[..., None] * SLC_BLOCK + jnp.arange(SLC_BLOCK, dtype=jnp.int32)
    return pos.reshape(*idx.shape[:-1], -1)


def gathered_attn(q, qpos, k_g, v_g, kpos, kok):
    s = jnp.einsum('btgrd,btgkd->btgrk', q, k_g).astype(jnp.float32) * ATTN_SCALE
    mask = (kok & (kpos <= qpos[None, :, None, None]))[:, :, :, None, :]
    p = masked_softmax(s, mask)
    return jnp.einsum('btgrk,btgkd->btgrd', p.astype(v_g.dtype), v_g)


def slc_branch_contiguous(q, qpos, idx, ok, sk, sv):
    bsz, t = sk.shape[:2]
    nb = t // SLC_QBLOCK
    bi = jnp.arange(bsz)[:, None, None, None]
    gi = jnp.arange(N_KV_HEADS)[None, None, :, None]

    def to_blocks(a):
        return a.reshape(bsz, nb, SLC_QBLOCK, *a.shape[2:]).swapaxes(0, 1)

    def one(args):
        qb, qpb, ib, okb = args
        kpos = block_positions(ib)
        kc = jnp.minimum(kpos, t - 1)
        return gathered_attn(qb, qpb, sk[bi, kc, gi], sv[bi, kc, gi], kpos,
                             jnp.repeat(okb, SLC_BLOCK, axis=-1))

    out = lax.map(one, (to_blocks(q), qpos.reshape(nb, SLC_QBLOCK), to_blocks(idx), to_blocks(ok)))
    return out.swapaxes(0, 1).reshape(bsz, t, N_KV_HEADS, GQA_REP, HEAD_DIM)


def gather_paged_rows(pool, page_table, new_rows, pos):
    bsz = pos.shape[0]
    past = page_table.shape[1] * PAGE_SIZE
    bi = jnp.arange(bsz)[:, None, None, None]
    gi = jnp.arange(N_KV_HEADS)[None, None, :, None]
    pc = jnp.clip(pos, 0, past - 1)
    page = page_table[bi, pc // PAGE_SIZE]
    from_pool = pool[page, pc % PAGE_SIZE, gi]
    pn = jnp.clip(pos - past, 0, new_rows.shape[1] - 1)
    from_new = new_rows[bi, pn, gi]
    return jnp.where((pos < past)[..., None], from_pool, from_new)


def slc_branch_paged(q, qpos, idx, ok, pool_k, pool_v, page_table, new_k, new_v):
    kpos = block_positions(idx)
    kg = gather_paged_rows(pool_k, page_table, new_k, kpos)
    vg = gather_paged_rows(pool_v, page_table, new_v, kpos)
    return gathered_attn(q, qpos, kg, vg, kpos, jnp.repeat(ok, SLC_BLOCK, axis=-1))


def window_attn(q, qpos, k, v, kpos):
    s = jnp.einsum('bnqgrd,bnkgd->bngrqk', q, k).astype(jnp.float32) * ATTN_SCALE
    dist = qpos[:, :, None] - kpos[:, None, :]
    mask = (kpos[:, None, :] >= 0) & (dist >= 0) & (dist < WINDOW)
    p = masked_softmax(s, mask[None, :, None, None])
    return jnp.einsum('bngrqk,bnkgd->bnqgrd', p.astype(v.dtype), v)


def win_branch_banded(q, qpos, wk, wv):
    bsz, t = q.shape[:2]
    nb = t // WIN_QBLOCK
    span = WINDOW + WIN_QBLOCK
    idx = np.arange(nb)[:, None] * WIN_QBLOCK + np.arange(span)[None, :]
    pad = ((0, 0), (WINDOW, 0), (0, 0), (0, 0))
    kb = jnp.pad(wk, pad)[:, idx]
    vb = jnp.pad(wv, pad)[:, idx]
    kpos = jnp.asarray(idx - WINDOW, dtype=jnp.int32)
    qb = q.reshape(bsz, nb, WIN_QBLOCK, N_KV_HEADS, GQA_REP, HEAD_DIM)
    o = window_attn(qb, qpos.reshape(nb, WIN_QBLOCK), kb, vb, kpos)
    return o.reshape(bsz, t, N_KV_HEADS, GQA_REP, HEAD_DIM)


def win_branch_buffer(q, qpos, kcat, vcat, kpos):
    o = window_attn(q[:, None], qpos[None], kcat[:, None], vcat[:, None], kpos[None])
    return o[:, 0]


def nsa_mixer(hn, pos, kv, w_q, w_gate, b_gate, w_o):
    bsz, t = hn.shape[:2]
    q = rope((hn @ w_q).reshape(bsz, t, N_HEADS, HEAD_DIM), pos)
    q = q.reshape(bsz, t, N_KV_HEADS, GQA_REP, HEAD_DIM)
    gates = jax.nn.sigmoid(hn @ w_gate + b_gate).reshape(bsz, t, 3, N_KV_HEADS, GQA_REP)[..., None]
    o_cmp, p_grp = cmp_branch(q, pos, kv['k_cmp'], kv['v_cmp'])
    idx, ok = select_blocks(p_grp, pos, kv['n_slc'])
    o_slc = kv['slc_fn'](q, pos, idx, ok)
    o_win = kv['win_fn'](q, pos)
    o = gates[:, :, 0] * o_cmp + gates[:, :, 1] * o_slc + gates[:, :, 2] * o_win
    return o.reshape(bsz, t, N_HEADS * HEAD_DIM) @ w_o


def trunk(x, p, pos, h0_re, h0_im, build_kv, w):
    ssm_re, ssm_im = [], []
    kv = None
    for i in range(DEPTH):
        hn = rmsnorm(x, w['norm_mix_g'][i])
        if i < N_A_LAYERS:
            o, hr, hi = s5_mixer(hn, h0_re[i], h0_im[i], w['ssm_a_re'][i], w['ssm_a_im'][i],
                                 w['ssm_log_dt'][i], w['ssm_b_re'][i], w['ssm_b_im'][i],
                                 w['ssm_c_re'][i], w['ssm_c_im'][i], w['ssm_d'][i],
                                 w['ssm_w_glu'][i], w['ssm_b_glu'][i])
            ssm_re.append(hr)
            ssm_im.append(hi)
        else:
            j = i - N_A_LAYERS
            o = nsa_mixer(hn, pos, kv, w['nsa_w_q'][j], w['nsa_w_gate'][j],
                          w['nsa_b_gate'][j], w['nsa_w_o'][j])
        x = x + o
        x = x + swiglu(rmsnorm(x, w['norm_ffn_g'][i]), w['ffn_w_gate'][i],
                       w['ffn_w_up'][i], w['ffn_w_down'][i])
        gate = jax.nn.sigmoid(rmsnorm(x, w['norm_ple_g'][i]) @ w['ple_w_gate'][i])
        x = x + gate * (p[i] @ w['ple_w_proj'][i])
        if i == N_A_LAYERS - 1:
            kv = build_kv(x)
    return rmsnorm(x, w['norm_final_g']), jnp.stack(ssm_re), jnp.stack(ssm_im), kv['new']


def setup_inputs(seed: int = 0) -> dict:
    key = jax.random.key(seed)
    ks = iter(jax.random.split(key, 64))
    f32 = jnp.float32

    def nrm(shape, scale):
        return scale * jax.random.normal(next(ks), shape, f32)

    def gain(shape):
        return 1.0 + nrm(shape, 0.02)

    n_pages = PAST_LEN // PAGE_SIZE
    n_used = DEC_BATCH * n_pages
    n_pool = n_used + (n_used + 3) // 4
    win_buf = min(WINDOW, PAST_LEN)
    kv_pool = (n_pool, PAGE_SIZE, N_KV_HEADS, HEAD_DIM)
    ssm_st = (N_A_LAYERS, DEC_BATCH, SSM_GROUPS, SSM_STATE)
    x_prompt = nrm((BATCH, SEQ, D_MODEL), 1.0)
    x_sample = nrm((DEC_BATCH, DEC_SEQ, D_MODEL), 1.0)
    state_ssm_re = nrm(ssm_st, 0.5)
    state_ssm_im = nrm(ssm_st, 0.5)
    cache_cmp_k = nrm(kv_pool, 1.0)
    cache_cmp_v = nrm(kv_pool, 1.0)
    cache_slc_k = nrm(kv_pool, 1.0)
    cache_slc_v = nrm(kv_pool, 1.0)
    state_win_k = nrm((DEC_BATCH, win_buf, N_KV_HEADS, HEAD_DIM), 1.0)
    state_win_v = nrm((DEC_BATCH, win_buf, N_KV_HEADS, HEAD_DIM), 1.0)
    page_table = jax.random.permutation(next(ks), n_pool)[:n_used].reshape(DEC_BATCH, n_pages).astype(jnp.int32)
    p_prompt = nrm((DEPTH, BATCH, SEQ, PLE_DIM), 1.0)
    p_sample = nrm((DEPTH, DEC_BATCH, DEC_SEQ, PLE_DIM), 1.0)
    a_shape = (N_A_LAYERS, SSM_GROUPS, SSM_STATE)
    n_idx = jnp.arange(SSM_STATE, dtype=f32)
    return {
        'x_prompt': x_prompt, 'x_sample': x_sample,
        'state_ssm_re': state_ssm_re, 'state_ssm_im': state_ssm_im,
        'cache_cmp_k': cache_cmp_k, 'cache_cmp_v': cache_cmp_v,
        'cache_slc_k': cache_slc_k, 'cache_slc_v': cache_slc_v,
        'state_win_k': state_win_k, 'state_win_v': state_win_v,
        'page_table': page_table, 'p_prompt': p_prompt, 'p_sample': p_sample,
        'norm_mix_g': gain((DEPTH, D_MODEL)), 'norm_ffn_g': gain((DEPTH, D_MODEL)),
        'norm_ple_g': gain((DEPTH, D_MODEL)), 'norm_kv_g': gain((D_MODEL,)),
        'norm_final_g': gain((D_MODEL,)),
        'ssm_a_re': -0.5 + nrm(a_shape, 0.01),
        'ssm_a_im': jnp.pi * n_idx + nrm(a_shape, 0.01),
        'ssm_log_dt': jax.random.uniform(next(ks), (N_A_LAYERS, SSM_GROUPS), f32,
                                         float(np.log(DT_MIN)), float(np.log(DT_MAX))),
        'ssm_b_re': nrm((N_A_LAYERS, SSM_GROUPS, SSM_STATE, SSM_GROUP), (2 * SSM_GROUP) ** -0.5),
        'ssm_b_im': nrm((N_A_LAYERS, SSM_GROUPS, SSM_STATE, SSM_GROUP), (2 * SSM_GROUP) ** -0.5),
        'ssm_c_re': nrm((N_A_LAYERS, SSM_GROUPS, SSM_GROUP, SSM_STATE), (2 * SSM_STATE) ** -0.5),
        'ssm_c_im': nrm((N_A_LAYERS, SSM_GROUPS, SSM_GROUP, SSM_STATE), (2 * SSM_STATE) ** -0.5),
        'ssm_d': 1.0 + nrm((N_A_LAYERS, D_MODEL), 0.1),
        'ssm_w_glu': nrm((N_A_LAYERS, D_MODEL, 2 * D_MODEL), D_MODEL ** -0.5),
        'ssm_b_glu': nrm((N_A_LAYERS, 2 * D_MODEL), 0.01),
        'kv_w': nrm((D_MODEL, 6 * N_KV_HEADS * HEAD_DIM), D_MODEL ** -0.5),
        'cmp_k_w1': nrm((CMP_BLOCK, HEAD_DIM, CMP_HIDDEN), (CMP_BLOCK * HEAD_DIM) ** -0.5),
        'cmp_k_b1': nrm((CMP_HIDDEN,), 0.01),
        'cmp_k_w2': nrm((CMP_HIDDEN, HEAD_DIM), CMP_HIDDEN ** -0.5),
        'cmp_v_w1': nrm((CMP_BLOCK, HEAD_DIM, CMP_HIDDEN), (CMP_BLOCK * HEAD_DIM) ** -0.5),
        'cmp_v_b1': nrm((CMP_HIDDEN,), 0.01),
        'cmp_v_w2': nrm((CMP_HIDDEN, HEAD_DIM), CMP_HIDDEN ** -0.5),
        'nsa_w_q': nrm((N_B_LAYERS, D_MODEL, N_HEADS * HEAD_DIM), D_MODEL ** -0.5),
        'nsa_w_gate': nrm((N_B_LAYERS, D_MODEL, 3 * N_HEADS), D_MODEL ** -0.5),
        'nsa_b_gate': nrm((N_B_LAYERS, 3 * N_HEADS), 0.01),
        'nsa_w_o': nrm((N_B_LAYERS, N_HEADS * HEAD_DIM, D_MODEL), (N_HEADS * HEAD_DIM) ** -0.5),
        'ffn_w_gate': nrm((DEPTH, D_MODEL, D_FF), D_MODEL ** -0.5),
        'ffn_w_up': nrm((DEPTH, D_MODEL, D_FF), D_MODEL ** -0.5),
        'ffn_w_down': nrm((DEPTH, D_FF, D_MODEL), D_FF ** -0.5),
        'ple_w_gate': nrm((DEPTH, D_MODEL, D_MODEL), D_MODEL ** -0.5),
        'ple_w_proj': nrm((DEPTH, PLE_DIM, D_MODEL), PLE_DIM ** -0.5),
    }


def reference(x_prompt, x_sample, state_ssm_re, state_ssm_im, cache_cmp_k, cache_cmp_v,
              cache_slc_k, cache_slc_v, state_win_k, state_win_v, page_table, p_prompt, p_sample,
              norm_mix_g, norm_ffn_g, norm_ple_g, norm_kv_g, norm_final_g,
              ssm_a_re, ssm_a_im, ssm_log_dt, ssm_b_re, ssm_b_im, ssm_c_re, ssm_c_im, ssm_d,
              ssm_w_glu, ssm_b_glu, kv_w, cmp_k_w1, cmp_k_b1, cmp_k_w2, cmp_v_w1, cmp_v_b1,
              cmp_v_w2, nsa_w_q, nsa_w_gate, nsa_b_gate, nsa_w_o, ffn_w_gate, ffn_w_up,
              ffn_w_down, ple_w_gate, ple_w_proj):
    w = dict(norm_mix_g=norm_mix_g, norm_ffn_g=norm_ffn_g, norm_ple_g=norm_ple_g,
             norm_final_g=norm_final_g, ssm_a_re=ssm_a_re, ssm_a_im=ssm_a_im,
             ssm_log_dt=ssm_log_dt, ssm_b_re=ssm_b_re, ssm_b_im=ssm_b_im, ssm_c_re=ssm_c_re,
             ssm_c_im=ssm_c_im, ssm_d=ssm_d, ssm_w_glu=ssm_w_glu, ssm_b_glu=ssm_b_glu,
             cmp_k_w1=cmp_k_w1, cmp_k_b1=cmp_k_b1, cmp_k_w2=cmp_k_w2, cmp_v_w1=cmp_v_w1,
             cmp_v_b1=cmp_v_b1, cmp_v_w2=cmp_v_w2, nsa_w_q=nsa_w_q, nsa_w_gate=nsa_w_gate,
             nsa_b_gate=nsa_b_gate, nsa_w_o=nsa_w_o, ffn_w_gate=ffn_w_gate, ffn_w_up=ffn_w_up,
             ffn_w_down=ffn_w_down, ple_w_gate=ple_w_gate, ple_w_proj=ple_w_proj)
    past = page_table.shape[1] * PAGE_SIZE
    pos_p = jnp.arange(x_prompt.shape[1], dtype=jnp.int32)
    pos_s = past + jnp.arange(x_sample.shape[1], dtype=jnp.int32)

    def build_kv_prompt(h):
        ck, cv, sk, sv, wk, wv = shared_kv_rows(h, pos_p, norm_kv_g, kv_w)
        k_cmp, v_cmp = compress_kv(ck, cv, w)
        t = h.shape[1]
        wb = min(WINDOW, t)
        return dict(k_cmp=k_cmp, v_cmp=v_cmp, n_slc=-(-t // SLC_BLOCK),
                    slc_fn=lambda q, qpos, idx, ok: slc_branch_contiguous(q, qpos, idx, ok, sk, sv),
                    win_fn=lambda q, qpos: win_branch_banded(q, qpos, wk, wv),
                    new=(ck, cv, sk, sv, wk[:, t - wb:], wv[:, t - wb:]))

    def build_kv_sample(h):
        ck, cv, sk, sv, wk, wv = shared_kv_rows(h, pos_s, norm_kv_g, kv_w)
        bd, tn = h.shape[:2]
        ck_all = jnp.concatenate(
            [cache_cmp_k[page_table].reshape(bd, past, N_KV_HEADS, HEAD_DIM), ck], axis=1)
        cv_all = jnp.concatenate(
            [cache_cmp_v[page_table].reshape(bd, past, N_KV_HEADS, HEAD_DIM), cv], axis=1)
        k_cmp, v_cmp = compress_kv(ck_all, cv_all, w)
        wb = state_win_k.shape[1]
        kcat = jnp.concatenate([state_win_k, wk], axis=1)
        vcat = jnp.concatenate([state_win_v, wv], axis=1)
        kpos_win = past - wb + jnp.arange(wb + tn, dtype=jnp.int32)
        return dict(k_cmp=k_cmp, v_cmp=v_cmp, n_slc=-(-(past + tn) // SLC_BLOCK),
                    slc_fn=lambda q, qpos, idx, ok: slc_branch_paged(
                        q, qpos, idx, ok, cache_slc_k, cache_slc_v, page_table, sk, sv),
                    win_fn=lambda q, qpos: win_branch_buffer(q, qpos, kcat, vcat, kpos_win),
                    new=(ck, cv, sk, sv, kcat[:, tn:], vcat[:, tn:]))

    h0 = jnp.zeros((N_A_LAYERS, x_prompt.shape[0], SSM_GROUPS, SSM_STATE), jnp.float32)
    y_prompt, sre_p, sim_p, new_p = trunk(x_prompt, p_prompt, pos_p, h0, h0, build_kv_prompt, w)
    y_sample, sre_s, sim_s, new_s = trunk(x_sample, p_sample, pos_s, state_ssm_re, state_ssm_im,
                                          build_kv_sample, w)
    ck_p, cv_p, sk_p, sv_p, wk_p, wv_p = new_p
    ck_s, cv_s, sk_s, sv_s, wk_s, wv_s = new_s
    return (y_prompt, y_sample, sre_p, sim_p, ck_p, cv_p, sk_p, sv_p, wk_p, wv_p,
            sre_s, sim_s, ck_s, cv_s, sk_s, sv_s, wk_s, wv_s)
```

```python
import functools

import numpy as np
import jax
import jax.numpy as jnp
from jax import lax
from jax.experimental import pallas as pl
from jax.experimental.pallas import tpu as pltpu

F32 = jnp.float32
BF16 = jnp.bfloat16

N_HEADS = 16
N_KV_HEADS = 4
GQA_REP = N_HEADS // N_KV_HEADS
HEAD_DIM = 128
SSM_GROUP = 16
SSM_STATE = 64
CMP_BLOCK = 32
CMP_STRIDE = 16
CMP_PARTS = CMP_BLOCK // CMP_STRIDE
SLC_BLOCK = 64
N_SELECT = 16
WINDOW = 512
PAGE_SIZE = 128
ROPE_THETA = 10000.0
ATTN_SCALE = HEAD_DIM ** -0.5
NORM_EPS = 1e-6
NEG_INF = -1e30
FORCED_SCORE = 1e9

LANES = 128
KV_LANES = N_KV_HEADS * HEAD_DIM
PAIR_CH = 2 * LANES
PAIR_ST = (LANES // SSM_GROUP) * SSM_STATE
SLAB_ROWS = 2048
VMEM_LIMIT = 56 * 1024 * 1024


def _cparams(sem, vmem=VMEM_LIMIT):
    return pltpu.CompilerParams(dimension_semantics=sem, vmem_limit_bytes=vmem)


def _rms(x, g):
    ms = jnp.mean(x * x, axis=-1, keepdims=True)
    return x * lax.rsqrt(ms + NORM_EPS) * g


def _rope(x, cosf, sinf):
    return x * cosf + pltpu.roll(x, HEAD_DIM // 2, axis=1) * sinf


def _dot(a, b):
    return jnp.dot(a, b, preferred_element_type=F32)


def _dot_nt(a, b):
    return lax.dot_general(a, b, (((1,), (1,)), ((), ())), preferred_element_type=F32)


def _copy_kernel(x_ref, o_ref):
    o_ref[...] = x_ref[0]


def to_time_major(x, tm):
    b, t, d = x.shape
    return pl.pallas_call(
        _copy_kernel, grid=(b, t // tm),
        in_specs=[pl.BlockSpec((1, tm, d), lambda c, i: (c, i, 0))],
        out_specs=pl.BlockSpec((tm, d), lambda c, i: (i, c)),
        out_shape=jax.ShapeDtypeStruct((t, b * d), x.dtype),
        compiler_params=_cparams(("parallel", "parallel")),
        name="to_time_major")(x)


def _norm_dup_kernel(x_ref, g_ref, o_ref):
    o_ref[...] = _rms(x_ref[...], g_ref[...]).astype(BF16)


def norm_dup(x2, g, C, R, tm, sample):
    d = g.shape[-1]
    if sample:
        out_shape = jax.ShapeDtypeStruct((2 * R, d), BF16)
        out_map = lambda i, c: (c, 0)
    else:
        out_shape = jax.ShapeDtypeStruct((R, 2 * C * d), BF16)
        out_map = lambda i, c: (i, c)
    return pl.pallas_call(
        _norm_dup_kernel, grid=(R // tm, 2 * C),
        in_specs=[pl.BlockSpec((tm, d), lambda i, c: (i, c % C)),
                  pl.BlockSpec((1, d), lambda i, c: (0, 0))],
        out_specs=pl.BlockSpec((tm, d), out_map),
        out_shape=out_shape,
        compiler_params=_cparams(("parallel", "parallel")),
        name="norm_dup")(x2, g)


def _final_norm_kernel(x_ref, g_ref, o_ref):
    o_ref[0] = _rms(x_ref[...], g_ref[...])


def final_norm(x2, g, C, R, tm):
    d = g.shape[-1]
    return pl.pallas_call(
        _final_norm_kernel, grid=(C, R // tm),
        in_specs=[pl.BlockSpec((tm, d), lambda c, i: (i, c)),
                  pl.BlockSpec((1, d), lambda c, i: (0, 0))],
        out_specs=pl.BlockSpec((1, tm, d), lambda c, i: (c, i, 0)),
        out_shape=jax.ShapeDtypeStruct((C, R, d), F32),
        compiler_params=_cparams(("parallel", "parallel")),
        name="final_norm")(x2, g)


def _s5_kernel(hn_ref, wre_ref, wim_ref, cre_ref, cim_ref, ar_ref, ai_ref, d_ref, h0r_ref, h0i_ref,
               g_ref, hr_ref, hi_ref, xr_s, xi_s, sr_s, si_s, *, tc, r2, nb, npair):
    @pl.when(pl.program_id(0) == 0)
    def _():
        hr_ref[...] = h0r_ref[...]
        hi_ref[...] = h0i_ref[...]

    rows = tc * r2
    row_j = (lax.broadcasted_iota(jnp.int32, (rows, PAIR_CH), 0) // nb) % 2
    lane_hi = lax.broadcasted_iota(jnp.int32, (rows, PAIR_CH), 1) >= LANES
    keep = (row_j == 1) == lane_hi

    for k in range(npair):
        cols = slice(k * PAIR_CH, (k + 1) * PAIR_CH)
        lhs = jnp.where(keep, hn_ref[:, cols], jnp.zeros((rows, PAIR_CH), BF16))
        xr_s[...] = _dot(lhs, wre_ref[k]).reshape(tc, r2, PAIR_ST)
        xi_s[...] = _dot(lhs, wim_ref[k]).reshape(tc, r2, PAIR_ST)
        ar = ar_ref[k]
        ai = ai_ref[k]

        def body(t, carry):
            hr, hi = carry
            nr = ar * hr - ai * hi + xr_s[t]
            ni = ar * hi + ai * hr + xi_s[t]
            sr_s[t] = nr
            si_s[t] = ni
            return nr, ni

        hr, hi = lax.fori_loop(0, tc, body, (hr_ref[k], hi_ref[k]), unroll=min(8, tc))
        hr_ref[k] = hr
        hi_ref[k] = hi
        h_re = sr_s[...].reshape(rows, PAIR_ST).astype(BF16)
        h_im = si_s[...].reshape(rows, PAIR_ST).astype(BF16)
        y = _dot(h_re, cre_ref[k]) - _dot(h_im, cim_ref[k])
        y = y + d_ref[:, cols] * lhs.astype(F32)
        g_ref[:, cols] = jnp.where(keep, jax.nn.gelu(y), 0.0).astype(BF16)


def s5_scan(hn2, prm, h0r, h0i, T, nb, tc):
    r2 = 2 * nb
    d = hn2.shape[1]
    npair = d // PAIR_CH
    rows = tc * r2
    full3 = lambda s: (0, 0, 0)
    st_shape = (npair, r2, PAIR_ST)
    kern = functools.partial(_s5_kernel, tc=tc, r2=r2, nb=nb, npair=npair)
    return pl.pallas_call(
        kern, grid=(T // tc,),
        in_specs=[pl.BlockSpec((rows, d), lambda s: (s, 0)),
                  pl.BlockSpec((npair, PAIR_CH, PAIR_ST), full3),
                  pl.BlockSpec((npair, PAIR_CH, PAIR_ST), full3),
                  pl.BlockSpec((npair, PAIR_ST, PAIR_CH), full3),
                  pl.BlockSpec((npair, PAIR_ST, PAIR_CH), full3),
                  pl.BlockSpec(st_shape, full3),
                  pl.BlockSpec(st_shape, full3),
                  pl.BlockSpec((1, d), lambda s: (0, 0)),
                  pl.BlockSpec(st_shape, full3),
                  pl.BlockSpec(st_shape, full3)],
        out_specs=[pl.BlockSpec((rows, d), lambda s: (s, 0)),
                   pl.BlockSpec(st_shape, full3),
                   pl.BlockSpec(st_shape, full3)],
        out_shape=[jax.ShapeDtypeStruct((T * r2, d), BF16),
                   jax.ShapeDtypeStruct(st_shape, F32),
                   jax.ShapeDtypeStruct(st_shape, F32)],
        scratch_shapes=[pltpu.VMEM((tc, r2, PAIR_ST), F32)] * 4,
        compiler_params=_cparams(("arbitrary",)),
        name="s5_scan")(hn2, prm["wre"], prm["wim"], prm["cre"], prm["cim"],
                        prm["ar"], prm["ai"], prm["d"], h0r, h0i)


def s5_params(a_re, a_im, log_dt, b_re, b_im, c_re, c_im, d, nb):
    G, P, H = b_re.shape
    gpc = LANES // H
    nchunk = G // gpc
    dt = jnp.exp(log_dt)[:, None]
    mag = jnp.exp(dt * a_re)
    ab_re, ab_im = mag * jnp.cos(dt * a_im), mag * jnp.sin(dt * a_im)
    den = a_re * a_re + a_im * a_im
    nr, ni = ab_re - 1.0, ab_im
    cf_re = (nr * a_re + ni * a_im) / den
    cf_im = (ni * a_re - nr * a_im) / den
    bb_re = cf_re[..., None] * b_re - cf_im[..., None] * b_im
    bb_im = cf_re[..., None] * b_im + cf_im[..., None] * b_re
    eye = jnp.eye(gpc, dtype=F32)

    def blk_in(bb):
        w = jnp.einsum("cgph,gk->cghkp", bb.reshape(nchunk, gpc, P, H), eye)
        return w.reshape(nchunk // 2, PAIR_CH, gpc * P).astype(BF16)

    def blk_out(cc):
        w = jnp.einsum("cghp,gk->ckpgh", cc.reshape(nchunk, gpc, H, P), eye)
        w = w.reshape(nchunk // 2, 2, gpc * P, LANES)
        return jnp.swapaxes(w, 1, 2).reshape(nchunk // 2, gpc * P, PAIR_CH).astype(BF16)

    def rows(a):
        a = a.reshape(nchunk // 2, 2, 1, gpc * P)
        return jnp.broadcast_to(a, (nchunk // 2, 2, nb, gpc * P)).reshape(nchunk // 2, 2 * nb, gpc * P)

    return dict(wre=blk_in(bb_re), wim=blk_in(bb_im), cre=blk_out(c_re), cim=blk_out(c_im),
                ar=rows(ab_re), ai=rows(ab_im), d=d[None])


def state_to_rows(h, nb):
    npair = h.shape[1] * h.shape[2] // PAIR_ST // 2
    return jnp.transpose(h.reshape(nb, npair, 2, PAIR_ST), (1, 2, 0, 3)).reshape(npair, 2 * nb, PAIR_ST)


def rows_to_state(h, nb):
    npair = h.shape[0]
    h = jnp.transpose(h.reshape(npair, 2, nb, PAIR_ST), (2, 0, 1, 3))
    return h.reshape(nb, npair * 2 * PAIR_ST // SSM_STATE, SSM_STATE)


def _glu_kernel(ga_ref, gb_ref, w1_ref, w2_ref, b1_ref, b2_ref, x_ref, o_ref, a_s):
    @pl.when(pl.program_id(2) == 0)
    def _():
        a_s[...] = ga_ref[...] + gb_ref[...]

    a = a_s[...]
    z1 = _dot(a, w1_ref[...]) + b1_ref[...]
    z2 = _dot(a, w2_ref[...]) + b2_ref[...]
    o_ref[...] = x_ref[...] + z1 * jax.nn.sigmoid(z2)


def glu_call(g2, w_glu, b_glu, x2, C, R, tm, tn, sample):
    d = w_glu.shape[0]
    nj = d // tn
    if sample:
        ga_map = lambda c, i, j: (0, 0)
        gb_map = lambda c, i, j: (1, 0)
    else:
        ga_map = lambda c, i, j: (i, c)
        gb_map = lambda c, i, j: (i, C + c)
    return pl.pallas_call(
        _glu_kernel, grid=(C, R // tm, nj),
        in_specs=[pl.BlockSpec((tm, d), ga_map),
                  pl.BlockSpec((tm, d), gb_map),
                  pl.BlockSpec((d, tn), lambda c, i, j: (0, j)),
                  pl.BlockSpec((d, tn), lambda c, i, j: (0, nj + j)),
                  pl.BlockSpec((1, tn), lambda c, i, j: (0, j)),
                  pl.BlockSpec((1, tn), lambda c, i, j: (0, nj + j)),
                  pl.BlockSpec((tm, tn), lambda c, i, j: (i, c * nj + j))],
        out_specs=pl.BlockSpec((tm, tn), lambda c, i, j: (i, c * nj + j)),
        out_shape=jax.ShapeDtypeStruct(x2.shape, F32),
        scratch_shapes=[pltpu.VMEM((tm, d), BF16)],
        compiler_params=_cparams(("parallel", "parallel", "arbitrary")),
        name="s5_glu")(g2, g2, w_glu, w_glu, b_glu, b_glu, x2)


def _ffn_kernel(x_ref, g_ref, wg_ref, wu_ref, wd_ref, o_ref, hn_s, acc_s, *, nf):
    f = pl.program_id(2)

    @pl.when(f == 0)
    def _():
        hn_s[...] = _rms(x_ref[...], g_ref[...]).astype(BF16)
        acc_s[...] = jnp.zeros_like(acc_s)

    h = hn_s[...]
    a = _dot(h, wg_ref[...])
    u = _dot(h, wu_ref[...])
    act = (a * jax.nn.sigmoid(a) * u).astype(BF16)
    acc_s[...] += _dot(act, wd_ref[...])

    @pl.when(f == nf - 1)
    def _():
        o_ref[...] = x_ref[...] + acc_s[...]


def ffn_call(x2, g, wg, wu, wd, C, R, tm, tf):
    d, ff = wg.shape
    nf = ff // tf
    return pl.pallas_call(
        functools.partial(_ffn_kernel, nf=nf), grid=(C, R // tm, nf),
        in_specs=[pl.BlockSpec((tm, d), lambda c, i, f: (i, c)),
                  pl.BlockSpec((1, d), lambda c, i, f: (0, 0)),
                  pl.BlockSpec((d, tf), lambda c, i, f: (0, f)),
                  pl.BlockSpec((d, tf), lambda c, i, f: (0, f)),
                  pl.BlockSpec((tf, d), lambda c, i, f: (f, 0))],
        out_specs=pl.BlockSpec((tm, d), lambda c, i, f: (i, c)),
        out_shape=jax.ShapeDtypeStruct(x2.shape, F32),
        scratch_shapes=[pltpu.VMEM((tm, d), BF16), pltpu.VMEM((tm, d), F32)],
        compiler_params=_cparams(("parallel", "parallel", "arbitrary")),
        name="ffn")(x2, g, wg, wu, wd)


def _ple_kernel(x_ref, g_ref, p_ref, wg_ref, wp_ref, o_ref, hn_s, *, tn):
    j = pl.program_id(2)

    @pl.when(j == 0)
    def _():
        hn_s[...] = _rms(x_ref[...], g_ref[...]).astype(BF16)

    gate = jax.nn.sigmoid(_dot(hn_s[...], wg_ref[...]))
    proj = _dot(p_ref[0].astype(BF16), wp_ref[...])
    o_ref[...] = x_ref[:, pl.ds(pl.multiple_of(j * tn, tn), tn)] + gate * proj


def ple_call(x2, g, p3, wg, wp, C, R, tm, tn):
    d = wg.shape[0]
    pd = wp.shape[0]
    nj = d // tn
    return pl.pallas_call(
        functools.partial(_ple_kernel, tn=tn), grid=(C, R // tm, nj),
        in_specs=[pl.BlockSpec((tm, d), lambda c, i, j: (i, c)),
                  pl.BlockSpec((1, d), lambda c, i, j: (0, 0)),
                  pl.BlockSpec((1, tm, pd), lambda c, i, j: (c, i, 0)),
                  pl.BlockSpec((d, tn), lambda c, i, j: (0, j)),
                  pl.BlockSpec((pd, tn), lambda c, i, j: (0, j))],
        out_specs=pl.BlockSpec((tm, tn), lambda c, i, j: (i, c * nj + j)),
        out_shape=jax.ShapeDtypeStruct(x2.shape, F32),
        scratch_shapes=[pltpu.VMEM((tm, d), BF16)],
        compiler_params=_cparams(("parallel", "parallel", "arbitrary")),
        name="ple")(x2, g, p3, wg, wp)


def _kv_kernel(x_ref, g_ref, w_ref, cos_ref, sin_ref, *rest):
    outs, hn_s = rest[:6], rest[6]
    j = pl.program_id(2)

    @pl.when(j == 0)
    def _():
        hn_s[...] = _rms(x_ref[...], g_ref[...]).astype(BF16)

    z = _dot(hn_s[...], w_ref[...])
    for k, o_ref in enumerate(outs):
        @pl.when(j == k)
        def _(k=k, o_ref=o_ref):
            if k in (2, 4):
                for h in range(N_KV_HEADS):
                    sl = slice(h * HEAD_DIM, (h + 1) * HEAD_DIM)
                    o_ref[0, :, sl] = _rope(z[:, sl], cos_ref[...], sin_ref[...])
            else:
                o_ref[0] = z


def kv_call(x2, g, kv_w, cosf, sinf, C, R, tm):
    d = kv_w.shape[0]
    out = jax.ShapeDtypeStruct((C, R, KV_LANES), F32)
    return pl.pallas_call(
        _kv_kernel, grid=(C, R // tm, 6),
        in_specs=[pl.BlockSpec((tm, d), lambda c, i, j: (i, c)),
                  pl.BlockSpec((1, d), lambda c, i, j: (0, 0)),
                  pl.BlockSpec((d, KV_LANES), lambda c, i, j: (0, j)),
                  pl.BlockSpec((tm, HEAD_DIM), lambda c, i, j: (i, 0)),
                  pl.BlockSpec((tm, HEAD_DIM), lambda c, i, j: (i, 0))],
        out_specs=[pl.BlockSpec((1, tm, KV_LANES), lambda c, i, j: (c, i, 0))] * 6,
        out_shape=[out] * 6,
        scratch_shapes=[pltpu.VMEM((tm, d), BF16)],
        compiler_params=_cparams(("parallel", "parallel", "arbitrary")),
        name="kv_proj")(x2, g, kv_w, cosf, sinf)


def _q_kernel(x_ref, g_ref, w_ref, wgt_ref, bgt_ref, cos_ref, sin_ref, q_ref, gate_ref, hn_s):
    j = pl.program_id(2)

    @pl.when(j == 0)
    def _():
        hn = _rms(x_ref[...], g_ref[...]).astype(BF16)
        hn_s[...] = hn
        gate_ref[0] = jax.nn.sigmoid(_dot(hn, wgt_ref[...]) + bgt_ref[...])

    z = _dot(hn_s[...], w_ref[...])
    for h in range(GQA_REP):
        q_ref[0, h] = _rope(z[:, h * HEAD_DIM:(h + 1) * HEAD_DIM], cos_ref[...], sin_ref[...]).astype(BF16)


def q_call(x2, g, w_q, w_gate, b_gate, cosf, sinf, C, R, tm):
    d = w_q.shape[0]
    tn = GQA_REP * HEAD_DIM
    return pl.pallas_call(
        _q_kernel, grid=(C, R // tm, N_KV_HEADS),
        in_specs=[pl.BlockSpec((tm, d), lambda c, i, j: (i, c)),
                  pl.BlockSpec((1, d), lambda c, i, j: (0, 0)),
                  pl.BlockSpec((d, tn), lambda c, i, j: (0, j)),
                  pl.BlockSpec((d, LANES), lambda c, i, j: (0, 0)),
                  pl.BlockSpec((1, LANES), lambda c, i, j: (0, 0)),
                  pl.BlockSpec((tm, HEAD_DIM), lambda c, i, j: (i, 0)),
                  pl.BlockSpec((tm, HEAD_DIM), lambda c, i, j: (i, 0))],
        out_specs=[pl.BlockSpec((1, GQA_REP, tm, HEAD_DIM), lambda c, i, j: (c, j, i, 0)),
                   pl.BlockSpec((1, tm, LANES), lambda c, i, j: (c, i, 0))],
        out_shape=[jax.ShapeDtypeStruct((C, N_HEADS, R, HEAD_DIM), BF16),
                   jax.ShapeDtypeStruct((C, R, LANES), F32)],
        scratch_shapes=[pltpu.VMEM((tm, d), BF16)],
        compiler_params=_cparams(("parallel", "parallel", "arbitrary")),
        name="q_proj")(x2, g, w_q, w_gate, b_gate, cosf, sinf)


def _oproj_kernel(o_ref, w_ref, x_ref, out_ref):
    out_ref[...] = x_ref[...] + _dot(o_ref[0], w_ref[...])


def oproj_call(o3, w_o, x2, C, R, tm, tn):
    d = w_o.shape[0]
    nj = w_o.shape[1] // tn
    return pl.pallas_call(
        _oproj_kernel, grid=(C, R // tm, nj),
        in_specs=[pl.BlockSpec((1, tm, d), lambda c, i, j: (c, i, 0)),
                  pl.BlockSpec((d, tn), lambda c, i, j: (0, j)),
                  pl.BlockSpec((tm, tn), lambda c, i, j: (i, c * nj + j))],
        out_specs=pl.BlockSpec((tm, tn), lambda c, i, j: (i, c * nj + j)),
        out_shape=jax.ShapeDtypeStruct(x2.shape, F32),
        compiler_params=_cparams(("parallel", "parallel", "arbitrary")),
        name="o_proj")(o3, w_o, x2)


def _compress_kernel(*refs, npage, rope, nslab, upb, slab, paged):
    if paged:
        refs = refs[1:]
    pages = refs[:npage]
    w1_ref, b1_ref, w2_ref, cos_ref, sin_ref, o_ref, p_s, stage_s = refs[npage:]
    i = pl.program_id(1)
    rpp = slab // npage
    for g in range(N_KV_HEADS):
        lanes = slice(g * HEAD_DIM, (g + 1) * HEAD_DIM)
        for r, pg in enumerate(pages):
            stage_s[r * rpp:(r + 1) * rpp, :] = pg[0, :, lanes]
        cols = [stage_s[pl.ds(s, upb, stride=CMP_STRIDE), :] for s in range(CMP_STRIDE)]
        lhs = jnp.concatenate(cols, axis=1).astype(BF16)
        p_s[g, pl.ds(pl.multiple_of(i * upb, upb), upb), :] = _dot(lhs, w1_ref[...])

    @pl.when(i == nslab - 1)
    def _():
        nu = nslab * upb
        for g in range(N_KV_HEADS):
            p_s[g, nu:nu + 8, :] = jnp.zeros((8, 2 * HEAD_DIM), F32)
            h = b1_ref[...] + p_s[g, 0:nu, 0:HEAD_DIM] + p_s[g, 1:nu + 1, HEAD_DIM:2 * HEAD_DIM]
            c = _dot(jax.nn.gelu(h).astype(BF16), w2_ref[...])
            if rope:
                c = _rope(c, cos_ref[...], sin_ref[...])
            o_ref[0, g] = c.astype(BF16)


def compress_call(src, page_table, w1r, b1, w2, cosf, sinf, rope, nbatch, nslab, slab):
    upb = slab // CMP_STRIDE
    nu = nslab * upb
    paged = page_table is not None
    npage = slab // PAGE_SIZE if paged else 1
    const = lambda b, i, *_: (0, 0)
    if paged:
        page_specs = [pl.BlockSpec((1, PAGE_SIZE, KV_LANES),
                                   functools.partial(lambda b, i, pt, r: (pt[b, i * npage + r], 0, 0), r=r))
                      for r in range(npage)]
    else:
        page_specs = [pl.BlockSpec((1, slab, KV_LANES), lambda b, i: (b, i, 0))]
    in_specs = page_specs + [pl.BlockSpec((CMP_STRIDE * HEAD_DIM, 2 * HEAD_DIM), const),
                             pl.BlockSpec((1, HEAD_DIM), const),
                             pl.BlockSpec((HEAD_DIM, HEAD_DIM), const),
                             pl.BlockSpec((nu, HEAD_DIM), const),
                             pl.BlockSpec((nu, HEAD_DIM), const)]
    out_specs = pl.BlockSpec((1, N_KV_HEADS, nu, HEAD_DIM), lambda b, i, *_: (b, 0, 0, 0))
    kern = functools.partial(_compress_kernel, npage=npage, rope=rope, nslab=nslab, upb=upb, slab=slab, paged=paged)
    scratch = [pltpu.VMEM((N_KV_HEADS, nu + 8, 2 * HEAD_DIM), F32), pltpu.VMEM((slab, HEAD_DIM), F32)]
    out_shape = jax.ShapeDtypeStruct((nbatch, N_KV_HEADS, nu, HEAD_DIM), BF16)
    args = [src] * npage + [w1r, b1, w2, cosf, sinf]
    if paged:
        gs = pltpu.PrefetchScalarGridSpec(num_scalar_prefetch=1, grid=(nbatch, nslab), in_specs=in_specs,
                                          out_specs=out_specs, scratch_shapes=scratch)
        return pl.pallas_call(kern, grid_spec=gs, out_shape=out_shape,
                              compiler_params=_cparams(("parallel", "arbitrary")),
                              name="compress_paged")(page_table, *args)
    return pl.pallas_call(kern, grid=(nbatch, nslab), in_specs=in_specs, out_specs=out_specs,
                          out_shape=out_shape, scratch_shapes=scratch,
                          compiler_params=_cparams(("parallel", "arbitrary")),
                          name="compress_rows")(*args)


def cmp_to_slc_matrix(n_cmp, n_slc):
    units = np.arange(n_cmp)[:, None] + np.arange(CMP_PARTS)[None, :]
    sblk = (units * CMP_STRIDE) // SLC_BLOCK
    m = np.zeros((n_cmp, n_slc), np.float32)
    np.add.at(m, (np.repeat(np.arange(n_cmp), CMP_PARTS), sblk.reshape(-1)), 1.0)
    return m


def _split_bf16(x):
    hi = x.astype(BF16)
    return hi, (x - hi.astype(F32)).astype(BF16)


def _softmax_rows(s, mask):
    s = jnp.where(mask, s, NEG_INF)
    m = jnp.max(s, axis=-1, keepdims=True)
    p = jnp.where(mask, jnp.exp(s - m), 0.0)
    return p / jnp.maximum(jnp.sum(p, axis=-1, keepdims=True), 1e-30)


def _nsa_prompt_kernel(q_ref, kc_ref, vc_ref, sk_ref, sv_ref, wk_ref, wv_ref, gate_ref, mt_ref, o_ref,
                       *, tq, n_cmp, n_slc):
    g = pl.program_id(1)
    qi = pl.program_id(2)
    t0 = qi * tq
    rows = GQA_REP * tq
    q = q_ref[0].reshape(rows, HEAD_DIM)
    qpos = t0 + lax.broadcasted_iota(jnp.int32, (tq, 1), 0)

    ncp = kc_ref.shape[2]
    s = _dot_nt(q, kc_ref[0, 0]) * ATTN_SCALE
    n_id = lax.broadcasted_iota(jnp.int32, (tq, ncp), 1)
    cmask = (n_id * CMP_STRIDE + (CMP_BLOCK - 1) <= qpos) & (n_id < n_cmp)
    p = _softmax_rows(s.reshape(GQA_REP, tq, ncp), cmask[None]).reshape(rows, ncp)
    o_cmp = _dot(p.astype(BF16), vc_ref[0, 0])
    p_grp = p[0:tq] + p[tq:2 * tq] + p[2 * tq:3 * tq] + p[3 * tq:4 * tq]

    p_hi, p_lo = _split_bf16(p_grp)
    sc = _dot_nt(mt_ref[...], p_hi) + _dot_nt(mt_ref[...], p_lo)
    nsp = 32 * ((n_slc + 31) // 32)
    sc = sc[0:nsp]
    j_id = lax.broadcasted_iota(jnp.int32, (nsp, tq), 0)
    tpos = t0 + lax.broadcasted_iota(jnp.int32, (nsp, tq), 1)
    cur = tpos // SLC_BLOCK
    valid = (j_id * SLC_BLOCK <= tpos) & (j_id < n_slc)
    forced = (j_id == 0) | (j_id == cur) | (j_id == cur - 1)
    sc = jnp.where(valid, jnp.where(forced, FORCED_SCORE, sc), NEG_INF)
    cnt = jnp.zeros((nsp, tq), F32)
    for i in range(n_slc):
        ri = sc[i:i + 1, :]
        beats = (ri > sc) | ((ri == sc) & (j_id > i))
        cnt = cnt + jnp.where(beats, 1.0, 0.0)
    sel_t = jnp.where((cnt < float(min(N_SELECT, n_slc))) & (sc > 0.5 * NEG_INF), 1.0, 0.0)
    if nsp < LANES:
        sel_t = jnp.concatenate([sel_t, jnp.zeros((LANES - nsp, tq), F32)], axis=0)
    sel = sel_t.T.astype(BF16)

    def flash(k_ref, v_ref, lo, hi, mask_fn):
        def body(kb, carry):
            m, l, acc = carry
            k0 = pl.multiple_of(kb * tq, tq)
            kblk = k_ref[0, pl.ds(k0, tq), :].astype(BF16)
            vblk = v_ref[0, pl.ds(k0, tq), :].astype(BF16)
            sb = _dot_nt(q, kblk) * ATTN_SCALE
            mk = mask_fn(k0)[None]
            sb = jnp.where(mk, sb.reshape(GQA_REP, tq, tq), NEG_INF).reshape(rows, tq)
            m_new = jnp.maximum(m, jnp.max(sb, axis=-1, keepdims=True))
            pb = jnp.where(mk, jnp.exp(sb - m_new).reshape(GQA_REP, tq, tq), 0.0).reshape(rows, tq)
            alpha = jnp.exp(m - m_new)
            l = alpha * l + jnp.sum(pb, axis=-1, keepdims=True)
            acc = alpha * acc + _dot(pb.astype(BF16), vblk)
            return m_new, l, acc

        init = (jnp.full((rows, 1), NEG_INF, F32), jnp.zeros((rows, 1), F32), jnp.zeros((rows, HEAD_DIM), F32))
        _, l, acc = lax.fori_loop(lo, hi, body, init)
        return acc / jnp.maximum(l, 1e-30)

    kcol = lax.broadcasted_iota(jnp.int32, (tq, tq), 1)

    def slc_mask(k0):
        blk = lax.broadcasted_iota(jnp.int32, (LANES, tq), 0)
        col_blk = (k0 + lax.broadcasted_iota(jnp.int32, (LANES, tq), 1)) // SLC_BLOCK
        expand = jnp.where(blk == col_blk, 1.0, 0.0).astype(BF16)
        chosen = _dot(sel, expand) > 0.5
        return chosen & (k0 + kcol <= qpos)

    def win_mask(k0):
        dist = qpos - (k0 + kcol)
        return (dist >= 0) & (dist < WINDOW)

    o_slc = flash(sk_ref, sv_ref, 0, qi + 1, slc_mask)
    o_win = flash(wk_ref, wv_ref, jnp.maximum(qi - WINDOW // tq, 0), qi + 1, win_mask)

    gates = gate_ref[0]
    lane = lax.broadcasted_iota(jnp.int32, (tq, LANES), 1)

    def gate_col(c):
        return jnp.sum(jnp.where(lane == c, gates, 0.0), axis=-1, keepdims=True)

    for r in range(GQA_REP):
        rs = slice(r * tq, (r + 1) * tq)
        c0 = g * GQA_REP + r
        o = (gate_col(c0) * o_cmp[rs] + gate_col(N_HEADS + c0) * o_slc[rs]
             + gate_col(2 * N_HEADS + c0) * o_win[rs])
        o_ref[0, :, r * HEAD_DIM:(r + 1) * HEAD_DIM] = o.astype(BF16)


def nsa_prompt_call(q, k_cmp, v_cmp, sk, sv, wk, wv, gates, tq):
    B, _, T, _ = q.shape
    ncp = k_cmp.shape[2]
    n_cmp = T // CMP_STRIDE - CMP_PARTS + 1
    n_slc = -(-T // SLC_BLOCK)
    mt = np.zeros((LANES, ncp), np.float32)
    mt[:n_slc, :n_cmp] = cmp_to_slc_matrix(n_cmp, n_slc).T
    mt = jnp.asarray(mt, BF16)
    kv_spec = pl.BlockSpec((1, T, HEAD_DIM), lambda b, g, i: (b, 0, g))
    cmp_spec = pl.BlockSpec((1, 1, ncp, HEAD_DIM), lambda b, g, i: (b, g, 0, 0))
    kern = functools.partial(_nsa_prompt_kernel, tq=tq, n_cmp=n_cmp, n_slc=n_slc)
    return pl.pallas_call(
        kern, grid=(B, N_KV_HEADS, T // tq),
        in_specs=[pl.BlockSpec((1, GQA_REP, tq, HEAD_DIM), lambda b, g, i: (b, g, i, 0)),
                  cmp_spec, cmp_spec, kv_spec, kv_spec, kv_spec, kv_spec,
                  pl.BlockSpec((1, tq, LANES), lambda b, g, i: (b, i, 0)),
                  pl.BlockSpec((LANES, ncp), lambda b, g, i: (0, 0))],
        out_specs=pl.BlockSpec((1, tq, GQA_REP * HEAD_DIM), lambda b, g, i: (b, i, g)),
        out_shape=jax.ShapeDtypeStruct((B, T, N_HEADS * HEAD_DIM), BF16),
        compiler_params=_cparams(("parallel", "parallel", "arbitrary")),
        name="nsa_prompt")(q, k_cmp, v_cmp, sk, sv, wk, wv, gates, mt)


def _nsa_sample_select_kernel(q_ref, kc_ref, vc_ref, m_ref, ocmp_ref, idx_ref, ok_ref, *, nb, n_cmp, n_slc, qpos):
    rows = GQA_REP * nb
    q = q_ref[0].astype(F32).reshape(rows, HEAD_DIM).astype(BF16)
    ncp = kc_ref.shape[2]
    row_b = lax.broadcasted_iota(jnp.int32, (rows, 1), 0) % nb
    n_id = lax.broadcasted_iota(jnp.int32, (rows, ncp), 1)
    cmask = (n_id * CMP_STRIDE + (CMP_BLOCK - 1) <= qpos) & (n_id < n_cmp)
    o_acc = jnp.zeros((rows, HEAD_DIM), F32)
    pg = jnp.zeros((nb, ncp), F32)
    b_id = lax.broadcasted_iota(jnp.int32, (nb, 1), 0)
    for b in range(nb):
        s = _dot_nt(q, kc_ref[b, 0]) * ATTN_SCALE
        p = _softmax_rows(s, cmask)
        o_b = _dot(p.astype(BF16), vc_ref[b, 0])
        o_acc = o_acc + jnp.where(row_b == b, o_b, 0.0)
        p_sum = jnp.sum(p.reshape(GQA_REP, nb, ncp), axis=0)
        pg = pg + jnp.where(b_id == b, p_sum, 0.0)
    ocmp_ref[0] = o_acc.reshape(GQA_REP, nb, HEAD_DIM)

    p_hi, p_lo = _split_bf16(pg)
    sc = _dot(p_hi, m_ref[...]) + _dot(p_lo, m_ref[...])
    nsp = sc.shape[1]
    j_id = lax.broadcasted_iota(jnp.int32, (nb, nsp), 1)
    cur = qpos // SLC_BLOCK
    valid = (j_id * SLC_BLOCK <= qpos) & (j_id < n_slc)
    forced = (j_id == 0) | (j_id == cur) | (j_id == cur - 1)
    sc = jnp.where(valid, jnp.where(forced, FORCED_SCORE, sc), NEG_INF)
    lane = lax.broadcasted_iota(jnp.int32, (nb, LANES), 1)
    idx = jnp.zeros((nb, LANES), jnp.int32)
    okv = jnp.zeros((nb, LANES), jnp.int32)
    for i in range(min(N_SELECT, n_slc)):
        m = jnp.max(sc, axis=-1, keepdims=True)
        first = jnp.min(jnp.where(sc == m, j_id.astype(F32), float(nsp)), axis=-1, keepdims=True).astype(jnp.int32)
        idx = jnp.where(lane == i, first, idx)
        okv = jnp.where(lane == i, jnp.where(m > 0.5 * NEG_INF, 1, 0), okv)
        sc = jnp.where(j_id == first, -3.0e38, sc)
    idx_ref[0] = idx
    ok_ref[0] = okv


def nsa_sample_select_call(q, k_cmp, v_cmp, qpos):
    nb = q.shape[2]
    ncp = k_cmp.shape[2]
    n_cmp = (qpos + 1) // CMP_STRIDE - CMP_PARTS + 1
    n_slc = -(-(qpos + 1) // SLC_BLOCK)
    nsp = LANES * (-(-n_slc // LANES))
    m = np.zeros((ncp, nsp), np.float32)
    m[:n_cmp, :n_slc] = cmp_to_slc_matrix(n_cmp, n_slc)
    m = jnp.asarray(m, BF16)
    cmp_spec = pl.BlockSpec((nb, 1, ncp, HEAD_DIM), lambda g: (0, g, 0, 0))
    kern = functools.partial(_nsa_sample_select_kernel, nb=nb, n_cmp=n_cmp, n_slc=n_slc, qpos=qpos)
    return pl.pallas_call(
        kern, grid=(N_KV_HEADS,),
        in_specs=[pl.BlockSpec((1, GQA_REP, nb, HEAD_DIM), lambda g: (0, g, 0, 0)),
                  cmp_spec, cmp_spec,
                  pl.BlockSpec((ncp, nsp), lambda g: (0, 0))],
        out_specs=[pl.BlockSpec((1, GQA_REP, nb, HEAD_DIM), lambda g: (g, 0, 0, 0)),
                   pl.BlockSpec((1, nb, LANES), lambda g: (g, 0, 0)),
                   pl.BlockSpec((1, nb, LANES), lambda g: (g, 0, 0))],
        out_shape=[jax.ShapeDtypeStruct((N_KV_HEADS, GQA_REP, nb, HEAD_DIM), F32),
                   jax.ShapeDtypeStruct((N_KV_HEADS, nb, LANES), jnp.int32),
                   jax.ShapeDtypeStruct((N_KV_HEADS, nb, LANES), jnp.int32)],
        compiler_params=_cparams(("parallel",)),
        name="nsa_sample_select")(q, k_cmp, v_cmp, m)


def _nsa_sample_attend_kernel(idx_ref, ok_ref, pt_ref, q_ref, *rest, nb, nsel, n_past_blk, qpos):
    kblks, vblks = rest[:nsel], rest[nsel:2 * nsel]
    (wk_ref, wv_ref, skn_ref, svn_ref, wkn_ref, wvn_ref, ocmp_ref, gate_ref, o_ref) = rest[2 * nsel:]
    g = pl.program_id(0)
    b = pl.program_id(1)
    rows = GQA_REP * nb
    qf = q_ref[0].astype(F32).reshape(rows, HEAD_DIM)
    q = qf.astype(BF16)
    row_b = lax.broadcasted_iota(jnp.int32, (rows, 1), 0) % nb

    @pl.when(b == 0)
    def _():
        o_ref[...] = jnp.zeros_like(o_ref)

    kcat = jnp.concatenate([kb[0, 0] for kb in kblks], axis=0).astype(BF16)
    vcat = jnp.concatenate([vb[0, 0] for vb in vblks], axis=0).astype(BF16)
    nk = nsel * SLC_BLOCK
    slot = lax.broadcasted_iota(jnp.int32, (1, nk), 1) // SLC_BLOCK
    within = lax.broadcasted_iota(jnp.int32, (1, nk), 1) % SLC_BLOCK
    kpos = jnp.zeros((1, nk), jnp.int32)
    kok = jnp.zeros((1, nk), jnp.int32)
    new_ok = jnp.zeros((1, 1), jnp.int32)
    base = (g * nb + b) * LANES
    for s in range(nsel):
        bi = idx_ref[base + s]
        oks = ok_ref[base + s]
        kpos = jnp.where(slot == s, bi * SLC_BLOCK + within, kpos)
        kok = jnp.where(slot == s, jnp.where(bi < n_past_blk, oks, 0), kok)
        new_ok = jnp.maximum(new_ok, jnp.where(bi * SLC_BLOCK <= qpos, jnp.where(bi >= n_past_blk, oks, 0), 0))
    mask = (kok > 0) & (kpos <= qpos)
    s_past = jnp.where(mask, _dot_nt(q, kcat) * ATTN_SCALE, NEG_INF)
    has_new = new_ok > 0
    s_new = jnp.where(has_new, jnp.sum(qf * skn_ref[0], axis=-1, keepdims=True) * ATTN_SCALE,
                      NEG_INF)
    m = jnp.maximum(jnp.max(s_past, axis=-1, keepdims=True), s_new)
    p_past = jnp.where(mask, jnp.exp(s_past - m), 0.0)
    p_new = jnp.where(has_new, jnp.exp(s_new - m), 0.0)
    den = jnp.maximum(jnp.sum(p_past, axis=-1, keepdims=True) + p_new, 1e-30)
    o_slc = (_dot(p_past.astype(BF16), vcat)
             + p_new * svn_ref[0]) / den

    nw = wk_ref.shape[1]
    wpos = qpos - nw + lax.broadcasted_iota(jnp.int32, (1, nw), 1)
    wdist = qpos - wpos
    wmask = (wpos >= 0) & (wdist >= 0) & (wdist < WINDOW)
    s_w = jnp.where(wmask, _dot_nt(q, wk_ref[0].astype(BF16)) * ATTN_SCALE, NEG_INF)
    s_wn = jnp.sum(qf * wkn_ref[0], axis=-1, keepdims=True) * ATTN_SCALE
    mw = jnp.maximum(jnp.max(s_w, axis=-1, keepdims=True), s_wn)
    p_w = jnp.where(wmask, jnp.exp(s_w - mw), 0.0)
    p_wn = jnp.exp(s_wn - mw)
    den_w = jnp.maximum(jnp.sum(p_w, axis=-1, keepdims=True) + p_wn, 1e-30)
    o_win = (_dot(p_w.astype(BF16), wv_ref[0].astype(BF16))
             + p_wn * wvn_ref[0]) / den_w

    gates = jnp.concatenate([gate_ref[...]] * GQA_REP, axis=0)
    lane = lax.broadcasted_iota(jnp.int32, (rows, LANES), 1)
    row_r = lax.broadcasted_iota(jnp.int32, (rows, 1), 0) // nb

    def gate_col(branch):
        c = branch * N_HEADS + g * GQA_REP + row_r
        return jnp.sum(jnp.where(lane == c, gates, 0.0), axis=-1, keepdims=True)

    o = (gate_col(0) * ocmp_ref[0].reshape(rows, HEAD_DIM) + gate_col(1) * o_slc + gate_col(2) * o_win)
    o_ref[0] += jnp.where(row_b == b, o, 0.0).reshape(GQA_REP, nb, HEAD_DIM)


def nsa_sample_attend_call(idx, ok, page_table, q, pool_k, pool_v, win_k, win_v, sk_new, sv_new, wk_new, wv_new,
                           o_cmp, gates, qpos):
    nb = q.shape[2]
    nsel = min(N_SELECT, -(-(qpos + 1) // SLC_BLOCK))
    n_past_blk = page_table.shape[1] * PAGE_SIZE // SLC_BLOCK
    half = PAGE_SIZE // SLC_BLOCK

    def blk_map(g, b, idx_r, ok_r, pt_r, s):
        bi = jnp.minimum(idx_r[(g * nb + b) * LANES + s], n_past_blk - 1)
        return (pt_r[b, bi // half], bi % half, 0, g)

    blk_specs = [pl.BlockSpec((1, 1, SLC_BLOCK, HEAD_DIM), functools.partial(blk_map, s=s)) for s in range(nsel)]
    win_spec = pl.BlockSpec((1, win_k.shape[1], HEAD_DIM), lambda g, b, *_: (b, 0, g))
    new_spec = pl.BlockSpec((1, 1, HEAD_DIM), lambda g, b, *_: (b * N_KV_HEADS + g, 0, 0))
    qo_spec = pl.BlockSpec((1, GQA_REP, nb, HEAD_DIM), lambda g, b, *_: (0, g, 0, 0))
    in_specs = ([qo_spec] + blk_specs + blk_specs + [win_spec, win_spec, new_spec, new_spec, new_spec, new_spec,
                pl.BlockSpec((1, GQA_REP, nb, HEAD_DIM), lambda g, b, *_: (g, 0, 0, 0)),
                pl.BlockSpec((nb, LANES), lambda g, b, *_: (0, 0))])
    gs = pltpu.PrefetchScalarGridSpec(
        num_scalar_prefetch=3, grid=(N_KV_HEADS, nb), in_specs=in_specs,
        out_specs=pl.BlockSpec((1, GQA_REP, nb, HEAD_DIM), lambda g, b, *_: (g, 0, 0, 0)))
    kern = functools.partial(_nsa_sample_attend_kernel, nb=nb, nsel=nsel, n_past_blk=n_past_blk, qpos=qpos)
    return pl.pallas_call(
        kern, grid_spec=gs,
        out_shape=jax.ShapeDtypeStruct((N_KV_HEADS, GQA_REP, nb, HEAD_DIM), F32),
        compiler_params=_cparams(("parallel", "arbitrary")),
        name="nsa_sample_attend")(idx.reshape(-1), ok.reshape(-1), page_table, q,
                                  *([pool_k] * nsel), *([pool_v] * nsel), win_k, win_v,
                                  sk_new, sv_new, wk_new, wv_new, o_cmp, gates)


def _rope_tables(pos):
    half = HEAD_DIM // 2
    inv_freq = ROPE_THETA ** (-jnp.arange(half, dtype=F32) / half)
    ang = pos.astype(F32)[:, None] * inv_freq[None, :]
    cos, sin = jnp.cos(ang), jnp.sin(ang)
    return jnp.concatenate([cos, cos], axis=-1), jnp.concatenate([-sin, sin], axis=-1)


def _trunk(x2, p4, pos, h0r, h0i, w, lay, sample, kv_ctx):
    C, R, tm, nb, T, tc = lay
    depth = w["norm_mix_g"].shape[0]
    n_a = w["ssm_a_re"].shape[0]
    cosf, sinf = _rope_tables(pos)
    ssm_re, ssm_im = [], []
    kv = None
    tn = 512
    for i in range(depth):
        g_mix = w["norm_mix_g"][i][None]
        if i < n_a:
            hn2 = norm_dup(x2, g_mix, C, R, tm, sample)
            prm = s5_params(w["ssm_a_re"][i], w["ssm_a_im"][i], w["ssm_log_dt"][i], w["ssm_b_re"][i],
                            w["ssm_b_im"][i], w["ssm_c_re"][i], w["ssm_c_im"][i], w["ssm_d"][i], nb)
            g2, hr, hi = s5_scan(hn2.reshape(T * 2 * nb, -1), prm, h0r[i], h0i[i], T, nb, tc)
            ssm_re.append(rows_to_state(hr, nb))
            ssm_im.append(rows_to_state(hi, nb))
            g2 = g2 if sample else g2.reshape(R, -1)
            x2 = glu_call(g2, w["ssm_w_glu"][i].astype(BF16), w["ssm_b_glu"][i][None], x2, C, R, tm, tn, sample)
        else:
            j = i - n_a
            wgt = jnp.pad(w["nsa_w_gate"][j], ((0, 0), (0, LANES - 3 * N_HEADS))).astype(BF16)
            bgt = jnp.pad(w["nsa_b_gate"][j], (0, LANES - 3 * N_HEADS))[None]
            q, gates = q_call(x2, g_mix, w["nsa_w_q"][j].astype(BF16), wgt, bgt, cosf, sinf, C, R, tm)
            o3 = kv_ctx["attend"](q, gates, kv)
            x2 = oproj_call(o3, w["nsa_w_o"][j].astype(BF16), x2, C, R, tm, tn)
        x2 = ffn_call(x2, w["norm_ffn_g"][i][None], w["ffn_w_gate"][i].astype(BF16),
                      w["ffn_w_up"][i].astype(BF16), w["ffn_w_down"][i].astype(BF16), C, R, tm, 512)
        x2 = ple_call(x2, w["norm_ple_g"][i][None], p4[i], w["ple_w_gate"][i].astype(BF16),
                      w["ple_w_proj"][i].astype(BF16), C, R, tm, tn)
        if i == n_a - 1:
            kv = kv_ctx["build"](x2, cosf, sinf)
    y = final_norm(x2, w["norm_final_g"][None], C, R, tm)
    return y, jnp.stack(ssm_re), jnp.stack(ssm_im), kv["new"]


def _cmp_weights(w1, b1, w2):
    w1p = w1.reshape(CMP_PARTS, CMP_STRIDE * HEAD_DIM, HEAD_DIM)
    return jnp.concatenate([w1p[0], w1p[1]], axis=1).astype(BF16), b1[None], w2.astype(BF16)


def kernel(x_prompt, x_sample, state_ssm_re, state_ssm_im, cache_cmp_k, cache_cmp_v, cache_slc_k, cache_slc_v, state_win_k, state_win_v, page_table, p_prompt, p_sample, norm_mix_g, norm_ffn_g, norm_ple_g, norm_kv_g, norm_final_g, ssm_a_re, ssm_a_im, ssm_log_dt, ssm_b_re, ssm_b_im, ssm_c_re, ssm_c_im, ssm_d, ssm_w_glu, ssm_b_glu, kv_w, cmp_k_w1, cmp_k_b1, cmp_k_w2, cmp_v_w1, cmp_v_b1, cmp_v_w2, nsa_w_q, nsa_w_gate, nsa_b_gate, nsa_w_o, ffn_w_gate, ffn_w_up, ffn_w_down, ple_w_gate, ple_w_proj):
    w = dict(norm_mix_g=norm_mix_g, norm_ffn_g=norm_ffn_g, norm_ple_g=norm_ple_g, norm_final_g=norm_final_g,
             ssm_a_re=ssm_a_re, ssm_a_im=ssm_a_im, ssm_log_dt=ssm_log_dt, ssm_b_re=ssm_b_re, ssm_b_im=ssm_b_im,
             ssm_c_re=ssm_c_re, ssm_c_im=ssm_c_im, ssm_d=ssm_d, ssm_w_glu=ssm_w_glu, ssm_b_glu=ssm_b_glu,
             nsa_w_q=nsa_w_q, nsa_w_gate=nsa_w_gate, nsa_b_gate=nsa_b_gate, nsa_w_o=nsa_w_o,
             ffn_w_gate=ffn_w_gate, ffn_w_up=ffn_w_up, ffn_w_down=ffn_w_down,
             ple_w_gate=ple_w_gate, ple_w_proj=ple_w_proj)
    B, T, D = x_prompt.shape
    nbs, ts, _ = x_sample.shape
    assert ts == 1
    n_a = ssm_a_re.shape[0]
    past = page_table.shape[1] * PAGE_SIZE
    kvw = kv_w.astype(BF16)
    kvg = norm_kv_g[None]
    ck_w = _cmp_weights(cmp_k_w1, cmp_k_b1, cmp_k_w2)
    cv_w = _cmp_weights(cmp_v_w1, cmp_v_b1, cmp_v_w2)

    tm_p = min(512, T)
    tq = min(256, T)
    pos_p = jnp.arange(T, dtype=jnp.int32)

    def build_prompt(x2, cosf, sinf):
        ck, cv, sk, sv, wk, wv = kv_call(x2, kvg, kvw, cosf, sinf, B, T, tm_p)
        slab = min(SLAB_ROWS, T)
        nslab = T // slab
        nu = T // CMP_STRIDE
        ccos, csin = _rope_tables(jnp.arange(nu, dtype=jnp.int32) * CMP_STRIDE + (CMP_BLOCK - 1))
        k_cmp = compress_call(ck, None, *ck_w, ccos, csin, True, B, nslab, slab)
        v_cmp = compress_call(cv, None, *cv_w, ccos, csin, False, B, nslab, slab)
        wb = min(WINDOW, T)
        r4 = lambda a: a.reshape(B, -1, N_KV_HEADS, HEAD_DIM)
        new = (r4(ck), r4(cv), r4(sk), r4(sv), r4(wk[:, T - wb:]), r4(wv[:, T - wb:]))
        return dict(k_cmp=k_cmp, v_cmp=v_cmp, sk=sk, sv=sv, wk=wk, wv=wv, new=new)

    def attend_prompt(q, gates, kv):
        return nsa_prompt_call(q, kv["k_cmp"], kv["v_cmp"], kv["sk"], kv["sv"], kv["wk"], kv["wv"], gates, tq)

    x2p = to_time_major(x_prompt, tm_p)
    h0 = jnp.zeros((n_a, D // PAIR_CH, 2 * B, PAIR_ST), F32)
    y_p, sre_p, sim_p, new_p = _trunk(x2p, p_prompt, pos_p, h0, h0, w, (B, T, tm_p, B, T, min(128, T)), False,
                                      dict(build=build_prompt, attend=attend_prompt))

    qpos = past
    pos_s = jnp.full((nbs,), past, jnp.int32)

    def build_sample(x2, cosf, sinf):
        ck, cv, sk, sv, wk, wv = kv_call(x2, kvg, kvw, cosf, sinf, 1, nbs, nbs)
        slab = min(SLAB_ROWS, past)
        nslab = past // slab
        nu = past // CMP_STRIDE
        ccos, csin = _rope_tables(jnp.arange(nu, dtype=jnp.int32) * CMP_STRIDE + (CMP_BLOCK - 1))
        pool3 = lambda a: a.reshape(a.shape[0], PAGE_SIZE, KV_LANES)
        k_cmp = compress_call(pool3(cache_cmp_k), page_table, *ck_w, ccos, csin, True, nbs, nslab, slab)
        v_cmp = compress_call(pool3(cache_cmp_v), page_table, *cv_w, ccos, csin, False, nbs, nslab, slab)
        r4 = lambda a: a.reshape(nbs, 1, N_KV_HEADS, HEAD_DIM)
        win_k = jnp.concatenate([state_win_k[:, 1:], r4(wk)], axis=1)
        win_v = jnp.concatenate([state_win_v[:, 1:], r4(wv)], axis=1)
        new = (r4(ck), r4(cv), r4(sk), r4(sv), win_k, win_v)
        return dict(k_cmp=k_cmp, v_cmp=v_cmp, sk=sk, sv=sv, wk=wk, wv=wv, new=new)

    def attend_sample(q, gates, kv):
        o_cmp, idx, ok = nsa_sample_select_call(q, kv["k_cmp"], kv["v_cmp"], qpos)
        half = PAGE_SIZE // SLC_BLOCK
        pool4 = lambda a: a.reshape(a.shape[0], half, SLC_BLOCK, KV_LANES)
        rows1 = lambda a: a.reshape(nbs * N_KV_HEADS, 1, HEAD_DIM)
        wb = state_win_k.shape[1]
        o = nsa_sample_attend_call(idx, ok, page_table, q, pool4(cache_slc_k), pool4(cache_slc_v),
                                   state_win_k.reshape(nbs, wb, KV_LANES), state_win_v.reshape(nbs, wb, KV_LANES),
                                   rows1(kv["sk"]), rows1(kv["sv"]), rows1(kv["wk"]), rows1(kv["wv"]),
                                   o_cmp, gates[0], qpos)
        return jnp.transpose(o, (2, 0, 1, 3)).reshape(1, nbs, N_HEADS * HEAD_DIM).astype(BF16)

    x2s = x_sample.reshape(nbs, D)
    h0r = jnp.stack([state_to_rows(state_ssm_re[i], nbs) for i in range(n_a)])
    h0i = jnp.stack([state_to_rows(state_ssm_im[i], nbs) for i in range(n_a)])
    p4s = p_sample.reshape(p_sample.shape[0], 1, nbs, -1)
    y_s, sre_s, sim_s, new_s = _trunk(x2s, p4s, pos_s, h0r, h0i, w, (1, nbs, nbs, nbs, 1, 1), True,
                                      dict(build=build_sample, attend=attend_sample))
    y_s = y_s.reshape(nbs, 1, D)

    ck_p, cv_p, sk_p, sv_p, wk_p, wv_p = new_p
    ck_s, cv_s, sk_s, sv_s, wk_s, wv_s = new_s
    return (y_p, y_s, sre_p, sim_p, ck_p, cv_p, sk_p, sv_p, wk_p, wv_p,
            sre_s, sim_s, ck_s, cv_s, sk_s, sv_s, wk_s, wv_s)
```

```python
import functools

import numpy as np
import jax
import jax.numpy as jnp
from jax import lax
from jax.experimental import pallas as pl
from jax.experimental.pallas import tpu as pltpu

F32 = jnp.float32
BF16 = jnp.bfloat16

N_HEADS = 16
N_KV_HEADS = 4
GQA_REP = N_HEADS // N_KV_HEADS
HEAD_DIM = 128
SSM_GROUP = 16
SSM_STATE = 64
CMP_BLOCK = 32
CMP_STRIDE = 16
CMP_PARTS = CMP_BLOCK // CMP_STRIDE
SLC_BLOCK = 64
N_SELECT = 16
WINDOW = 512
PAGE_SIZE = 128
ROPE_THETA = 10000.0
ATTN_SCALE = HEAD_DIM ** -0.5
EXP2_SCALE = ATTN_SCALE * float(np.log2(np.e))
NORM_EPS = 1e-6
NEG_INF = -1e30
FORCED_SCORE = 1e9

LANES = 128
KV_LANES = N_KV_HEADS * HEAD_DIM
PAIR_CH = 2 * LANES
PAIR_ST = (LANES // SSM_GROUP) * SSM_STATE
SLAB_ROWS = 2048
VMEM_LIMIT = 56 * 1024 * 1024


def _cparams(sem, vmem=VMEM_LIMIT):
    return pltpu.CompilerParams(dimension_semantics=sem, vmem_limit_bytes=vmem)


def _rms(x, g):
    ms = jnp.mean(x * x, axis=-1, keepdims=True)
    return x * lax.rsqrt(ms + NORM_EPS) * g


def _rope(x, cosf, sinf):
    return x * cosf + pltpu.roll(x, HEAD_DIM // 2, axis=1) * sinf


def _dot(a, b):
    return jnp.dot(a, b, preferred_element_type=F32)


def _dot_nt(a, b):
    return lax.dot_general(a, b, (((1,), (1,)), ((), ())), preferred_element_type=F32)


def _final_norm_kernel(x_ref, g_ref, o_ref):
    o_ref[...] = _rms(x_ref[...], g_ref[...])


def final_norm(x2, g, tm):
    m, d = x2.shape
    return pl.pallas_call(
        _final_norm_kernel, grid=(m // tm,),
        in_specs=[pl.BlockSpec((tm, d), lambda i: (i, 0)),
                  pl.BlockSpec((1, d), lambda i: (0, 0))],
        out_specs=pl.BlockSpec((tm, d), lambda i: (i, 0)),
        out_shape=jax.ShapeDtypeStruct((m, d), F32),
        compiler_params=_cparams(("parallel",)),
        name="final_norm")(x2, g)


def _s5_kernel(x_ref, gn_ref, wre_ref, wim_ref, cre_ref, cim_ref, ar_ref, ai_ref, d_ref, h0r_ref, h0i_ref,
               g_ref, hr_ref, hi_ref, hn_s, lhs_s, xr_s, xi_s, sr_s, si_s, y_s, *, tc, nb, npair):
    @pl.when(pl.program_id(0) == 0)
    def _():
        hr_ref[...] = h0r_ref[...]
        hi_ref[...] = h0i_ref[...]

    r2 = 2 * nb
    rows = tc * r2
    for b in range(nb):
        hn_s[b] = _rms(x_ref[b], gn_ref[...])
    keep = ((lax.broadcasted_iota(jnp.int32, (r2, PAIR_CH), 0) // nb == 1)
            == (lax.broadcasted_iota(jnp.int32, (r2, PAIR_CH), 1) >= LANES))
    lane_hi = lax.broadcasted_iota(jnp.int32, (tc, PAIR_CH), 1) >= LANES

    for k in range(npair):
        cols = slice(k * PAIR_CH, (k + 1) * PAIR_CH)

        def gather(t, c):
            v = hn_s[:, t, cols]
            lhs_s[t] = jnp.where(keep, jnp.concatenate([v, v], axis=0), 0.0)
            return c

        lax.fori_loop(0, tc, gather, 0, unroll=min(8, tc))
        lhs = lhs_s[...].reshape(rows, PAIR_CH).astype(BF16)
        xr_s[...] = _dot(lhs, wre_ref[k]).reshape(tc, r2, PAIR_ST)
        xi_s[...] = _dot(lhs, wim_ref[k]).reshape(tc, r2, PAIR_ST)
        ar = ar_ref[k]
        ai = ai_ref[k]

        def step(t, carry):
            hr, hi = carry
            nr = ar * hr - ai * hi + xr_s[t]
            ni = ar * hi + ai * hr + xi_s[t]
            sr_s[t] = nr
            si_s[t] = ni
            return nr, ni

        hr, hi = lax.fori_loop(0, tc, step, (hr_ref[k], hi_ref[k]), unroll=min(8, tc))
        hr_ref[k] = hr
        hi_ref[k] = hi
        h_re = sr_s[...].reshape(rows, PAIR_ST).astype(BF16)
        h_im = si_s[...].reshape(rows, PAIR_ST).astype(BF16)
        y_s[...] = (_dot(h_re, cre_ref[k]) - _dot(h_im, cim_ref[k])).reshape(tc, r2, PAIR_CH)
        for b in range(nb):
            y = jnp.where(lane_hi, y_s[:, nb + b, :], y_s[:, b, :])
            y = y + d_ref[:, cols] * hn_s[b, :, cols]
            g_ref[b, :, cols] = jax.nn.gelu(y).astype(BF16)


def s5_mixer(x3, gn, prm, h0r, h0i, tc):
    nb, T, d = x3.shape
    r2 = 2 * nb
    npair = d // PAIR_CH
    full3 = lambda s: (0, 0, 0)
    st_shape = (npair, r2, PAIR_ST)
    kern = functools.partial(_s5_kernel, tc=tc, nb=nb, npair=npair)
    return pl.pallas_call(
        kern, grid=(T // tc,),
        in_specs=[pl.BlockSpec((nb, tc, d), lambda s: (0, s, 0)),
                  pl.BlockSpec((1, d), lambda s: (0, 0)),
                  pl.BlockSpec((npair, PAIR_CH, PAIR_ST), full3),
                  pl.BlockSpec((npair, PAIR_CH, PAIR_ST), full3),
                  pl.BlockSpec((npair, PAIR_ST, PAIR_CH), full3),
                  pl.BlockSpec((npair, PAIR_ST, PAIR_CH), full3),
                  pl.BlockSpec(st_shape, full3),
                  pl.BlockSpec(st_shape, full3),
                  pl.BlockSpec((1, d), lambda s: (0, 0)),
                  pl.BlockSpec(st_shape, full3),
                  pl.BlockSpec(st_shape, full3)],
        out_specs=[pl.BlockSpec((nb, tc, d), lambda s: (0, s, 0)),
                   pl.BlockSpec(st_shape, full3),
                   pl.BlockSpec(st_shape, full3)],
        out_shape=[jax.ShapeDtypeStruct((nb, T, d), BF16),
                   jax.ShapeDtypeStruct(st_shape, F32),
                   jax.ShapeDtypeStruct(st_shape, F32)],
        scratch_shapes=[pltpu.VMEM((nb, tc, d), F32),
                        pltpu.VMEM((tc, r2, PAIR_CH), F32)]
                       + [pltpu.VMEM((tc, r2, PAIR_ST), F32)] * 4
                       + [pltpu.VMEM((tc, r2, PAIR_CH), F32)],
        compiler_params=_cparams(("arbitrary",)),
        name="s5_mixer")(x3, gn, prm["wre"], prm["wim"], prm["cre"], prm["cim"],
                         prm["ar"], prm["ai"], prm["d"], h0r, h0i)


def s5_params(a_re, a_im, log_dt, b_re, b_im, c_re, c_im, d, nb):
    G, P, H = b_re.shape
    gpc = LANES // H
    nchunk = G // gpc
    dt = jnp.exp(log_dt)[:, None]
    mag = jnp.exp(dt * a_re)
    ab_re, ab_im = mag * jnp.cos(dt * a_im), mag * jnp.sin(dt * a_im)
    den = a_re * a_re + a_im * a_im
    nr, ni = ab_re - 1.0, ab_im
    cf_re = (nr * a_re + ni * a_im) / den
    cf_im = (ni * a_re - nr * a_im) / den
    bb_re = cf_re[..., None] * b_re - cf_im[..., None] * b_im
    bb_im = cf_re[..., None] * b_im + cf_im[..., None] * b_re
    eye = jnp.eye(gpc, dtype=F32)

    def blk_in(bb):
        w = jnp.einsum("cgph,gk->cghkp", bb.reshape(nchunk, gpc, P, H), eye)
        return w.reshape(nchunk // 2, PAIR_CH, gpc * P).astype(BF16)

    def blk_out(cc):
        w = jnp.einsum("cghp,gk->ckpgh", cc.reshape(nchunk, gpc, H, P), eye)
        w = w.reshape(nchunk // 2, 2, gpc * P, LANES)
        return jnp.swapaxes(w, 1, 2).reshape(nchunk // 2, gpc * P, PAIR_CH).astype(BF16)

    def rows(a):
        a = a.reshape(nchunk // 2, 2, 1, gpc * P)
        return jnp.broadcast_to(a, (nchunk // 2, 2, nb, gpc * P)).reshape(nchunk // 2, 2 * nb, gpc * P)

    return dict(wre=blk_in(bb_re), wim=blk_in(bb_im), cre=blk_out(c_re), cim=blk_out(c_im),
                ar=rows(ab_re), ai=rows(ab_im), d=d[None])


def state_to_rows(h, nb):
    npair = h.shape[1] * h.shape[2] // PAIR_ST // 2
    return jnp.transpose(h.reshape(nb, npair, 2, PAIR_ST), (1, 2, 0, 3)).reshape(npair, 2 * nb, PAIR_ST)


def rows_to_state(h, nb):
    npair = h.shape[0]
    h = jnp.transpose(h.reshape(npair, 2, nb, PAIR_ST), (2, 0, 1, 3))
    return h.reshape(nb, npair * 2 * PAIR_ST // SSM_STATE, SSM_STATE)


def _glu_kernel(a_ref, w1_ref, w2_ref, b1_ref, b2_ref, x_ref, o_ref):
    a = a_ref[...]
    z1 = _dot(a, w1_ref[...]) + b1_ref[...]
    z2 = _dot(a, w2_ref[...]) + b2_ref[...]
    o_ref[...] = x_ref[...] + z1 * jax.nn.sigmoid(z2)


def glu_call(g2, w_glu, b_glu, x2, li, tm, tn):
    m, d = x2.shape
    nj = d // tn
    return pl.pallas_call(
        _glu_kernel, grid=(m // tm, nj),
        in_specs=[pl.BlockSpec((tm, d), lambda i, j: (i, 0)),
                  pl.BlockSpec((None, d, tn), lambda i, j: (li, 0, j)),
                  pl.BlockSpec((None, d, tn), lambda i, j: (li, 0, nj + j)),
                  pl.BlockSpec((None, 1, tn), lambda i, j: (li, 0, j)),
                  pl.BlockSpec((None, 1, tn), lambda i, j: (li, 0, nj + j)),
                  pl.BlockSpec((tm, tn), lambda i, j: (i, j))],
        out_specs=pl.BlockSpec((tm, tn), lambda i, j: (i, j)),
        out_shape=jax.ShapeDtypeStruct(x2.shape, F32),
        compiler_params=_cparams(("parallel", "arbitrary")),
        name="s5_glu")(g2, w_glu, w_glu, b_glu, b_glu, x2)


def _ffn_kernel(x_ref, g_ref, wg_ref, wu_ref, wd_ref, o_ref, hn_s, acc_s, *, nf):
    f = pl.program_id(1)

    @pl.when(f == 0)
    def _():
        hn_s[...] = _rms(x_ref[...], g_ref[...]).astype(BF16)
        acc_s[...] = jnp.zeros_like(acc_s)

    h = hn_s[...]
    a = _dot(h, wg_ref[...])
    u = _dot(h, wu_ref[...])
    act = (a * jax.nn.sigmoid(a) * u).astype(BF16)
    acc_s[...] += _dot(act, wd_ref[...])

    @pl.when(f == nf - 1)
    def _():
        o_ref[...] = x_ref[...] + acc_s[...]


def ffn_call(x2, g, wg, wu, wd, li, tm, tf):
    m, d = x2.shape
    ff = wg.shape[-1]
    nf = ff // tf
    return pl.pallas_call(
        functools.partial(_ffn_kernel, nf=nf), grid=(m // tm, nf),
        in_specs=[pl.BlockSpec((tm, d), lambda i, f: (i, 0)),
                  pl.BlockSpec((None, 1, d), lambda i, f: (li, 0, 0)),
                  pl.BlockSpec((None, d, tf), lambda i, f: (li, 0, f)),
                  pl.BlockSpec((None, d, tf), lambda i, f: (li, 0, f)),
                  pl.BlockSpec((None, tf, d), lambda i, f: (li, f, 0))],
        out_specs=pl.BlockSpec((tm, d), lambda i, f: (i, 0)),
        out_shape=jax.ShapeDtypeStruct(x2.shape, F32),
        scratch_shapes=[pltpu.VMEM((tm, d), BF16), pltpu.VMEM((tm, d), F32)],
        compiler_params=_cparams(("parallel", "arbitrary")),
        name="ffn")(x2, g, wg, wu, wd)


def _ple_kernel(x_ref, g_ref, p_ref, wg_ref, wp_ref, o_ref, hn_s, *, tn):
    j = pl.program_id(1)

    @pl.when(j == 0)
    def _():
        hn_s[...] = _rms(x_ref[...], g_ref[...]).astype(BF16)

    gate = jax.nn.sigmoid(_dot(hn_s[...], wg_ref[...]))
    proj = _dot(p_ref[...].astype(BF16), wp_ref[...])
    o_ref[...] = x_ref[:, pl.ds(pl.multiple_of(j * tn, tn), tn)] + gate * proj


def ple_call(x2, g, p3, wg, wp, li, tm, tn):
    m, d = x2.shape
    pd = wp.shape[1]
    nj = d // tn
    return pl.pallas_call(
        functools.partial(_ple_kernel, tn=tn), grid=(m // tm, nj),
        in_specs=[pl.BlockSpec((tm, d), lambda i, j: (i, 0)),
                  pl.BlockSpec((None, 1, d), lambda i, j: (li, 0, 0)),
                  pl.BlockSpec((None, tm, pd), lambda i, j: (li, i, 0)),
                  pl.BlockSpec((None, d, tn), lambda i, j: (li, 0, j)),
                  pl.BlockSpec((None, pd, tn), lambda i, j: (li, 0, j))],
        out_specs=pl.BlockSpec((tm, tn), lambda i, j: (i, j)),
        out_shape=jax.ShapeDtypeStruct(x2.shape, F32),
        scratch_shapes=[pltpu.VMEM((tm, d), BF16)],
        compiler_params=_cparams(("parallel", "arbitrary")),
        name="ple")(x2, g, p3, wg, wp)


def _kv_kernel(x_ref, g_ref, w_ref, cos_ref, sin_ref, *rest, heads_out):
    outs = rest[:6]
    hn_s = rest[-1]
    j = pl.program_id(1)

    @pl.when(j == 0)
    def _():
        hn_s[...] = _rms(x_ref[...], g_ref[...]).astype(BF16)

    z = _dot(hn_s[...], w_ref[...])
    for k, o_ref in enumerate(outs):
        @pl.when(j == k)
        def _(k=k, o_ref=o_ref):
            for h in range(N_KV_HEADS):
                zh = z[:, h * HEAD_DIM:(h + 1) * HEAD_DIM]
                if k in (2, 4):
                    zh = _rope(zh, cos_ref[...], sin_ref[...])
                o_ref[:, h, :] = zh
                if heads_out and k in (2, 4):
                    rest[6 + (k == 4)][0, h] = zh.astype(BF16)
                if heads_out and k in (3, 5):
                    rest[8 + (k == 5)][0, h] = zh.T.astype(BF16)


def kv_call(x2, g, kv_w, cosf, sinf, C, R, tm, heads_out):
    m, d = x2.shape
    nt = R // tm
    row_out = jax.ShapeDtypeStruct((m, N_KV_HEADS, HEAD_DIM), F32)
    out_shape = [row_out] * 6
    out_specs = [pl.BlockSpec((tm, N_KV_HEADS, HEAD_DIM), lambda i, j: (i, 0, 0))] * 6
    if heads_out:
        out_shape += [jax.ShapeDtypeStruct((C, N_KV_HEADS, R, HEAD_DIM), BF16)] * 2
        out_specs += [pl.BlockSpec((1, N_KV_HEADS, tm, HEAD_DIM), lambda i, j: (i // nt, 0, i % nt, 0))] * 2
        out_shape += [jax.ShapeDtypeStruct((C, N_KV_HEADS, HEAD_DIM, R), BF16)] * 2
        out_specs += [pl.BlockSpec((1, N_KV_HEADS, HEAD_DIM, tm), lambda i, j: (i // nt, 0, 0, i % nt))] * 2
    return pl.pallas_call(
        functools.partial(_kv_kernel, heads_out=heads_out), grid=(m // tm, 6),
        in_specs=[pl.BlockSpec((tm, d), lambda i, j: (i, 0)),
                  pl.BlockSpec((1, d), lambda i, j: (0, 0)),
                  pl.BlockSpec((d, KV_LANES), lambda i, j: (0, j)),
                  pl.BlockSpec((tm, HEAD_DIM), lambda i, j: (i % nt, 0)),
                  pl.BlockSpec((tm, HEAD_DIM), lambda i, j: (i % nt, 0))],
        out_specs=out_specs, out_shape=out_shape,
        scratch_shapes=[pltpu.VMEM((tm, d), BF16)],
        compiler_params=_cparams(("parallel", "arbitrary")),
        name="kv_proj")(x2, g, kv_w, cosf, sinf)


def _q_kernel(x_ref, g_ref, w_ref, wgt_ref, bgt_ref, cos_ref, sin_ref, q_ref, gate_ref, hn_s):
    j = pl.program_id(1)

    @pl.when(j == 0)
    def _():
        hn = _rms(x_ref[...], g_ref[...]).astype(BF16)
        hn_s[...] = hn
        gate_ref[0] = jax.nn.sigmoid(_dot_nt(wgt_ref[...], hn) + bgt_ref[...])

    z = _dot(hn_s[...], w_ref[...])
    for h in range(GQA_REP):
        q_ref[0, h] = _rope(z[:, h * HEAD_DIM:(h + 1) * HEAD_DIM], cos_ref[...], sin_ref[...]).astype(BF16)


def q_call(x2, g, w_q, wgt_t, bgt, cosf, sinf, li, wi, C, R, tm):
    m, d = x2.shape
    nt = R // tm
    tn = GQA_REP * HEAD_DIM
    return pl.pallas_call(
        _q_kernel, grid=(m // tm, N_KV_HEADS),
        in_specs=[pl.BlockSpec((tm, d), lambda i, j: (i, 0)),
                  pl.BlockSpec((None, 1, d), lambda i, j: (li, 0, 0)),
                  pl.BlockSpec((None, d, tn), lambda i, j: (wi, 0, j)),
                  pl.BlockSpec((LANES, d), lambda i, j: (0, 0)),
                  pl.BlockSpec((LANES, 1), lambda i, j: (0, 0)),
                  pl.BlockSpec((tm, HEAD_DIM), lambda i, j: (i % nt, 0)),
                  pl.BlockSpec((tm, HEAD_DIM), lambda i, j: (i % nt, 0))],
        out_specs=[pl.BlockSpec((1, GQA_REP, tm, HEAD_DIM), lambda i, j: (i // nt, j, i % nt, 0)),
                   pl.BlockSpec((1, LANES, tm), lambda i, j: (i // nt, 0, i % nt))],
        out_shape=[jax.ShapeDtypeStruct((C, N_HEADS, R, HEAD_DIM), BF16),
                   jax.ShapeDtypeStruct((C, LANES, R), F32)],
        scratch_shapes=[pltpu.VMEM((tm, d), BF16)],
        compiler_params=_cparams(("parallel", "arbitrary")),
        name="q_proj")(x2, g, w_q, wgt_t, bgt, cosf, sinf)


def _oproj_kernel(o_ref, w_ref, x_ref, out_ref):
    out_ref[...] = x_ref[...] + _dot(o_ref[...], w_ref[...])


def oproj_call(o2, w_o, x2, li, tm, tn):
    m, d = x2.shape
    nj = d // tn
    return pl.pallas_call(
        _oproj_kernel, grid=(m // tm, nj),
        in_specs=[pl.BlockSpec((tm, d), lambda i, j: (i, 0)),
                  pl.BlockSpec((None, d, tn), lambda i, j: (li, 0, j)),
                  pl.BlockSpec((tm, tn), lambda i, j: (i, j))],
        out_specs=pl.BlockSpec((tm, tn), lambda i, j: (i, j)),
        out_shape=jax.ShapeDtypeStruct(x2.shape, F32),
        compiler_params=_cparams(("parallel", "arbitrary")),
        name="o_proj")(o2, w_o, x2)


def _compress_kernel(*refs, npage, rope, nslab, upb, slab, paged, transpose_out):
    if paged:
        refs = refs[1:]
    pages = refs[:npage]
    w1_ref, b1_ref, w2_ref, cos_ref, sin_ref, o_ref, p_s, stage_s = refs[npage:]
    i = pl.program_id(1)
    rpp = slab // npage
    for g in range(N_KV_HEADS):
        for r, pg in enumerate(pages):
            stage_s[r * rpp:(r + 1) * rpp, :] = pg[0, :, g, :]
        cols = [stage_s[pl.ds(s, upb, stride=CMP_STRIDE), :] for s in range(CMP_STRIDE)]
        lhs = jnp.concatenate(cols, axis=1).astype(BF16)
        p_s[g, pl.ds(pl.multiple_of(i * upb, upb), upb), :] = _dot(lhs, w1_ref[...])

    @pl.when(i == nslab - 1)
    def _():
        nu = nslab * upb
        for g in range(N_KV_HEADS):
            p_s[g, nu:nu + 8, :] = jnp.zeros((8, 2 * HEAD_DIM), F32)
            h = b1_ref[...] + p_s[g, 0:nu, 0:HEAD_DIM] + p_s[g, 1:nu + 1, HEAD_DIM:2 * HEAD_DIM]
            a = jax.nn.gelu(h).astype(BF16)
            if transpose_out:
                o_ref[0, g] = _dot_nt(w2_ref[...], a).astype(BF16)
            else:
                c = _dot(a, w2_ref[...])
                if rope:
                    c = _rope(c, cos_ref[...], sin_ref[...])
                o_ref[0, g] = c.astype(BF16)


def compress_call(src, page_table, w1r, b1, w2, cosf, sinf, rope, nbatch, nslab, slab, transpose_out=False):
    upb = slab // CMP_STRIDE
    nu = nslab * upb
    paged = page_table is not None
    npage = slab // PAGE_SIZE if paged else 1
    const = lambda b, i, *_: (0, 0)
    if paged:
        page_specs = [pl.BlockSpec((1, PAGE_SIZE, N_KV_HEADS, HEAD_DIM),
                                   functools.partial(lambda b, i, pt, r: (pt[b, i * npage + r], 0, 0, 0), r=r))
                      for r in range(npage)]
    else:
        page_specs = [pl.BlockSpec((1, slab, N_KV_HEADS, HEAD_DIM), lambda b, i: (b, i, 0, 0))]
    in_specs = page_specs + [pl.BlockSpec((CMP_STRIDE * HEAD_DIM, 2 * HEAD_DIM), const),
                             pl.BlockSpec((1, HEAD_DIM), const),
                             pl.BlockSpec((HEAD_DIM, HEAD_DIM), const),
                             pl.BlockSpec((nu, HEAD_DIM), const),
                             pl.BlockSpec((nu, HEAD_DIM), const)]
    o_dims = (HEAD_DIM, nu) if transpose_out else (nu, HEAD_DIM)
    out_specs = pl.BlockSpec((1, N_KV_HEADS) + o_dims, lambda b, i, *_: (b, 0, 0, 0))
    kern = functools.partial(_compress_kernel, npage=npage, rope=rope, nslab=nslab, upb=upb, slab=slab,
                             paged=paged, transpose_out=transpose_out)
    scratch = [pltpu.VMEM((N_KV_HEADS, nu + 8, 2 * HEAD_DIM), F32), pltpu.VMEM((slab, HEAD_DIM), F32)]
    out_shape = jax.ShapeDtypeStruct((nbatch, N_KV_HEADS) + o_dims, BF16)
    args = [src] * npage + [w1r, b1, w2, cosf, sinf]
    if paged:
        gs = pltpu.PrefetchScalarGridSpec(num_scalar_prefetch=1, grid=(nbatch, nslab), in_specs=in_specs,
                                          out_specs=out_specs, scratch_shapes=scratch)
        return pl.pallas_call(kern, grid_spec=gs, out_shape=out_shape,
                              compiler_params=_cparams(("parallel", "arbitrary")),
                              name="compress_paged")(page_table, *args)
    return pl.pallas_call(kern, grid=(nbatch, nslab), in_specs=in_specs, out_specs=out_specs,
                          out_shape=out_shape, scratch_shapes=scratch,
                          compiler_params=_cparams(("parallel", "arbitrary")),
                          name="compress_rows")(*args)


def cmp_to_slc_matrix(n_cmp, n_slc):
    units = np.arange(n_cmp)[:, None] + np.arange(CMP_PARTS)[None, :]
    sblk = (units * CMP_STRIDE) // SLC_BLOCK
    m = np.zeros((n_cmp, n_slc), np.float32)
    np.add.at(m, (np.repeat(np.arange(n_cmp), CMP_PARTS), sblk.reshape(-1)), 1.0)
    return m


def _split_bf16(x):
    hi = x.astype(BF16)
    return hi, (x - hi.astype(F32)).astype(BF16)


def _softmax_rows(s, mask):
    s = jnp.where(mask, s, NEG_INF)
    m = jnp.max(s, axis=-1, keepdims=True)
    p = jnp.where(mask, jnp.exp(s - m), 0.0)
    return p / jnp.maximum(jnp.sum(p, axis=-1, keepdims=True), 1e-30)


def _nsa_prompt_kernel(q_ref, kc_ref, vct_ref, sk_ref, svt_ref, wk_ref, wvt_ref, gate_ref, mt_ref, o_ref, sel_s,
                       *, tq, n_cmp, n_slc):
    g = pl.program_id(1)
    qi = pl.program_id(2)
    t0 = qi * tq
    rows = GQA_REP * tq
    q = q_ref[0].reshape(rows, HEAD_DIM)

    def tile4(a):
        return jnp.concatenate([a] * GQA_REP, axis=1)

    ncp = kc_ref.shape[2]
    n_id = lax.broadcasted_iota(jnp.int32, (ncp, tq), 0)
    tq_pos = t0 + lax.broadcasted_iota(jnp.int32, (ncp, tq), 1)
    cmask = tile4((n_id * CMP_STRIDE + (CMP_BLOCK - 1) <= tq_pos) & (n_id < n_cmp))
    s = jnp.where(cmask, _dot_nt(kc_ref[0, 0], q) * ATTN_SCALE, NEG_INF)
    m = jnp.max(s, axis=0, keepdims=True)
    p = jnp.where(cmask, jnp.exp(s - m), 0.0)
    p = p / jnp.maximum(jnp.sum(p, axis=0, keepdims=True), 1e-30)
    o_cmp = _dot(vct_ref[0, 0], p.astype(BF16))
    p_grp = p[:, 0:tq] + p[:, tq:2 * tq] + p[:, 2 * tq:3 * tq] + p[:, 3 * tq:4 * tq]

    p_hi, p_lo = _split_bf16(p_grp)
    nsp = sel_s.shape[0]
    sc = (_dot(mt_ref[...], p_hi) + _dot(mt_ref[...], p_lo))[0:nsp]
    j_id = lax.broadcasted_iota(jnp.int32, (nsp, tq), 0)
    tpos = t0 + lax.broadcasted_iota(jnp.int32, (nsp, tq), 1)
    cur = tpos // SLC_BLOCK
    valid = (j_id * SLC_BLOCK <= tpos) & (j_id < n_slc)
    forced = (j_id == 0) | (j_id == cur) | (j_id == cur - 1)
    sc = jnp.where(valid, jnp.where(forced, FORCED_SCORE, sc), NEG_INF)
    cnt = jnp.zeros((nsp, tq), F32)
    for i in range(n_slc):
        ri = sc[i:i + 1, :]
        beats = (ri > sc) | ((ri == sc) & (j_id > i))
        cnt = cnt + jnp.where(beats, 1.0, 0.0)
    sel_s[...] = jnp.where((cnt < float(min(N_SELECT, n_slc))) & (sc > 0.5 * NEG_INF), 1.0, 0.0)

    kp_off = lax.broadcasted_iota(jnp.int32, (tq, tq), 0)
    tq_off = lax.broadcasted_iota(jnp.int32, (tq, tq), 1)

    def slc_bias(k0):
        blk0 = k0 // SLC_BLOCK
        sel = jnp.concatenate([jnp.broadcast_to(sel_s[pl.ds(blk0 + jb, 1), :], (SLC_BLOCK, tq))
                               for jb in range(tq // SLC_BLOCK)], axis=0)
        return jnp.where((sel > 0.5) & (k0 + kp_off <= t0 + tq_off), 0.0, NEG_INF)

    def win_bias(k0):
        dist = (t0 + tq_off) - (k0 + kp_off)
        return jnp.where((dist >= 0) & (dist < WINDOW), 0.0, NEG_INF)

    def flash(k_ref, vt_ref, nblk, bias_fn):
        def body(it, carry):
            m, l, acc = carry
            k0 = pl.multiple_of((qi - it) * tq, tq)
            sb = _dot_nt(k_ref[0, 0, pl.ds(k0, tq), :], q) + tile4(bias_fn(k0))
            m_new = jnp.maximum(m, jnp.max(sb, axis=0, keepdims=True))
            pb = jnp.exp2((sb - m_new) * EXP2_SCALE)
            alpha = jnp.exp2((m - m_new) * EXP2_SCALE)
            l = alpha * l + jnp.sum(pb, axis=0, keepdims=True)
            acc = alpha * acc + _dot(vt_ref[0, 0, :, pl.ds(k0, tq)], pb.astype(BF16))
            return m_new, l, acc

        init = (jnp.full((1, rows), NEG_INF, F32), jnp.zeros((1, rows), F32), jnp.zeros((HEAD_DIM, rows), F32))
        _, l, acc = lax.fori_loop(0, nblk, body, init)
        return acc / jnp.maximum(l, 1e-30)

    o_slc = flash(sk_ref, svt_ref, qi + 1, slc_bias)
    o_win = flash(wk_ref, wvt_ref, jnp.minimum(qi, WINDOW // tq) + 1, win_bias)

    for r in range(GQA_REP):
        cs = slice(r * tq, (r + 1) * tq)
        c0 = g * GQA_REP + r
        o = (gate_ref[0, pl.ds(c0, 1), :] * o_cmp[:, cs] + gate_ref[0, pl.ds(N_HEADS + c0, 1), :] * o_slc[:, cs]
             + gate_ref[0, pl.ds(2 * N_HEADS + c0, 1), :] * o_win[:, cs])
        o_ref[0, :, r * HEAD_DIM:(r + 1) * HEAD_DIM] = o.T.astype(BF16)


def nsa_prompt_call(q, k_cmp, v_cmp_t, sk, sv_t, wk, wv_t, gates_t, tq):
    B, _, T, _ = q.shape
    ncp = k_cmp.shape[2]
    n_cmp = T // CMP_STRIDE - CMP_PARTS + 1
    n_slc = -(-T // SLC_BLOCK)
    nsp = 8 * (-(-n_slc // 8))
    mt = np.zeros((LANES, ncp), np.float32)
    mt[:n_slc, :n_cmp] = cmp_to_slc_matrix(n_cmp, n_slc).T
    mt = jnp.asarray(mt, BF16)
    k_spec = pl.BlockSpec((1, 1, T, HEAD_DIM), lambda b, g, i: (b, g, 0, 0))
    vt_spec = pl.BlockSpec((1, 1, HEAD_DIM, T), lambda b, g, i: (b, g, 0, 0))
    kern = functools.partial(_nsa_prompt_kernel, tq=tq, n_cmp=n_cmp, n_slc=n_slc)
    return pl.pallas_call(
        kern, grid=(B, N_KV_HEADS, T // tq),
        in_specs=[pl.BlockSpec((1, GQA_REP, tq, HEAD_DIM), lambda b, g, i: (b, g, i, 0)),
                  pl.BlockSpec((1, 1, ncp, HEAD_DIM), lambda b, g, i: (b, g, 0, 0)),
                  pl.BlockSpec((1, 1, HEAD_DIM, ncp), lambda b, g, i: (b, g, 0, 0)),
                  k_spec, vt_spec, k_spec, vt_spec,
                  pl.BlockSpec((1, LANES, tq), lambda b, g, i: (b, 0, i)),
                  pl.BlockSpec((LANES, ncp), lambda b, g, i: (0, 0))],
        out_specs=pl.BlockSpec((1, tq, GQA_REP * HEAD_DIM), lambda b, g, i: (b, i, g)),
        out_shape=jax.ShapeDtypeStruct((B, T, N_HEADS * HEAD_DIM), BF16),
        scratch_shapes=[pltpu.VMEM((nsp, tq), F32)],
        compiler_params=_cparams(("parallel", "parallel", "arbitrary")),
        name="nsa_prompt")(q, k_cmp, v_cmp_t, sk, sv_t, wk, wv_t, gates_t, mt)


def _nsa_sample_select_kernel(q_ref, kc_ref, vc_ref, m_ref, ocmp_ref, idx_ref, ok_ref, *, nb, n_cmp, n_slc, qpos):
    rows = GQA_REP * nb
    q = q_ref[0].astype(F32).reshape(rows, HEAD_DIM).astype(BF16)
    ncp = kc_ref.shape[2]
    row_b = lax.broadcasted_iota(jnp.int32, (rows, 1), 0) % nb
    n_id = lax.broadcasted_iota(jnp.int32, (rows, ncp), 1)
    cmask = (n_id * CMP_STRIDE + (CMP_BLOCK - 1) <= qpos) & (n_id < n_cmp)
    o_acc = jnp.zeros((rows, HEAD_DIM), F32)
    pg = jnp.zeros((nb, ncp), F32)
    b_id = lax.broadcasted_iota(jnp.int32, (nb, 1), 0)
    for b in range(nb):
        s = _dot_nt(q, kc_ref[b, 0]) * ATTN_SCALE
        p = _softmax_rows(s, cmask)
        o_b = _dot(p.astype(BF16), vc_ref[b, 0])
        o_acc = o_acc + jnp.where(row_b == b, o_b, 0.0)
        p_sum = jnp.sum(p.reshape(GQA_REP, nb, ncp), axis=0)
        pg = pg + jnp.where(b_id == b, p_sum, 0.0)
    ocmp_ref[0] = o_acc.reshape(GQA_REP, nb, HEAD_DIM)

    p_hi, p_lo = _split_bf16(pg)
    sc = _dot(p_hi, m_ref[...]) + _dot(p_lo, m_ref[...])
    nsp = sc.shape[1]
    j_id = lax.broadcasted_iota(jnp.int32, (nb, nsp), 1)
    cur = qpos // SLC_BLOCK
    valid = (j_id * SLC_BLOCK <= qpos) & (j_id < n_slc)
    forced = (j_id == 0) | (j_id == cur) | (j_id == cur - 1)
    sc = jnp.where(valid, jnp.where(forced, FORCED_SCORE, sc), NEG_INF)
    lane = lax.broadcasted_iota(jnp.int32, (nb, LANES), 1)
    idx = jnp.zeros((nb, LANES), jnp.int32)
    okv = jnp.zeros((nb, LANES), jnp.int32)
    for i in range(min(N_SELECT, n_slc)):
        m = jnp.max(sc, axis=-1, keepdims=True)
        first = jnp.min(jnp.where(sc == m, j_id.astype(F32), float(nsp)), axis=-1, keepdims=True).astype(jnp.int32)
        idx = jnp.where(lane == i, first, idx)
        okv = jnp.where(lane == i, jnp.where(m > 0.5 * NEG_INF, 1, 0), okv)
        sc = jnp.where(j_id == first, -3.0e38, sc)
    idx_ref[0] = idx
    ok_ref[0] = okv


def nsa_sample_select_call(q, k_cmp, v_cmp, qpos):
    nb = q.shape[2]
    ncp = k_cmp.shape[2]
    n_cmp = (qpos + 1) // CMP_STRIDE - CMP_PARTS + 1
    n_slc = -(-(qpos + 1) // SLC_BLOCK)
    nsp = LANES * (-(-n_slc // LANES))
    m = np.zeros((ncp, nsp), np.float32)
    m[:n_cmp, :n_slc] = cmp_to_slc_matrix(n_cmp, n_slc)
    m = jnp.asarray(m, BF16)
    cmp_spec = pl.BlockSpec((nb, 1, ncp, HEAD_DIM), lambda g: (0, g, 0, 0))
    kern = functools.partial(_nsa_sample_select_kernel, nb=nb, n_cmp=n_cmp, n_slc=n_slc, qpos=qpos)
    return pl.pallas_call(
        kern, grid=(N_KV_HEADS,),
        in_specs=[pl.BlockSpec((1, GQA_REP, nb, HEAD_DIM), lambda g: (0, g, 0, 0)),
                  cmp_spec, cmp_spec,
                  pl.BlockSpec((ncp, nsp), lambda g: (0, 0))],
        out_specs=[pl.BlockSpec((1, GQA_REP, nb, HEAD_DIM), lambda g: (g, 0, 0, 0)),
                   pl.BlockSpec((1, nb, LANES), lambda g: (g, 0, 0)),
                   pl.BlockSpec((1, nb, LANES), lambda g: (g, 0, 0))],
        out_shape=[jax.ShapeDtypeStruct((N_KV_HEADS, GQA_REP, nb, HEAD_DIM), F32),
                   jax.ShapeDtypeStruct((N_KV_HEADS, nb, LANES), jnp.int32),
                   jax.ShapeDtypeStruct((N_KV_HEADS, nb, LANES), jnp.int32)],
        compiler_params=_cparams(("parallel",)),
        name="nsa_sample_select")(q, k_cmp, v_cmp, m)


def _nsa_sample_attend_kernel(idx_ref, ok_ref, pt_ref, q_ref, *rest, nb, nsel, n_past_blk, qpos):
    kblks, vblks = rest[:nsel], rest[nsel:2 * nsel]
    (wk_ref, wv_ref, skn_ref, svn_ref, wkn_ref, wvn_ref, ocmp_ref, gate_ref, o_ref) = rest[2 * nsel:]
    g = pl.program_id(0)
    b = pl.program_id(1)
    rows = GQA_REP * nb
    qf = q_ref[0].astype(F32).reshape(rows, HEAD_DIM)
    q = qf.astype(BF16)
    row_b = lax.broadcasted_iota(jnp.int32, (rows, 1), 0) % nb

    @pl.when(b == 0)
    def _():
        o_ref[...] = jnp.zeros_like(o_ref)

    kcat = jnp.concatenate([kb[0, :, g, :] for kb in kblks], axis=0).astype(BF16)
    vcat = jnp.concatenate([vb[0, :, g, :] for vb in vblks], axis=0).astype(BF16)
    nk = nsel * SLC_BLOCK
    slot = lax.broadcasted_iota(jnp.int32, (1, nk), 1) // SLC_BLOCK
    within = lax.broadcasted_iota(jnp.int32, (1, nk), 1) % SLC_BLOCK
    kpos = jnp.zeros((1, nk), jnp.int32)
    kok = jnp.zeros((1, nk), jnp.int32)
    new_ok = jnp.zeros((1, 1), jnp.int32)
    base = (g * nb + b) * LANES
    for s in range(nsel):
        bi = idx_ref[base + s]
        oks = ok_ref[base + s]
        kpos = jnp.where(slot == s, bi * SLC_BLOCK + within, kpos)
        kok = jnp.where(slot == s, jnp.where(bi < n_past_blk, oks, 0), kok)
        new_ok = jnp.maximum(new_ok, jnp.where(bi * SLC_BLOCK <= qpos, jnp.where(bi >= n_past_blk, oks, 0), 0))
    mask = (kok > 0) & (kpos <= qpos)
    s_past = jnp.where(mask, _dot_nt(q, kcat) * ATTN_SCALE, NEG_INF)
    has_new = new_ok > 0
    s_new = jnp.where(has_new, jnp.sum(qf * skn_ref[0], axis=-1, keepdims=True) * ATTN_SCALE,
                      NEG_INF)
    m = jnp.maximum(jnp.max(s_past, axis=-1, keepdims=True), s_new)
    p_past = jnp.where(mask, jnp.exp(s_past - m), 0.0)
    p_new = jnp.where(has_new, jnp.exp(s_new - m), 0.0)
    den = jnp.maximum(jnp.sum(p_past, axis=-1, keepdims=True) + p_new, 1e-30)
    o_slc = (_dot(p_past.astype(BF16), vcat)
             + p_new * svn_ref[0]) / den

    nw = wk_ref.shape[1]
    wpos = qpos - nw + lax.broadcasted_iota(jnp.int32, (1, nw), 1)
    wdist = qpos - wpos
    wmask = (wpos >= 0) & (wdist >= 0) & (wdist < WINDOW)
    s_w = jnp.where(wmask, _dot_nt(q, wk_ref[0, :, g, :].astype(BF16)) * ATTN_SCALE, NEG_INF)
    s_wn = jnp.sum(qf * wkn_ref[0], axis=-1, keepdims=True) * ATTN_SCALE
    mw = jnp.maximum(jnp.max(s_w, axis=-1, keepdims=True), s_wn)
    p_w = jnp.where(wmask, jnp.exp(s_w - mw), 0.0)
    p_wn = jnp.exp(s_wn - mw)
    den_w = jnp.maximum(jnp.sum(p_w, axis=-1, keepdims=True) + p_wn, 1e-30)
    o_win = (_dot(p_w.astype(BF16), wv_ref[0, :, g, :].astype(BF16))
             + p_wn * wvn_ref[0]) / den_w

    gates = jnp.concatenate([gate_ref[...]] * GQA_REP, axis=0)
    lane = lax.broadcasted_iota(jnp.int32, (rows, LANES), 1)
    row_r = lax.broadcasted_iota(jnp.int32, (rows, 1), 0) // nb

    def gate_col(branch):
        c = branch * N_HEADS + g * GQA_REP + row_r
        return jnp.sum(jnp.where(lane == c, gates, 0.0), axis=-1, keepdims=True)

    o = (gate_col(0) * ocmp_ref[0].reshape(rows, HEAD_DIM) + gate_col(1) * o_slc + gate_col(2) * o_win)
    o_ref[0] += jnp.where(row_b == b, o, 0.0).reshape(GQA_REP, nb, HEAD_DIM)


def nsa_sample_attend_call(idx, ok, page_table, q, pool_k, pool_v, win_k, win_v, sk_new, sv_new, wk_new, wv_new,
                           o_cmp, gates, qpos):
    nb = q.shape[2]
    nsel = min(N_SELECT, -(-(qpos + 1) // SLC_BLOCK))
    n_past_blk = page_table.shape[1] * PAGE_SIZE // SLC_BLOCK
    half = PAGE_SIZE // SLC_BLOCK

    def blk_map(g, b, idx_r, ok_r, pt_r, s):
        bi = jnp.minimum(idx_r[(g * nb + b) * LANES + s], n_past_blk - 1)
        return (pt_r[b, bi // half], bi % half, 0, 0)

    blk_specs = [pl.BlockSpec((1, SLC_BLOCK, N_KV_HEADS, HEAD_DIM), functools.partial(blk_map, s=s))
                 for s in range(nsel)]
    win_spec = pl.BlockSpec((1, win_k.shape[1], N_KV_HEADS, HEAD_DIM), lambda g, b, *_: (b, 0, 0, 0))
    new_spec = pl.BlockSpec((1, 1, HEAD_DIM), lambda g, b, *_: (b * N_KV_HEADS + g, 0, 0))
    qo_spec = pl.BlockSpec((1, GQA_REP, nb, HEAD_DIM), lambda g, b, *_: (0, g, 0, 0))
    in_specs = ([qo_spec] + blk_specs + blk_specs + [win_spec, win_spec, new_spec, new_spec, new_spec, new_spec,
                pl.BlockSpec((1, GQA_REP, nb, HEAD_DIM), lambda g, b, *_: (g, 0, 0, 0)),
                pl.BlockSpec((nb, LANES), lambda g, b, *_: (0, 0))])
    gs = pltpu.PrefetchScalarGridSpec(
        num_scalar_prefetch=3, grid=(N_KV_HEADS, nb), in_specs=in_specs,
        out_specs=pl.BlockSpec((1, GQA_REP, nb, HEAD_DIM), lambda g, b, *_: (g, 0, 0, 0)))
    kern = functools.partial(_nsa_sample_attend_kernel, nb=nb, nsel=nsel, n_past_blk=n_past_blk, qpos=qpos)
    return pl.pallas_call(
        kern, grid_spec=gs,
        out_shape=jax.ShapeDtypeStruct((N_KV_HEADS, GQA_REP, nb, HEAD_DIM), F32),
        compiler_params=_cparams(("parallel", "arbitrary")),
        name="nsa_sample_attend")(idx.reshape(-1), ok.reshape(-1), page_table, q,
                                  *([pool_k] * nsel), *([pool_v] * nsel), win_k, win_v,
                                  sk_new, sv_new, wk_new, wv_new, o_cmp, gates)


def _rope_tables(pos):
    half = HEAD_DIM // 2
    inv_freq = ROPE_THETA ** (-jnp.arange(half, dtype=F32) / half)
    ang = pos.astype(F32)[:, None] * inv_freq[None, :]
    cos, sin = jnp.cos(ang), jnp.sin(ang)
    return jnp.concatenate([cos, cos], axis=-1), jnp.concatenate([-sin, sin], axis=-1)


def _trunk(x2, p3, pos, h0r, h0i, w, lay, kv_ctx):
    nb, T, tc, C, R, tm = lay
    depth = w["norm_mix_g"].shape[0]
    n_a = w["ssm_a_re"].shape[0]
    d = x2.shape[1]
    cosf, sinf = _rope_tables(pos)
    ssm_re, ssm_im = [], []
    kv = None
    tn = 512
    for i in range(depth):
        if i < n_a:
            prm = s5_params(w["ssm_a_re"][i], w["ssm_a_im"][i], w["ssm_log_dt"][i], w["ssm_b_re"][i],
                            w["ssm_b_im"][i], w["ssm_c_re"][i], w["ssm_c_im"][i], w["ssm_d"][i], nb)
            g3, hr, hi = s5_mixer(x2.reshape(nb, T, d), w["norm_mix_g"][i][None], prm, h0r[i], h0i[i], tc)
            ssm_re.append(rows_to_state(hr, nb))
            ssm_im.append(rows_to_state(hi, nb))
            x2 = glu_call(g3.reshape(nb * T, d), w["ssm_w_glu"], w["ssm_b_glu"], x2, i, tm, tn)
        else:
            j = i - n_a
            q, gates_t = q_call(x2, w["norm_mix_g3"], w["nsa_w_q"], w["nsa_wgt_t"][j], w["nsa_bgt"][j],
                                cosf, sinf, i, j, C, R, tm)
            o2 = kv_ctx["attend"](q, gates_t, kv)
            x2 = oproj_call(o2, w["nsa_w_o"], x2, j, tm, tn)
        x2 = ffn_call(x2, w["norm_ffn_g3"], w["ffn_w_gate"], w["ffn_w_up"], w["ffn_w_down"], i, tm, 512)
        x2 = ple_call(x2, w["norm_ple_g3"], p3, w["ple_w_gate"], w["ple_w_proj"], i, tm, tn)
        if i == n_a - 1:
            kv = kv_ctx["build"](x2, cosf, sinf)
    y = final_norm(x2, w["norm_final_g"][None], tm)
    return y, jnp.stack(ssm_re), jnp.stack(ssm_im), kv["new"]


def _cmp_weights(w1, b1, w2):
    w1p = w1.reshape(CMP_PARTS, CMP_STRIDE * HEAD_DIM, HEAD_DIM)
    return jnp.concatenate([w1p[0], w1p[1]], axis=1).astype(BF16), b1[None], w2.astype(BF16)


def kernel(x_prompt, x_sample, state_ssm_re, state_ssm_im, cache_cmp_k, cache_cmp_v, cache_slc_k, cache_slc_v, state_win_k, state_win_v, page_table, p_prompt, p_sample, norm_mix_g, norm_ffn_g, norm_ple_g, norm_kv_g, norm_final_g, ssm_a_re, ssm_a_im, ssm_log_dt, ssm_b_re, ssm_b_im, ssm_c_re, ssm_c_im, ssm_d, ssm_w_glu, ssm_b_glu, kv_w, cmp_k_w1, cmp_k_b1, cmp_k_w2, cmp_v_w1, cmp_v_b1, cmp_v_w2, nsa_w_q, nsa_w_gate, nsa_b_gate, nsa_w_o, ffn_w_gate, ffn_w_up, ffn_w_down, ple_w_gate, ple_w_proj):
    B, T, D = x_prompt.shape
    nbs, ts, _ = x_sample.shape
    assert ts == 1
    n_a = ssm_a_re.shape[0]
    past = page_table.shape[1] * PAGE_SIZE
    gate_pad = LANES - nsa_w_gate.shape[-1]
    w = dict(norm_mix_g=norm_mix_g, norm_mix_g3=norm_mix_g[:, None], norm_ffn_g3=norm_ffn_g[:, None],
             norm_ple_g3=norm_ple_g[:, None], norm_final_g=norm_final_g,
             ssm_a_re=ssm_a_re, ssm_a_im=ssm_a_im, ssm_log_dt=ssm_log_dt, ssm_b_re=ssm_b_re, ssm_b_im=ssm_b_im,
             ssm_c_re=ssm_c_re, ssm_c_im=ssm_c_im, ssm_d=ssm_d,
             ssm_w_glu=ssm_w_glu.astype(BF16), ssm_b_glu=ssm_b_glu[:, None],
             nsa_w_q=nsa_w_q.astype(BF16),
             nsa_wgt_t=jnp.pad(jnp.swapaxes(nsa_w_gate, 1, 2), ((0, 0), (0, gate_pad), (0, 0))).astype(BF16),
             nsa_bgt=jnp.pad(nsa_b_gate, ((0, 0), (0, gate_pad)))[..., None],
             nsa_w_o=nsa_w_o.astype(BF16),
             ffn_w_gate=ffn_w_gate.astype(BF16), ffn_w_up=ffn_w_up.astype(BF16), ffn_w_down=ffn_w_down.astype(BF16),
             ple_w_gate=ple_w_gate.astype(BF16), ple_w_proj=ple_w_proj.astype(BF16))
    kvw = kv_w.astype(BF16)
    kvg = norm_kv_g[None]
    ck_w = _cmp_weights(cmp_k_w1, cmp_k_b1, cmp_k_w2)
    cv_w = _cmp_weights(cmp_v_w1, cmp_v_b1, cmp_v_w2)

    tm_p = min(512, T)
    tq = min(256, T)
    pos_p = jnp.arange(T, dtype=jnp.int32)

    def build_prompt(x2, cosf, sinf):
        ck, cv, sk, sv, wk, wv, skh, wkh, svt, wvt = kv_call(x2, kvg, kvw, cosf, sinf, B, T, tm_p, True)
        r4 = lambda a: a.reshape(B, T, N_KV_HEADS, HEAD_DIM)
        ck, cv, sk, sv, wk, wv = [r4(a) for a in (ck, cv, sk, sv, wk, wv)]
        slab = min(SLAB_ROWS, T)
        nslab = T // slab
        nu = T // CMP_STRIDE
        ccos, csin = _rope_tables(jnp.arange(nu, dtype=jnp.int32) * CMP_STRIDE + (CMP_BLOCK - 1))
        k_cmp = compress_call(ck, None, *ck_w, ccos, csin, True, B, nslab, slab)
        v_cmp_t = compress_call(cv, None, cv_w[0], cv_w[1], cv_w[2].T, ccos, csin, False, B, nslab, slab,
                                transpose_out=True)
        wb = min(WINDOW, T)
        new = (ck, cv, sk, sv, wk[:, T - wb:], wv[:, T - wb:])
        return dict(k_cmp=k_cmp, v_cmp_t=v_cmp_t, sk=skh, sv_t=svt, wk=wkh, wv_t=wvt, new=new)

    def attend_prompt(q, gates_t, kv):
        o = nsa_prompt_call(q, kv["k_cmp"], kv["v_cmp_t"], kv["sk"], kv["sv_t"], kv["wk"], kv["wv_t"], gates_t, tq)
        return o.reshape(B * T, N_HEADS * HEAD_DIM)

    h0 = jnp.zeros((n_a, D // PAIR_CH, 2 * B, PAIR_ST), F32)
    y_p, sre_p, sim_p, new_p = _trunk(x_prompt.reshape(B * T, D), p_prompt.reshape(-1, B * T, p_prompt.shape[-1]),
                                      pos_p, h0, h0, w, (B, T, min(128, T), B, T, tm_p),
                                      dict(build=build_prompt, attend=attend_prompt))
    y_p = y_p.reshape(B, T, D)

    qpos = past
    pos_s = jnp.full((nbs,), past, jnp.int32)

    def build_sample(x2, cosf, sinf):
        ck, cv, sk, sv, wk, wv = kv_call(x2, kvg, kvw, cosf, sinf, 1, nbs, nbs, False)
        slab = min(SLAB_ROWS, past)
        nslab = past // slab
        nu = past // CMP_STRIDE
        ccos, csin = _rope_tables(jnp.arange(nu, dtype=jnp.int32) * CMP_STRIDE + (CMP_BLOCK - 1))
        k_cmp = compress_call(cache_cmp_k, page_table, *ck_w, ccos, csin, True, nbs, nslab, slab)
        v_cmp = compress_call(cache_cmp_v, page_table, *cv_w, ccos, csin, False, nbs, nslab, slab)
        r4 = lambda a: a.reshape(nbs, 1, N_KV_HEADS, HEAD_DIM)
        win_k = jnp.concatenate([state_win_k[:, 1:], r4(wk)], axis=1)
        win_v = jnp.concatenate([state_win_v[:, 1:], r4(wv)], axis=1)
        new = (r4(ck), r4(cv), r4(sk), r4(sv), win_k, win_v)
        return dict(k_cmp=k_cmp, v_cmp=v_cmp, sk=sk, sv=sv, wk=wk, wv=wv, new=new)

    def attend_sample(q, gates_t, kv):
        o_cmp, idx, ok = nsa_sample_select_call(q, kv["k_cmp"], kv["v_cmp"], qpos)
        rows1 = lambda a: a.reshape(nbs * N_KV_HEADS, 1, HEAD_DIM)
        o = nsa_sample_attend_call(idx, ok, page_table, q, cache_slc_k, cache_slc_v, state_win_k, state_win_v,
                                   rows1(kv["sk"]), rows1(kv["sv"]), rows1(kv["wk"]), rows1(kv["wv"]),
                                   o_cmp, gates_t[0].T, qpos)
        return jnp.transpose(o, (2, 0, 1, 3)).reshape(nbs, N_HEADS * HEAD_DIM).astype(BF16)

    h0r = jnp.stack([state_to_rows(state_ssm_re[i], nbs) for i in range(n_a)])
    h0i = jnp.stack([state_to_rows(state_ssm_im[i], nbs) for i in range(n_a)])
    y_s, sre_s, sim_s, new_s = _trunk(x_sample.reshape(nbs, D), p_sample.reshape(-1, nbs, p_sample.shape[-1]),
                                      pos_s, h0r, h0i, w, (nbs, 1, 1, 1, nbs, nbs),
                                      dict(build=build_sample, attend=attend_sample))
    y_s = y_s.reshape(nbs, 1, D)

    ck_p, cv_p, sk_p, sv_p, wk_p, wv_p = new_p
    ck_s, cv_s, sk_s, sv_s, wk_s, wv_s = new_s
    return (y_p, y_s, sre_p, sim_p, ck_p, cv_p, sk_p, sv_p, wk_p, wv_p,
            sre_s, sim_s, ck_s, cv_s, sk_s, sv_s, wk_s, wv_s)
```

```python
import functools

import numpy as np
import jax
import jax.numpy as jnp
from jax import lax
from jax.experimental import pallas as pl
from jax.experimental.pallas import tpu as pltpu

F32 = jnp.float32
BF16 = jnp.bfloat16

N_HEADS = 16
N_KV_HEADS = 4
GQA_REP = N_HEADS // N_KV_HEADS
HEAD_DIM = 128
SSM_GROUP = 16
SSM_STATE = 64
CMP_BLOCK = 32
CMP_STRIDE = 16
CMP_PARTS = CMP_BLOCK // CMP_STRIDE
SLC_BLOCK = 64
N_SELECT = 16
WINDOW = 512
PAGE_SIZE = 128
ROPE_THETA = 10000.0
ATTN_SCALE = HEAD_DIM ** -0.5
EXP2_SCALE = ATTN_SCALE * float(np.log2(np.e))
NORM_EPS = 1e-6
NEG_INF = -1e30
FORCED_SCORE = 1e9

LANES = 128
KV_LANES = N_KV_HEADS * HEAD_DIM
PAIR_CH = 2 * LANES
PAIR_ST = (LANES // SSM_GROUP) * SSM_STATE
SLAB_ROWS = 2048
VMEM_LIMIT = 56 * 1024 * 1024


def _cparams(sem, vmem=VMEM_LIMIT):
    return pltpu.CompilerParams(dimension_semantics=sem, vmem_limit_bytes=vmem)


def _rms(x, g):
    ms = jnp.mean(x * x, axis=-1, keepdims=True)
    return x * lax.rsqrt(ms + NORM_EPS) * g


def _rope(x, cosf, sinf):
    return x * cosf + pltpu.roll(x, HEAD_DIM // 2, axis=1) * sinf


def _dot(a, b):
    return jnp.dot(a, b, preferred_element_type=F32)


def _dot_nt(a, b):
    return lax.dot_general(a, b, (((1,), (1,)), ((), ())), preferred_element_type=F32)


def _final_norm_kernel(x_ref, g_ref, o_ref):
    o_ref[...] = _rms(x_ref[...], g_ref[...])


def final_norm(x2, g, tm):
    m, d = x2.shape
    return pl.pallas_call(
        _final_norm_kernel, grid=(m // tm,),
        in_specs=[pl.BlockSpec((tm, d), lambda i: (i, 0)),
                  pl.BlockSpec((1, d), lambda i: (0, 0))],
        out_specs=pl.BlockSpec((tm, d), lambda i: (i, 0)),
        out_shape=jax.ShapeDtypeStruct((m, d), F32),
        compiler_params=_cparams(("parallel",)),
        name="final_norm")(x2, g)


def _s5_kernel(x_ref, gn_ref, wre_ref, wim_ref, cre_ref, cim_ref, ar_ref, ai_ref, d_ref, h0r_ref, h0i_ref,
               g_ref, hr_ref, hi_ref, hn_s, lhs_s, xr_s, xi_s, sr_s, si_s, y_s, *, tc, nb, npair):
    @pl.when(pl.program_id(0) == 0)
    def _():
        hr_ref[...] = h0r_ref[...]
        hi_ref[...] = h0i_ref[...]

    r2 = 2 * nb
    rows = tc * r2
    for b in range(nb):
        hn_s[b] = _rms(x_ref[b], gn_ref[...])
    keep = ((lax.broadcasted_iota(jnp.int32, (r2, PAIR_CH), 0) // nb == 1)
            == (lax.broadcasted_iota(jnp.int32, (r2, PAIR_CH), 1) >= LANES))
    lane_hi = lax.broadcasted_iota(jnp.int32, (tc, PAIR_CH), 1) >= LANES

    for k in range(npair):
        cols = slice(k * PAIR_CH, (k + 1) * PAIR_CH)

        def gather(t, c):
            v = hn_s[:, t, cols]
            lhs_s[t] = jnp.where(keep, jnp.concatenate([v, v], axis=0), 0.0)
            return c

        lax.fori_loop(0, tc, gather, 0, unroll=min(8, tc))
        lhs = lhs_s[...].reshape(rows, PAIR_CH).astype(BF16)
        xr_s[...] = _dot(lhs, wre_ref[k]).reshape(tc, r2, PAIR_ST)
        xi_s[...] = _dot(lhs, wim_ref[k]).reshape(tc, r2, PAIR_ST)
        ar = ar_ref[k]
        ai = ai_ref[k]

        def step(t, carry):
            hr, hi = carry
            nr = ar * hr - ai * hi + xr_s[t]
            ni = ar * hi + ai * hr + xi_s[t]
            sr_s[t] = nr
            si_s[t] = ni
            return nr, ni

        hr, hi = lax.fori_loop(0, tc, step, (hr_ref[k], hi_ref[k]), unroll=min(8, tc))
        hr_ref[k] = hr
        hi_ref[k] = hi
        h_re = sr_s[...].reshape(rows, PAIR_ST).astype(BF16)
        h_im = si_s[...].reshape(rows, PAIR_ST).astype(BF16)
        y_s[...] = (_dot(h_re, cre_ref[k]) - _dot(h_im, cim_ref[k])).reshape(tc, r2, PAIR_CH)
        for b in range(nb):
            y = jnp.where(lane_hi, y_s[:, nb + b, :], y_s[:, b, :])
            y = y + d_ref[:, cols] * hn_s[b, :, cols]
            g_ref[b, :, cols] = jax.nn.gelu(y).astype(BF16)


def s5_mixer(x3, gn, prm, h0r, h0i, tc):
    nb, T, d = x3.shape
    r2 = 2 * nb
    npair = d // PAIR_CH
    full3 = lambda s: (0, 0, 0)
    st_shape = (npair, r2, PAIR_ST)
    kern = functools.partial(_s5_kernel, tc=tc, nb=nb, npair=npair)
    return pl.pallas_call(
        kern, grid=(T // tc,),
        in_specs=[pl.BlockSpec((nb, tc, d), lambda s: (0, s, 0)),
                  pl.BlockSpec((1, d), lambda s: (0, 0)),
                  pl.BlockSpec((npair, PAIR_CH, PAIR_ST), full3),
                  pl.BlockSpec((npair, PAIR_CH, PAIR_ST), full3),
                  pl.BlockSpec((npair, PAIR_ST, PAIR_CH), full3),
                  pl.BlockSpec((npair, PAIR_ST, PAIR_CH), full3),
                  pl.BlockSpec(st_shape, full3),
                  pl.BlockSpec(st_shape, full3),
                  pl.BlockSpec((1, d), lambda s: (0, 0)),
                  pl.BlockSpec(st_shape, full3),
                  pl.BlockSpec(st_shape, full3)],
        out_specs=[pl.BlockSpec((nb, tc, d), lambda s: (0, s, 0)),
                   pl.BlockSpec(st_shape, full3),
                   pl.BlockSpec(st_shape, full3)],
        out_shape=[jax.ShapeDtypeStruct((nb, T, d), BF16),
                   jax.ShapeDtypeStruct(st_shape, F32),
                   jax.ShapeDtypeStruct(st_shape, F32)],
        scratch_shapes=[pltpu.VMEM((nb, tc, d), F32),
                        pltpu.VMEM((tc, r2, PAIR_CH), F32)]
                       + [pltpu.VMEM((tc, r2, PAIR_ST), F32)] * 4
                       + [pltpu.VMEM((tc, r2, PAIR_CH), F32)],
        compiler_params=_cparams(("arbitrary",)),
        name="s5_mixer")(x3, gn, prm["wre"], prm["wim"], prm["cre"], prm["cim"],
                         prm["ar"], prm["ai"], prm["d"], h0r, h0i)


def s5_params(a_re, a_im, log_dt, b_re, b_im, c_re, c_im, d, nb):
    G, P, H = b_re.shape
    gpc = LANES // H
    nchunk = G // gpc
    dt = jnp.exp(log_dt)[:, None]
    mag = jnp.exp(dt * a_re)
    ab_re, ab_im = mag * jnp.cos(dt * a_im), mag * jnp.sin(dt * a_im)
    den = a_re * a_re + a_im * a_im
    nr, ni = ab_re - 1.0, ab_im
    cf_re = (nr * a_re + ni * a_im) / den
    cf_im = (ni * a_re - nr * a_im) / den
    bb_re = cf_re[..., None] * b_re - cf_im[..., None] * b_im
    bb_im = cf_re[..., None] * b_im + cf_im[..., None] * b_re
    eye = jnp.eye(gpc, dtype=F32)

    def blk_in(bb):
        w = jnp.einsum("cgph,gk->cghkp", bb.reshape(nchunk, gpc, P, H), eye)
        return w.reshape(nchunk // 2, PAIR_CH, gpc * P).astype(BF16)

    def blk_out(cc):
        w = jnp.einsum("cghp,gk->ckpgh", cc.reshape(nchunk, gpc, H, P), eye)
        w = w.reshape(nchunk // 2, 2, gpc * P, LANES)
        return jnp.swapaxes(w, 1, 2).reshape(nchunk // 2, gpc * P, PAIR_CH).astype(BF16)

    def rows(a):
        a = a.reshape(nchunk // 2, 2, 1, gpc * P)
        return jnp.broadcast_to(a, (nchunk // 2, 2, nb, gpc * P)).reshape(nchunk // 2, 2 * nb, gpc * P)

    return dict(wre=blk_in(bb_re), wim=blk_in(bb_im), cre=blk_out(c_re), cim=blk_out(c_im),
                ar=rows(ab_re), ai=rows(ab_im), d=d[None])


def state_to_rows(h, nb):
    npair = h.shape[1] * h.shape[2] // PAIR_ST // 2
    return jnp.transpose(h.reshape(nb, npair, 2, PAIR_ST), (1, 2, 0, 3)).reshape(npair, 2 * nb, PAIR_ST)


def rows_to_state(h, nb):
    npair = h.shape[0]
    h = jnp.transpose(h.reshape(npair, 2, nb, PAIR_ST), (2, 0, 1, 3))
    return h.reshape(nb, npair * 2 * PAIR_ST // SSM_STATE, SSM_STATE)


COL_CHUNK = 512


def _resident(block_shape, index_map):
    return pl.BlockSpec(block_shape, index_map, pipeline_mode=pl.Buffered(1))


def _glu_kernel(a_ref, w_ref, b_ref, x_ref, o_ref):
    a = a_ref[...]
    d = x_ref.shape[1]
    for c in range(0, d, COL_CHUNK):
        z1 = _dot(a, w_ref[:, c:c + COL_CHUNK]) + b_ref[:, c:c + COL_CHUNK]
        z2 = _dot(a, w_ref[:, d + c:d + c + COL_CHUNK]) + b_ref[:, d + c:d + c + COL_CHUNK]
        o_ref[:, c:c + COL_CHUNK] = x_ref[:, c:c + COL_CHUNK] + z1 * jax.nn.sigmoid(z2)


def glu_call(g2, w_glu, b_glu, x2, li, tm):
    m, d = x2.shape
    return pl.pallas_call(
        _glu_kernel, grid=(m // tm,),
        in_specs=[pl.BlockSpec((tm, d), lambda i: (i, 0)),
                  _resident((None, d, 2 * d), lambda i: (li, 0, 0)),
                  _resident((None, 1, 2 * d), lambda i: (li, 0, 0)),
                  pl.BlockSpec((tm, d), lambda i: (i, 0))],
        out_specs=pl.BlockSpec((tm, d), lambda i: (i, 0)),
        out_shape=jax.ShapeDtypeStruct(x2.shape, F32),
        compiler_params=_cparams(("parallel",)),
        name="s5_glu")(g2, w_glu, b_glu, x2)


def _ffn_kernel(x_ref, g_ref, wg_ref, wu_ref, wd_ref, o_ref, hn_s, acc_s, *, nf):
    f = pl.program_id(1)

    @pl.when(f == 0)
    def _():
        hn_s[...] = _rms(x_ref[...], g_ref[...]).astype(BF16)
        acc_s[...] = jnp.zeros_like(acc_s)

    h = hn_s[...]
    a = _dot(h, wg_ref[...])
    u = _dot(h, wu_ref[...])
    act = (a * jax.nn.sigmoid(a) * u).astype(BF16)
    acc_s[...] += _dot(act, wd_ref[...])

    @pl.when(f == nf - 1)
    def _():
        o_ref[...] = x_ref[...] + acc_s[...]


def ffn_call(x2, g, wg, wu, wd, li, tm, tf):
    m, d = x2.shape
    ff = wg.shape[-1]
    nf = ff // tf
    return pl.pallas_call(
        functools.partial(_ffn_kernel, nf=nf), grid=(m // tm, nf),
        in_specs=[pl.BlockSpec((tm, d), lambda i, f: (i, 0)),
                  pl.BlockSpec((None, 1, d), lambda i, f: (li, 0, 0)),
                  pl.BlockSpec((None, d, tf), lambda i, f: (li, 0, f)),
                  pl.BlockSpec((None, d, tf), lambda i, f: (li, 0, f)),
                  pl.BlockSpec((None, tf, d), lambda i, f: (li, f, 0))],
        out_specs=pl.BlockSpec((tm, d), lambda i, f: (i, 0)),
        out_shape=jax.ShapeDtypeStruct(x2.shape, F32),
        scratch_shapes=[pltpu.VMEM((tm, d), BF16), pltpu.VMEM((tm, d), F32)],
        compiler_params=_cparams(("parallel", "arbitrary")),
        name="ffn")(x2, g, wg, wu, wd)


def _ple_kernel(x_ref, g_ref, p_ref, wg_ref, wp_ref, o_ref):
    hn = _rms(x_ref[...], g_ref[...]).astype(BF16)
    pe = p_ref[...].astype(BF16)
    for c in range(0, x_ref.shape[1], COL_CHUNK):
        cs = slice(c, c + COL_CHUNK)
        gate = jax.nn.sigmoid(_dot(hn, wg_ref[:, cs]))
        o_ref[:, cs] = x_ref[:, cs] + gate * _dot(pe, wp_ref[:, cs])


def ple_call(x2, g, p3, wg, wp, li, tm):
    m, d = x2.shape
    pd = wp.shape[1]
    return pl.pallas_call(
        _ple_kernel, grid=(m // tm,),
        in_specs=[pl.BlockSpec((tm, d), lambda i: (i, 0)),
                  _resident((None, 1, d), lambda i: (li, 0, 0)),
                  pl.BlockSpec((None, tm, pd), lambda i: (li, i, 0)),
                  _resident((None, d, d), lambda i: (li, 0, 0)),
                  _resident((None, pd, d), lambda i: (li, 0, 0))],
        out_specs=pl.BlockSpec((tm, d), lambda i: (i, 0)),
        out_shape=jax.ShapeDtypeStruct(x2.shape, F32),
        compiler_params=_cparams(("parallel",)),
        name="ple")(x2, g, p3, wg, wp)


def _kv_kernel(x_ref, g_ref, w_ref, cos_ref, sin_ref, *outs, heads_out):
    hn = _rms(x_ref[...], g_ref[...]).astype(BF16)
    tm = x_ref.shape[0]
    for k in range(6):
        z = _dot(hn, w_ref[:, k * KV_LANES:(k + 1) * KV_LANES])
        for h in range(N_KV_HEADS):
            zh = z[:, h * HEAD_DIM:(h + 1) * HEAD_DIM]
            if k in (2, 4):
                zh = _rope(zh, cos_ref[...], sin_ref[...])
            outs[k][pl.ds(h, tm, stride=N_KV_HEADS), :] = zh
            if heads_out and k in (2, 4):
                outs[6 + (k == 4)][0, h] = zh.astype(BF16)
            if heads_out and k in (3, 5):
                outs[8 + (k == 5)][0, h] = zh.T.astype(BF16)


def kv_call(x2, g, kv_w, cosf, sinf, C, R, tm, heads_out):
    m, d = x2.shape
    nt = R // tm
    out_shape = [jax.ShapeDtypeStruct((m * N_KV_HEADS, HEAD_DIM), F32)] * 6
    out_specs = [pl.BlockSpec((tm * N_KV_HEADS, HEAD_DIM), lambda i: (i, 0))] * 6
    if heads_out:
        out_shape += [jax.ShapeDtypeStruct((C, N_KV_HEADS, R, HEAD_DIM), BF16)] * 2
        out_specs += [pl.BlockSpec((1, N_KV_HEADS, tm, HEAD_DIM), lambda i: (i // nt, 0, i % nt, 0))] * 2
        out_shape += [jax.ShapeDtypeStruct((C, N_KV_HEADS, HEAD_DIM, R), BF16)] * 2
        out_specs += [pl.BlockSpec((1, N_KV_HEADS, HEAD_DIM, tm), lambda i: (i // nt, 0, 0, i % nt))] * 2
    return pl.pallas_call(
        functools.partial(_kv_kernel, heads_out=heads_out), grid=(m // tm,),
        in_specs=[pl.BlockSpec((tm, d), lambda i: (i, 0)),
                  _resident((1, d), lambda i: (0, 0)),
                  _resident((d, 6 * KV_LANES), lambda i: (0, 0)),
                  pl.BlockSpec((tm, HEAD_DIM), lambda i: (i % nt, 0)),
                  pl.BlockSpec((tm, HEAD_DIM), lambda i: (i % nt, 0))],
        out_specs=out_specs, out_shape=out_shape,
        compiler_params=_cparams(("parallel",)),
        name="kv_proj")(x2, g, kv_w, cosf, sinf)


def _q_kernel(x_ref, g_ref, w_ref, wgt_ref, bgt_ref, cos_ref, sin_ref, q_ref, gate_ref):
    hn = _rms(x_ref[...], g_ref[...]).astype(BF16)
    gate_ref[0] = jax.nn.sigmoid(_dot_nt(wgt_ref[...], hn) + bgt_ref[...])
    hpc = COL_CHUNK // HEAD_DIM
    for c in range(N_HEADS // hpc):
        z = _dot(hn, w_ref[:, c * COL_CHUNK:(c + 1) * COL_CHUNK])
        for h in range(hpc):
            zh = _rope(z[:, h * HEAD_DIM:(h + 1) * HEAD_DIM], cos_ref[...], sin_ref[...])
            q_ref[0, c * hpc + h] = zh.astype(BF16)


def q_call(x2, g, w_q, wgt_t, bgt, cosf, sinf, li, wi, C, R, tm):
    m, d = x2.shape
    nt = R // tm
    return pl.pallas_call(
        _q_kernel, grid=(m // tm,),
        in_specs=[pl.BlockSpec((tm, d), lambda i: (i, 0)),
                  _resident((None, 1, d), lambda i: (li, 0, 0)),
                  _resident((None, d, N_HEADS * HEAD_DIM), lambda i: (wi, 0, 0)),
                  _resident((LANES, d), lambda i: (0, 0)),
                  _resident((LANES, 1), lambda i: (0, 0)),
                  pl.BlockSpec((tm, HEAD_DIM), lambda i: (i % nt, 0)),
                  pl.BlockSpec((tm, HEAD_DIM), lambda i: (i % nt, 0))],
        out_specs=[pl.BlockSpec((1, N_HEADS, tm, HEAD_DIM), lambda i: (i // nt, 0, i % nt, 0)),
                   pl.BlockSpec((1, LANES, tm), lambda i: (i // nt, 0, i % nt))],
        out_shape=[jax.ShapeDtypeStruct((C, N_HEADS, R, HEAD_DIM), BF16),
                   jax.ShapeDtypeStruct((C, LANES, R), F32)],
        compiler_params=_cparams(("parallel",)),
        name="q_proj")(x2, g, w_q, wgt_t, bgt, cosf, sinf)


def _oproj_kernel(o_ref, w_ref, x_ref, out_ref):
    o = o_ref[...]
    for c in range(0, x_ref.shape[1], COL_CHUNK):
        out_ref[:, c:c + COL_CHUNK] = x_ref[:, c:c + COL_CHUNK] + _dot(o, w_ref[:, c:c + COL_CHUNK])


def oproj_call(o2, w_o, x2, li, tm):
    m, d = x2.shape
    return pl.pallas_call(
        _oproj_kernel, grid=(m // tm,),
        in_specs=[pl.BlockSpec((tm, d), lambda i: (i, 0)),
                  _resident((None, d, d), lambda i: (li, 0, 0)),
                  pl.BlockSpec((tm, d), lambda i: (i, 0))],
        out_specs=pl.BlockSpec((tm, d), lambda i: (i, 0)),
        out_shape=jax.ShapeDtypeStruct(x2.shape, F32),
        compiler_params=_cparams(("parallel",)),
        name="o_proj")(o2, w_o, x2)


def _compress_kernel(*refs, npage, rope, nslab, upb, slab, paged, transpose_out):
    if paged:
        refs = refs[1:]
    pages = refs[:npage]
    w1_ref, b1_ref, w2_ref, cos_ref, sin_ref, o_ref, p_s = refs[npage:]
    i = pl.program_id(1)
    upp = upb // npage
    unit_rows = CMP_STRIDE * N_KV_HEADS
    swapped = [jnp.swapaxes(pg[0].reshape(upp, unit_rows, HEAD_DIM), 0, 1) for pg in pages]
    for g in range(N_KV_HEADS):
        cols = []
        for s in range(CMP_STRIDE):
            parts = [sw[s * N_KV_HEADS + g] for sw in swapped]
            cols.append(parts[0] if npage == 1 else jnp.concatenate(parts, axis=0))
        lhs = jnp.concatenate(cols, axis=1).astype(BF16)
        p_s[g, pl.ds(pl.multiple_of(i * upb, upb), upb), :] = _dot(lhs, w1_ref[...])

    @pl.when(i == nslab - 1)
    def _():
        nu = nslab * upb
        for g in range(N_KV_HEADS):
            p_s[g, nu:nu + 8, :] = jnp.zeros((8, 2 * HEAD_DIM), F32)
            h = b1_ref[...] + p_s[g, 0:nu, 0:HEAD_DIM] + p_s[g, 1:nu + 1, HEAD_DIM:2 * HEAD_DIM]
            a = jax.nn.gelu(h).astype(BF16)
            if transpose_out:
                o_ref[0, g] = _dot_nt(w2_ref[...], a).astype(BF16)
            else:
                c = _dot(a, w2_ref[...])
                if rope:
                    c = _rope(c, cos_ref[...], sin_ref[...])
                o_ref[0, g] = c.astype(BF16)


def compress_call(src, page_table, w1r, b1, w2, cosf, sinf, rope, nbatch, nslab, slab, transpose_out=False):
    upb = slab // CMP_STRIDE
    nu = nslab * upb
    paged = page_table is not None
    npage = slab // PAGE_SIZE if paged else 1
    const = lambda b, i, *_: (0, 0)
    if paged:
        page_specs = [pl.BlockSpec((1, PAGE_SIZE * N_KV_HEADS, HEAD_DIM),
                                   functools.partial(lambda b, i, pt, r: (pt[b, i * npage + r], 0, 0), r=r))
                      for r in range(npage)]
    else:
        page_specs = [pl.BlockSpec((1, slab * N_KV_HEADS, HEAD_DIM), lambda b, i: (b, i, 0))]
    in_specs = page_specs + [pl.BlockSpec((CMP_STRIDE * HEAD_DIM, 2 * HEAD_DIM), const),
                             pl.BlockSpec((1, HEAD_DIM), const),
                             pl.BlockSpec((HEAD_DIM, HEAD_DIM), const),
                             pl.BlockSpec((nu, HEAD_DIM), const),
                             pl.BlockSpec((nu, HEAD_DIM), const)]
    o_dims = (HEAD_DIM, nu) if transpose_out else (nu, HEAD_DIM)
    out_specs = pl.BlockSpec((1, N_KV_HEADS) + o_dims, lambda b, i, *_: (b, 0, 0, 0))
    kern = functools.partial(_compress_kernel, npage=npage, rope=rope, nslab=nslab, upb=upb, slab=slab,
                             paged=paged, transpose_out=transpose_out)
    scratch = [pltpu.VMEM((N_KV_HEADS, nu + 8, 2 * HEAD_DIM), F32)]
    out_shape = jax.ShapeDtypeStruct((nbatch, N_KV_HEADS) + o_dims, BF16)
    args = [src] * npage + [w1r, b1, w2, cosf, sinf]
    if paged:
        gs = pltpu.PrefetchScalarGridSpec(num_scalar_prefetch=1, grid=(nbatch, nslab), in_specs=in_specs,
                                          out_specs=out_specs, scratch_shapes=scratch)
        return pl.pallas_call(kern, grid_spec=gs, out_shape=out_shape,
                              compiler_params=_cparams(("parallel", "arbitrary")),
                              name="compress_paged")(page_table, *args)
    return pl.pallas_call(kern, grid=(nbatch, nslab), in_specs=in_specs, out_specs=out_specs,
                          out_shape=out_shape, scratch_shapes=scratch,
                          compiler_params=_cparams(("parallel", "arbitrary")),
                          name="compress_rows")(*args)


def cmp_to_slc_matrix(n_cmp, n_slc):
    units = np.arange(n_cmp)[:, None] + np.arange(CMP_PARTS)[None, :]
    sblk = (units * CMP_STRIDE) // SLC_BLOCK
    m = np.zeros((n_cmp, n_slc), np.float32)
    np.add.at(m, (np.repeat(np.arange(n_cmp), CMP_PARTS), sblk.reshape(-1)), 1.0)
    return m


def _split_bf16(x):
    hi = x.astype(BF16)
    return hi, (x - hi.astype(F32)).astype(BF16)


def _softmax_rows(s, mask):
    s = jnp.where(mask, s, NEG_INF)
    m = jnp.max(s, axis=-1, keepdims=True)
    p = jnp.where(mask, jnp.exp(s - m), 0.0)
    return p / jnp.maximum(jnp.sum(p, axis=-1, keepdims=True), 1e-30)


def _nsa_prompt_kernel(q_ref, kc_ref, vct_ref, sk_ref, svt_ref, wk_ref, wvt_ref, gate_ref, mt_ref, o_ref, sel_s,
                       *, tq, n_cmp, n_slc):
    g = pl.program_id(1)
    qi = pl.program_id(2)
    t0 = qi * tq
    rows = GQA_REP * tq
    q = q_ref[0].reshape(rows, HEAD_DIM)

    def tile4(a):
        return jnp.concatenate([a] * GQA_REP, axis=1)

    ncp = kc_ref.shape[2]
    n_id = lax.broadcasted_iota(jnp.int32, (ncp, tq), 0)
    tq_pos = t0 + lax.broadcasted_iota(jnp.int32, (ncp, tq), 1)
    cmask = tile4((n_id * CMP_STRIDE + (CMP_BLOCK - 1) <= tq_pos) & (n_id < n_cmp))
    s = jnp.where(cmask, _dot_nt(kc_ref[0, 0], q) * ATTN_SCALE, NEG_INF)
    m = jnp.max(s, axis=0, keepdims=True)
    p = jnp.where(cmask, jnp.exp(s - m), 0.0)
    p = p / jnp.maximum(jnp.sum(p, axis=0, keepdims=True), 1e-30)
    o_cmp = _dot(vct_ref[0, 0], p.astype(BF16))
    p_grp = p[:, 0:tq] + p[:, tq:2 * tq] + p[:, 2 * tq:3 * tq] + p[:, 3 * tq:4 * tq]

    p_hi, p_lo = _split_bf16(p_grp)
    nsp = sel_s.shape[0]
    sc = (_dot(mt_ref[...], p_hi) + _dot(mt_ref[...], p_lo))[0:nsp]
    j_id = lax.broadcasted_iota(jnp.int32, (nsp, tq), 0)
    tpos = t0 + lax.broadcasted_iota(jnp.int32, (nsp, tq), 1)
    cur = tpos // SLC_BLOCK
    valid = (j_id * SLC_BLOCK <= tpos) & (j_id < n_slc)
    forced = (j_id == 0) | (j_id == cur) | (j_id == cur - 1)
    sc = jnp.where(valid, jnp.where(forced, FORCED_SCORE, sc), NEG_INF)
    cnt = jnp.zeros((nsp, tq), F32)
    for i in range(n_slc):
        ri = sc[i:i + 1, :]
        beats = (ri > sc) | ((ri == sc) & (j_id > i))
        cnt = cnt + jnp.where(beats, 1.0, 0.0)
    sel_s[...] = jnp.where((cnt < float(min(N_SELECT, n_slc))) & (sc > 0.5 * NEG_INF), 0.0, NEG_INF)

    kp_off = lax.broadcasted_iota(jnp.int32, (tq, tq), 0)
    tq_off = lax.broadcasted_iota(jnp.int32, (tq, tq), 1)

    def slc_bias(k0, diag):
        blk0 = k0 // SLC_BLOCK
        bias = jnp.concatenate([jnp.broadcast_to(sel_s[pl.ds(blk0 + jb, 1), :], (SLC_BLOCK, tq))
                                for jb in range(tq // SLC_BLOCK)], axis=0)
        if diag:
            bias = jnp.where(kp_off <= tq_off, bias, NEG_INF)
        return bias

    def win_bias(k0, diag):
        dist = (t0 + tq_off) - (k0 + kp_off)
        return jnp.where((dist >= 0) & (dist < WINDOW), 0.0, NEG_INF)

    def flash(k_ref, vt_ref, nblk, bias_fn):
        def block(it, carry, diag):
            m, l, acc = carry
            k0 = pl.multiple_of((qi - it) * tq, tq)
            sb = _dot_nt(k_ref[0, 0, pl.ds(k0, tq), :], q) + tile4(bias_fn(k0, diag))
            m_new = jnp.maximum(m, jnp.max(sb, axis=0, keepdims=True))
            pb = jnp.exp2((sb - m_new) * EXP2_SCALE)
            alpha = jnp.exp2((m - m_new) * EXP2_SCALE)
            l = alpha * l + jnp.sum(pb, axis=0, keepdims=True)
            acc = alpha * acc + _dot(vt_ref[0, 0, :, pl.ds(k0, tq)], pb.astype(BF16))
            return m_new, l, acc

        init = (jnp.full((1, rows), NEG_INF, F32), jnp.zeros((1, rows), F32), jnp.zeros((HEAD_DIM, rows), F32))
        carry = block(0, init, True)
        _, l, acc = lax.fori_loop(1, nblk, functools.partial(block, diag=False), carry)
        return acc / jnp.maximum(l, 1e-30)

    o_slc = flash(sk_ref, svt_ref, qi + 1, slc_bias)
    o_win = flash(wk_ref, wvt_ref, jnp.minimum(qi, WINDOW // tq) + 1, win_bias)

    for r in range(GQA_REP):
        cs = slice(r * tq, (r + 1) * tq)
        c0 = g * GQA_REP + r
        o = (gate_ref[0, pl.ds(c0, 1), :] * o_cmp[:, cs] + gate_ref[0, pl.ds(N_HEADS + c0, 1), :] * o_slc[:, cs]
             + gate_ref[0, pl.ds(2 * N_HEADS + c0, 1), :] * o_win[:, cs])
        o_ref[0, :, r * HEAD_DIM:(r + 1) * HEAD_DIM] = o.T.astype(BF16)


def nsa_prompt_call(q, k_cmp, v_cmp_t, sk, sv_t, wk, wv_t, gates_t, tq):
    B, _, T, _ = q.shape
    ncp = k_cmp.shape[2]
    n_cmp = T // CMP_STRIDE - CMP_PARTS + 1
    n_slc = -(-T // SLC_BLOCK)
    nsp = 8 * (-(-n_slc // 8))
    mt = np.zeros((LANES, ncp), np.float32)
    mt[:n_slc, :n_cmp] = cmp_to_slc_matrix(n_cmp, n_slc).T
    mt = jnp.asarray(mt, BF16)
    k_spec = pl.BlockSpec((1, 1, T, HEAD_DIM), lambda b, g, i: (b, g, 0, 0))
    vt_spec = pl.BlockSpec((1, 1, HEAD_DIM, T), lambda b, g, i: (b, g, 0, 0))
    kern = functools.partial(_nsa_prompt_kernel, tq=tq, n_cmp=n_cmp, n_slc=n_slc)
    return pl.pallas_call(
        kern, grid=(B, N_KV_HEADS, T // tq),
        in_specs=[pl.BlockSpec((1, GQA_REP, tq, HEAD_DIM), lambda b, g, i: (b, g, i, 0)),
                  pl.BlockSpec((1, 1, ncp, HEAD_DIM), lambda b, g, i: (b, g, 0, 0)),
                  pl.BlockSpec((1, 1, HEAD_DIM, ncp), lambda b, g, i: (b, g, 0, 0)),
                  k_spec, vt_spec, k_spec, vt_spec,
                  pl.BlockSpec((1, LANES, tq), lambda b, g, i: (b, 0, i)),
                  pl.BlockSpec((LANES, ncp), lambda b, g, i: (0, 0))],
        out_specs=pl.BlockSpec((1, tq, GQA_REP * HEAD_DIM), lambda b, g, i: (b, i, g)),
        out_shape=jax.ShapeDtypeStruct((B, T, N_HEADS * HEAD_DIM), BF16),
        scratch_shapes=[pltpu.VMEM((nsp, tq), F32)],
        compiler_params=_cparams(("parallel", "parallel", "arbitrary")),
        name="nsa_prompt")(q, k_cmp, v_cmp_t, sk, sv_t, wk, wv_t, gates_t, mt)


def _nsa_sample_select_kernel(q_ref, kc_ref, vc_ref, m_ref, ocmp_ref, idx_ref, ok_ref, *, nb, n_cmp, n_slc, qpos):
    rows = GQA_REP * nb
    q = q_ref[0].astype(F32).reshape(rows, HEAD_DIM).astype(BF16)
    ncp = kc_ref.shape[2]
    row_b = lax.broadcasted_iota(jnp.int32, (rows, 1), 0) % nb
    n_id = lax.broadcasted_iota(jnp.int32, (rows, ncp), 1)
    cmask = (n_id * CMP_STRIDE + (CMP_BLOCK - 1) <= qpos) & (n_id < n_cmp)
    o_acc = jnp.zeros((rows, HEAD_DIM), F32)
    pg = jnp.zeros((nb, ncp), F32)
    b_id = lax.broadcasted_iota(jnp.int32, (nb, 1), 0)
    for b in range(nb):
        s = _dot_nt(q, kc_ref[b, 0]) * ATTN_SCALE
        p = _softmax_rows(s, cmask)
        o_b = _dot(p.astype(BF16), vc_ref[b, 0])
        o_acc = o_acc + jnp.where(row_b == b, o_b, 0.0)
        p_sum = jnp.sum(p.reshape(GQA_REP, nb, ncp), axis=0)
        pg = pg + jnp.where(b_id == b, p_sum, 0.0)
    ocmp_ref[0] = o_acc.reshape(GQA_REP, nb, HEAD_DIM)

    p_hi, p_lo = _split_bf16(pg)
    sc = _dot(p_hi, m_ref[...]) + _dot(p_lo, m_ref[...])
    nsp = sc.shape[1]
    j_id = lax.broadcasted_iota(jnp.int32, (nb, nsp), 1)
    cur = qpos // SLC_BLOCK
    valid = (j_id * SLC_BLOCK <= qpos) & (j_id < n_slc)
    forced = (j_id == 0) | (j_id == cur) | (j_id == cur - 1)
    sc = jnp.where(valid, jnp.where(forced, FORCED_SCORE, sc), NEG_INF)
    lane = lax.broadcasted_iota(jnp.int32, (nb, LANES), 1)
    idx = jnp.zeros((nb, LANES), jnp.int32)
    okv = jnp.zeros((nb, LANES), jnp.int32)
    for i in range(min(N_SELECT, n_slc)):
        m = jnp.max(sc, axis=-1, keepdims=True)
        first = jnp.min(jnp.where(sc == m, j_id.astype(F32), float(nsp)), axis=-1, keepdims=True).astype(jnp.int32)
        idx = jnp.where(lane == i, first, idx)
        okv = jnp.where(lane == i, jnp.where(m > 0.5 * NEG_INF, 1, 0), okv)
        sc = jnp.where(j_id == first, -3.0e38, sc)
    idx_ref[0] = idx
    ok_ref[0] = okv


def nsa_sample_select_call(q, k_cmp, v_cmp, qpos):
    nb = q.shape[2]
    ncp = k_cmp.shape[2]
    n_cmp = (qpos + 1) // CMP_STRIDE - CMP_PARTS + 1
    n_slc = -(-(qpos + 1) // SLC_BLOCK)
    nsp = LANES * (-(-n_slc // LANES))
    m = np.zeros((ncp, nsp), np.float32)
    m[:n_cmp, :n_slc] = cmp_to_slc_matrix(n_cmp, n_slc)
    m = jnp.asarray(m, BF16)
    cmp_spec = pl.BlockSpec((nb, 1, ncp, HEAD_DIM), lambda g: (0, g, 0, 0))
    kern = functools.partial(_nsa_sample_select_kernel, nb=nb, n_cmp=n_cmp, n_slc=n_slc, qpos=qpos)
    return pl.pallas_call(
        kern, grid=(N_KV_HEADS,),
        in_specs=[pl.BlockSpec((1, GQA_REP, nb, HEAD_DIM), lambda g: (0, g, 0, 0)),
                  cmp_spec, cmp_spec,
                  pl.BlockSpec((ncp, nsp), lambda g: (0, 0))],
        out_specs=[pl.BlockSpec((1, GQA_REP, nb, HEAD_DIM), lambda g: (g, 0, 0, 0)),
                   pl.BlockSpec((1, nb, LANES), lambda g: (g, 0, 0)),
                   pl.BlockSpec((1, nb, LANES), lambda g: (g, 0, 0))],
        out_shape=[jax.ShapeDtypeStruct((N_KV_HEADS, GQA_REP, nb, HEAD_DIM), F32),
                   jax.ShapeDtypeStruct((N_KV_HEADS, nb, LANES), jnp.int32),
                   jax.ShapeDtypeStruct((N_KV_HEADS, nb, LANES), jnp.int32)],
        compiler_params=_cparams(("parallel",)),
        name="nsa_sample_select")(q, k_cmp, v_cmp, m)


def _nsa_sample_attend_kernel(idx_ref, ok_ref, pt_ref, q_ref, *rest, nb, nsel, n_past_blk, qpos):
    kblks, vblks = rest[:nsel], rest[nsel:2 * nsel]
    (wk_ref, wv_ref, skn_ref, svn_ref, wkn_ref, wvn_ref, ocmp_ref, gate_ref, o_ref) = rest[2 * nsel:]
    g = pl.program_id(0)
    b = pl.program_id(1)
    rows = GQA_REP * nb
    qf = q_ref[0].astype(F32).reshape(rows, HEAD_DIM)
    q = qf.astype(BF16)
    row_b = lax.broadcasted_iota(jnp.int32, (rows, 1), 0) % nb

    @pl.when(b == 0)
    def _():
        o_ref[...] = jnp.zeros_like(o_ref)

    head_rows = pl.ds(g, SLC_BLOCK, stride=N_KV_HEADS)
    kcat = jnp.concatenate([kb[0, head_rows, :] for kb in kblks], axis=0).astype(BF16)
    vcat = jnp.concatenate([vb[0, head_rows, :] for vb in vblks], axis=0).astype(BF16)
    nk = nsel * SLC_BLOCK
    slot = lax.broadcasted_iota(jnp.int32, (1, nk), 1) // SLC_BLOCK
    within = lax.broadcasted_iota(jnp.int32, (1, nk), 1) % SLC_BLOCK
    kpos = jnp.zeros((1, nk), jnp.int32)
    kok = jnp.zeros((1, nk), jnp.int32)
    new_ok = jnp.zeros((1, 1), jnp.int32)
    base = (g * nb + b) * LANES
    for s in range(nsel):
        bi = idx_ref[base + s]
        oks = ok_ref[base + s]
        kpos = jnp.where(slot == s, bi * SLC_BLOCK + within, kpos)
        kok = jnp.where(slot == s, jnp.where(bi < n_past_blk, oks, 0), kok)
        new_ok = jnp.maximum(new_ok, jnp.where(bi * SLC_BLOCK <= qpos, jnp.where(bi >= n_past_blk, oks, 0), 0))
    mask = (kok > 0) & (kpos <= qpos)
    s_past = jnp.where(mask, _dot_nt(q, kcat) * ATTN_SCALE, NEG_INF)
    has_new = new_ok > 0
    s_new = jnp.where(has_new, jnp.sum(qf * skn_ref[0], axis=-1, keepdims=True) * ATTN_SCALE,
                      NEG_INF)
    m = jnp.maximum(jnp.max(s_past, axis=-1, keepdims=True), s_new)
    p_past = jnp.where(mask, jnp.exp(s_past - m), 0.0)
    p_new = jnp.where(has_new, jnp.exp(s_new - m), 0.0)
    den = jnp.maximum(jnp.sum(p_past, axis=-1, keepdims=True) + p_new, 1e-30)
    o_slc = (_dot(p_past.astype(BF16), vcat)
             + p_new * svn_ref[0]) / den

    nw = wk_ref.shape[1] // N_KV_HEADS
    win_rows = pl.ds(g, nw, stride=N_KV_HEADS)
    wpos = qpos - nw + lax.broadcasted_iota(jnp.int32, (1, nw), 1)
    wdist = qpos - wpos
    wmask = (wpos >= 0) & (wdist >= 0) & (wdist < WINDOW)
    s_w = jnp.where(wmask, _dot_nt(q, wk_ref[0, win_rows, :].astype(BF16)) * ATTN_SCALE, NEG_INF)
    s_wn = jnp.sum(qf * wkn_ref[0], axis=-1, keepdims=True) * ATTN_SCALE
    mw = jnp.maximum(jnp.max(s_w, axis=-1, keepdims=True), s_wn)
    p_w = jnp.where(wmask, jnp.exp(s_w - mw), 0.0)
    p_wn = jnp.exp(s_wn - mw)
    den_w = jnp.maximum(jnp.sum(p_w, axis=-1, keepdims=True) + p_wn, 1e-30)
    o_win = (_dot(p_w.astype(BF16), wv_ref[0, win_rows, :].astype(BF16))
             + p_wn * wvn_ref[0]) / den_w

    gates = jnp.concatenate([gate_ref[...]] * GQA_REP, axis=0)
    lane = lax.broadcasted_iota(jnp.int32, (rows, LANES), 1)
    row_r = lax.broadcasted_iota(jnp.int32, (rows, 1), 0) // nb

    def gate_col(branch):
        c = branch * N_HEADS + g * GQA_REP + row_r
        return jnp.sum(jnp.where(lane == c, gates, 0.0), axis=-1, keepdims=True)

    o = (gate_col(0) * ocmp_ref[0].reshape(rows, HEAD_DIM) + gate_col(1) * o_slc + gate_col(2) * o_win)
    o_ref[0] += jnp.where(row_b == b, o, 0.0).reshape(GQA_REP, nb, HEAD_DIM)


def nsa_sample_attend_call(idx, ok, page_table, q, pool_k, pool_v, win_k, win_v, sk_new, sv_new, wk_new, wv_new,
                           o_cmp, gates, qpos):
    nb = q.shape[2]
    nsel = min(N_SELECT, -(-(qpos + 1) // SLC_BLOCK))
    n_past_blk = page_table.shape[1] * PAGE_SIZE // SLC_BLOCK
    half = PAGE_SIZE // SLC_BLOCK

    def blk_map(g, b, idx_r, ok_r, pt_r, s):
        bi = jnp.minimum(idx_r[(g * nb + b) * LANES + s], n_past_blk - 1)
        return (pt_r[b, bi // half], bi % half, 0)

    blk_specs = [pl.BlockSpec((1, SLC_BLOCK * N_KV_HEADS, HEAD_DIM), functools.partial(blk_map, s=s))
                 for s in range(nsel)]
    win_spec = pl.BlockSpec((1, win_k.shape[1], HEAD_DIM), lambda g, b, *_: (b, 0, 0))
    new_spec = pl.BlockSpec((1, 1, HEAD_DIM), lambda g, b, *_: (b * N_KV_HEADS + g, 0, 0))
    qo_spec = pl.BlockSpec((1, GQA_REP, nb, HEAD_DIM), lambda g, b, *_: (0, g, 0, 0))
    in_specs = ([qo_spec] + blk_specs + blk_specs + [win_spec, win_spec, new_spec, new_spec, new_spec, new_spec,
                pl.BlockSpec((1, GQA_REP, nb, HEAD_DIM), lambda g, b, *_: (g, 0, 0, 0)),
                pl.BlockSpec((nb, LANES), lambda g, b, *_: (0, 0))])
    gs = pltpu.PrefetchScalarGridSpec(
        num_scalar_prefetch=3, grid=(N_KV_HEADS, nb), in_specs=in_specs,
        out_specs=pl.BlockSpec((1, GQA_REP, nb, HEAD_DIM), lambda g, b, *_: (g, 0, 0, 0)))
    kern = functools.partial(_nsa_sample_attend_kernel, nb=nb, nsel=nsel, n_past_blk=n_past_blk, qpos=qpos)
    return pl.pallas_call(
        kern, grid_spec=gs,
        out_shape=jax.ShapeDtypeStruct((N_KV_HEADS, GQA_REP, nb, HEAD_DIM), F32),
        compiler_params=_cparams(("parallel", "arbitrary")),
        name="nsa_sample_attend")(idx.reshape(-1), ok.reshape(-1), page_table, q,
                                  *([pool_k] * nsel), *([pool_v] * nsel), win_k, win_v,
                                  sk_new, sv_new, wk_new, wv_new, o_cmp, gates)


def _rope_tables(pos):
    half = HEAD_DIM // 2
    inv_freq = ROPE_THETA ** (-jnp.arange(half, dtype=F32) / half)
    ang = pos.astype(F32)[:, None] * inv_freq[None, :]
    cos, sin = jnp.cos(ang), jnp.sin(ang)
    return jnp.concatenate([cos, cos], axis=-1), jnp.concatenate([-sin, sin], axis=-1)


def _trunk(x2, p3, pos, h0r, h0i, w, lay, kv_ctx):
    nb, T, tc, C, R, tm = lay
    depth = w["norm_mix_g"].shape[0]
    n_a = w["ssm_a_re"].shape[0]
    d = x2.shape[1]
    cosf, sinf = _rope_tables(pos)
    ssm_re, ssm_im = [], []
    kv = None
    for i in range(depth):
        if i < n_a:
            prm = s5_params(w["ssm_a_re"][i], w["ssm_a_im"][i], w["ssm_log_dt"][i], w["ssm_b_re"][i],
                            w["ssm_b_im"][i], w["ssm_c_re"][i], w["ssm_c_im"][i], w["ssm_d"][i], nb)
            g3, hr, hi = s5_mixer(x2.reshape(nb, T, d), w["norm_mix_g"][i][None], prm, h0r[i], h0i[i], tc)
            ssm_re.append(rows_to_state(hr, nb))
            ssm_im.append(rows_to_state(hi, nb))
            x2 = glu_call(g3.reshape(nb * T, d), w["ssm_w_glu"], w["ssm_b_glu"], x2, i, tm)
        else:
            j = i - n_a
            q, gates_t = q_call(x2, w["norm_mix_g3"], w["nsa_w_q"], w["nsa_wgt_t"][j], w["nsa_bgt"][j],
                                cosf, sinf, i, j, C, R, tm)
            o2 = kv_ctx["attend"](q, gates_t, kv)
            x2 = oproj_call(o2, w["nsa_w_o"], x2, j, tm)
        x2 = ffn_call(x2, w["norm_ffn_g3"], w["ffn_w_gate"], w["ffn_w_up"], w["ffn_w_down"], i, tm, 512)
        x2 = ple_call(x2, w["norm_ple_g3"], p3, w["ple_w_gate"], w["ple_w_proj"], i, tm)
        if i == n_a - 1:
            kv = kv_ctx["build"](x2, cosf, sinf)
    y = final_norm(x2, w["norm_final_g"][None], tm)
    return y, jnp.stack(ssm_re), jnp.stack(ssm_im), kv["new"]


def _cmp_weights(w1, b1, w2):
    w1p = w1.reshape(CMP_PARTS, CMP_STRIDE * HEAD_DIM, HEAD_DIM)
    return jnp.concatenate([w1p[0], w1p[1]], axis=1).astype(BF16), b1[None], w2.astype(BF16)


def kernel(x_prompt, x_sample, state_ssm_re, state_ssm_im, cache_cmp_k, cache_cmp_v, cache_slc_k, cache_slc_v, state_win_k, state_win_v, page_table, p_prompt, p_sample, norm_mix_g, norm_ffn_g, norm_ple_g, norm_kv_g, norm_final_g, ssm_a_re, ssm_a_im, ssm_log_dt, ssm_b_re, ssm_b_im, ssm_c_re, ssm_c_im, ssm_d, ssm_w_glu, ssm_b_glu, kv_w, cmp_k_w1, cmp_k_b1, cmp_k_w2, cmp_v_w1, cmp_v_b1, cmp_v_w2, nsa_w_q, nsa_w_gate, nsa_b_gate, nsa_w_o, ffn_w_gate, ffn_w_up, ffn_w_down, ple_w_gate, ple_w_proj):
    B, T, D = x_prompt.shape
    nbs, ts, _ = x_sample.shape
    assert ts == 1
    n_a = ssm_a_re.shape[0]
    past = page_table.shape[1] * PAGE_SIZE
    gate_pad = LANES - nsa_w_gate.shape[-1]
    w = dict(norm_mix_g=norm_mix_g, norm_mix_g3=norm_mix_g[:, None], norm_ffn_g3=norm_ffn_g[:, None],
             norm_ple_g3=norm_ple_g[:, None], norm_final_g=norm_final_g,
             ssm_a_re=ssm_a_re, ssm_a_im=ssm_a_im, ssm_log_dt=ssm_log_dt, ssm_b_re=ssm_b_re, ssm_b_im=ssm_b_im,
             ssm_c_re=ssm_c_re, ssm_c_im=ssm_c_im, ssm_d=ssm_d,
             ssm_w_glu=ssm_w_glu.astype(BF16), ssm_b_glu=ssm_b_glu[:, None],
             nsa_w_q=nsa_w_q.astype(BF16),
             nsa_wgt_t=jnp.pad(jnp.swapaxes(nsa_w_gate, 1, 2), ((0, 0), (0, gate_pad), (0, 0))).astype(BF16),
             nsa_bgt=jnp.pad(nsa_b_gate, ((0, 0), (0, gate_pad)))[..., None],
             nsa_w_o=nsa_w_o.astype(BF16),
             ffn_w_gate=ffn_w_gate.astype(BF16), ffn_w_up=ffn_w_up.astype(BF16), ffn_w_down=ffn_w_down.astype(BF16),
             ple_w_gate=ple_w_gate.astype(BF16), ple_w_proj=ple_w_proj.astype(BF16))
    kvw = kv_w.astype(BF16)
    kvg = norm_kv_g[None]
    ck_w = _cmp_weights(cmp_k_w1, cmp_k_b1, cmp_k_w2)
    cv_w = _cmp_weights(cmp_v_w1, cmp_v_b1, cmp_v_w2)

    tm_p = min(512, T)
    tq = min(256, T)
    pos_p = jnp.arange(T, dtype=jnp.int32)

    def build_prompt(x2, cosf, sinf):
        ck, cv, sk, sv, wk, wv, skh, wkh, svt, wvt = kv_call(x2, kvg, kvw, cosf, sinf, B, T, tm_p, True)
        slab = min(SLAB_ROWS, T)
        nslab = T // slab
        nu = T // CMP_STRIDE
        ccos, csin = _rope_tables(jnp.arange(nu, dtype=jnp.int32) * CMP_STRIDE + (CMP_BLOCK - 1))
        rows3 = lambda a: a.reshape(B, T * N_KV_HEADS, HEAD_DIM)
        k_cmp = compress_call(rows3(ck), None, *ck_w, ccos, csin, True, B, nslab, slab)
        v_cmp_t = compress_call(rows3(cv), None, cv_w[0], cv_w[1], cv_w[2].T, ccos, csin, False, B, nslab, slab,
                                transpose_out=True)
        r4 = lambda a: a.reshape(B, T, N_KV_HEADS, HEAD_DIM)
        ck, cv, sk, sv, wk, wv = [r4(a) for a in (ck, cv, sk, sv, wk, wv)]
        wb = min(WINDOW, T)
        new = (ck, cv, sk, sv, wk[:, T - wb:], wv[:, T - wb:])
        return dict(k_cmp=k_cmp, v_cmp_t=v_cmp_t, sk=skh, sv_t=svt, wk=wkh, wv_t=wvt, new=new)

    def attend_prompt(q, gates_t, kv):
        o = nsa_prompt_call(q, kv["k_cmp"], kv["v_cmp_t"], kv["sk"], kv["sv_t"], kv["wk"], kv["wv_t"], gates_t, tq)
        return o.reshape(B * T, N_HEADS * HEAD_DIM)

    h0 = jnp.zeros((n_a, D // PAIR_CH, 2 * B, PAIR_ST), F32)
    y_p, sre_p, sim_p, new_p = _trunk(x_prompt.reshape(B * T, D), p_prompt.reshape(-1, B * T, p_prompt.shape[-1]),
                                      pos_p, h0, h0, w, (B, T, min(128, T), B, T, tm_p),
                                      dict(build=build_prompt, attend=attend_prompt))
    y_p = y_p.reshape(B, T, D)

    pool3 = lambda a: a.reshape(a.shape[0], -1, HEAD_DIM)
    qpos = past
    pos_s = jnp.full((nbs,), past, jnp.int32)

    def build_sample(x2, cosf, sinf):
        ck, cv, sk, sv, wk, wv = kv_call(x2, kvg, kvw, cosf, sinf, 1, nbs, nbs, False)
        slab = min(SLAB_ROWS, past)
        nslab = past // slab
        nu = past // CMP_STRIDE
        ccos, csin = _rope_tables(jnp.arange(nu, dtype=jnp.int32) * CMP_STRIDE + (CMP_BLOCK - 1))
        k_cmp = compress_call(pool3(cache_cmp_k), page_table, *ck_w, ccos, csin, True, nbs, nslab, slab)
        v_cmp = compress_call(pool3(cache_cmp_v), page_table, *cv_w, ccos, csin, False, nbs, nslab, slab)
        r4 = lambda a: a.reshape(nbs, 1, N_KV_HEADS, HEAD_DIM)
        win_k = jnp.concatenate([state_win_k[:, 1:], r4(wk)], axis=1)
        win_v = jnp.concatenate([state_win_v[:, 1:], r4(wv)], axis=1)
        new = (r4(ck), r4(cv), r4(sk), r4(sv), win_k, win_v)
        return dict(k_cmp=k_cmp, v_cmp=v_cmp, sk=sk, sv=sv, wk=wk, wv=wv, new=new)

    def attend_sample(q, gates_t, kv):
        o_cmp, idx, ok = nsa_sample_select_call(q, kv["k_cmp"], kv["v_cmp"], qpos)
        rows1 = lambda a: a.reshape(nbs * N_KV_HEADS, 1, HEAD_DIM)
        o = nsa_sample_attend_call(idx, ok, page_table, q, pool3(cache_slc_k), pool3(cache_slc_v),
                                   pool3(state_win_k), pool3(state_win_v),
                                   rows1(kv["sk"]), rows1(kv["sv"]), rows1(kv["wk"]), rows1(kv["wv"]),
                                   o_cmp, gates_t[0].T, qpos)
        return jnp.transpose(o, (2, 0, 1, 3)).reshape(nbs, N_HEADS * HEAD_DIM).astype(BF16)

    h0r = jnp.stack([state_to_rows(state_ssm_re[i], nbs) for i in range(n_a)])
    h0i = jnp.stack([state_to_rows(state_ssm_im[i], nbs) for i in range(n_a)])
    y_s, sre_s, sim_s, new_s = _trunk(x_sample.reshape(nbs, D), p_sample.reshape(-1, nbs, p_sample.shape[-1]),
                                      pos_s, h0r, h0i, w, (nbs, 1, 1, 1, nbs, nbs),
                                      dict(build=build_sample, attend=attend_sample))
    y_s = y_s.reshape(nbs, 1, D)

    ck_p, cv_p, sk_p, sv_p, wk_p, wv_p = new_p
    ck_s, cv_s, sk_s, sv_s, wk_s, wv_s = new_s
    return (y_p, y_s, sre_p, sim_p, ck_p, cv_p, sk_p, sv_p, wk_p, wv_p,
            sre_s, sim_s, ck_s, cv_s, sk_s, sv_s, wk_s, wv_s)
```

```python
import functools

import numpy as np
import jax
import jax.numpy as jnp
from jax import lax
from jax.experimental import pallas as pl
from jax.experimental.pallas import tpu as pltpu

F32 = jnp.float32
BF16 = jnp.bfloat16

N_HEADS = 16
N_KV_HEADS = 4
GQA_REP = N_HEADS // N_KV_HEADS
HEAD_DIM = 128
SSM_GROUP = 16
SSM_STATE = 64
CMP_BLOCK = 32
CMP_STRIDE = 16
CMP_PARTS = CMP_BLOCK // CMP_STRIDE
SLC_BLOCK = 64
N_SELECT = 16
WINDOW = 512
PAGE_SIZE = 128
ROPE_THETA = 10000.0
ATTN_SCALE = HEAD_DIM ** -0.5
EXP2_SCALE = ATTN_SCALE * float(np.log2(np.e))
NORM_EPS = 1e-6
NEG_INF = -1e30
FORCED_SCORE = 1e9

LANES = 128
KV_LANES = N_KV_HEADS * HEAD_DIM
PAIR_CH = 2 * LANES
PAIR_ST = (LANES // SSM_GROUP) * SSM_STATE
SLAB_ROWS = 2048
VMEM_LIMIT = 56 * 1024 * 1024


def _cparams(sem, vmem=VMEM_LIMIT):
    return pltpu.CompilerParams(dimension_semantics=sem, vmem_limit_bytes=vmem)


def _rms(x, g):
    ms = jnp.mean(x * x, axis=-1, keepdims=True)
    return x * lax.rsqrt(ms + NORM_EPS) * g


def _rope(x, cosf, sinf):
    return x * cosf + pltpu.roll(x, HEAD_DIM // 2, axis=1) * sinf


def _dot(a, b):
    return jnp.dot(a, b, preferred_element_type=F32)


def _dot_nt(a, b):
    return lax.dot_general(a, b, (((1,), (1,)), ((), ())), preferred_element_type=F32)


def _s5_kernel(x_ref, gn_ref, wre_ref, wim_ref, cre_ref, cim_ref, ar_ref, ai_ref, d_ref, h0r_ref, h0i_ref,
               g_ref, hr_ref, hi_ref, hn_s, lhs_s, xr_s, xi_s, sr_s, si_s, y_s, *, tc, nb, npair):
    @pl.when(pl.program_id(0) == 0)
    def _():
        hr_ref[...] = h0r_ref[...]
        hi_ref[...] = h0i_ref[...]

    r2 = 2 * nb
    rows = tc * r2
    for b in range(nb):
        hn_s[b] = _rms(x_ref[b], gn_ref[...])
    keep = ((lax.broadcasted_iota(jnp.int32, (r2, PAIR_CH), 0) // nb == 1)
            == (lax.broadcasted_iota(jnp.int32, (r2, PAIR_CH), 1) >= LANES))
    lane_hi = lax.broadcasted_iota(jnp.int32, (tc, PAIR_CH), 1) >= LANES

    for k in range(npair):
        cols = slice(k * PAIR_CH, (k + 1) * PAIR_CH)

        def gather(t, c):
            v = hn_s[:, t, cols]
            lhs_s[t] = jnp.where(keep, jnp.concatenate([v, v], axis=0), 0.0)
            return c

        lax.fori_loop(0, tc, gather, 0, unroll=min(8, tc))
        lhs = lhs_s[...].reshape(rows, PAIR_CH).astype(BF16)
        xr_s[...] = _dot(lhs, wre_ref[k]).reshape(tc, r2, PAIR_ST)
        xi_s[...] = _dot(lhs, wim_ref[k]).reshape(tc, r2, PAIR_ST)
        ar = ar_ref[k]
        ai = ai_ref[k]

        def step(t, carry):
            hr, hi = carry
            nr = ar * hr - ai * hi + xr_s[t]
            ni = ar * hi + ai * hr + xi_s[t]
            sr_s[t] = nr
            si_s[t] = ni
            return nr, ni

        hr, hi = lax.fori_loop(0, tc, step, (hr_ref[k], hi_ref[k]), unroll=min(8, tc))
        hr_ref[k] = hr
        hi_ref[k] = hi
        h_re = sr_s[...].reshape(rows, PAIR_ST).astype(BF16)
        h_im = si_s[...].reshape(rows, PAIR_ST).astype(BF16)
        y_s[...] = (_dot(h_re, cre_ref[k]) - _dot(h_im, cim_ref[k])).reshape(tc, r2, PAIR_CH)
        for b in range(nb):
            y = jnp.where(lane_hi, y_s[:, nb + b, :], y_s[:, b, :])
            y = y + d_ref[:, cols] * hn_s[b, :, cols]
            g_ref[b, :, cols] = jax.nn.gelu(y).astype(BF16)


def s5_mixer(x3, gn, prm, h0r, h0i, tc):
    nb, T, d = x3.shape
    r2 = 2 * nb
    npair = d // PAIR_CH
    full3 = lambda s: (0, 0, 0)
    st_shape = (npair, r2, PAIR_ST)
    kern = functools.partial(_s5_kernel, tc=tc, nb=nb, npair=npair)
    return pl.pallas_call(
        kern, grid=(T // tc,),
        in_specs=[pl.BlockSpec((nb, tc, d), lambda s: (0, s, 0)),
                  pl.BlockSpec((1, d), lambda s: (0, 0)),
                  pl.BlockSpec((npair, PAIR_CH, PAIR_ST), full3),
                  pl.BlockSpec((npair, PAIR_CH, PAIR_ST), full3),
                  pl.BlockSpec((npair, PAIR_ST, PAIR_CH), full3),
                  pl.BlockSpec((npair, PAIR_ST, PAIR_CH), full3),
                  pl.BlockSpec(st_shape, full3),
                  pl.BlockSpec(st_shape, full3),
                  pl.BlockSpec((1, d), lambda s: (0, 0)),
                  pl.BlockSpec(st_shape, full3),
                  pl.BlockSpec(st_shape, full3)],
        out_specs=[pl.BlockSpec((nb, tc, d), lambda s: (0, s, 0)),
                   pl.BlockSpec(st_shape, full3),
                   pl.BlockSpec(st_shape, full3)],
        out_shape=[jax.ShapeDtypeStruct((nb, T, d), BF16),
                   jax.ShapeDtypeStruct(st_shape, F32),
                   jax.ShapeDtypeStruct(st_shape, F32)],
        scratch_shapes=[pltpu.VMEM((nb, tc, d), F32),
                        pltpu.VMEM((tc, r2, PAIR_CH), F32)]
                       + [pltpu.VMEM((tc, r2, PAIR_ST), F32)] * 4
                       + [pltpu.VMEM((tc, r2, PAIR_CH), F32)],
        compiler_params=_cparams(("arbitrary",)),
        name="s5_mixer")(x3, gn, prm["wre"], prm["wim"], prm["cre"], prm["cim"],
                         prm["ar"], prm["ai"], prm["d"], h0r, h0i)


def s5_params(a_re, a_im, log_dt, b_re, b_im, c_re, c_im, d, nb):
    G, P, H = b_re.shape
    gpc = LANES // H
    nchunk = G // gpc
    dt = jnp.exp(log_dt)[:, None]
    mag = jnp.exp(dt * a_re)
    ab_re, ab_im = mag * jnp.cos(dt * a_im), mag * jnp.sin(dt * a_im)
    den = a_re * a_re + a_im * a_im
    nr, ni = ab_re - 1.0, ab_im
    cf_re = (nr * a_re + ni * a_im) / den
    cf_im = (ni * a_re - nr * a_im) / den
    bb_re = cf_re[..., None] * b_re - cf_im[..., None] * b_im
    bb_im = cf_re[..., None] * b_im + cf_im[..., None] * b_re
    eye = jnp.eye(gpc, dtype=F32)

    def blk_in(bb):
        w = jnp.einsum("cgph,gk->cghkp", bb.reshape(nchunk, gpc, P, H), eye)
        return w.reshape(nchunk // 2, PAIR_CH, gpc * P).astype(BF16)

    def blk_out(cc):
        w = jnp.einsum("cghp,gk->ckpgh", cc.reshape(nchunk, gpc, H, P), eye)
        w = w.reshape(nchunk // 2, 2, gpc * P, LANES)
        return jnp.swapaxes(w, 1, 2).reshape(nchunk // 2, gpc * P, PAIR_CH).astype(BF16)

    def rows(a):
        a = a.reshape(nchunk // 2, 2, 1, gpc * P)
        return jnp.broadcast_to(a, (nchunk // 2, 2, nb, gpc * P)).reshape(nchunk // 2, 2 * nb, gpc * P)

    return dict(wre=blk_in(bb_re), wim=blk_in(bb_im), cre=blk_out(c_re), cim=blk_out(c_im),
                ar=rows(ab_re), ai=rows(ab_im), d=d[None])


def state_to_rows(h, nb):
    npair = h.shape[1] * h.shape[2] // PAIR_ST // 2
    return jnp.transpose(h.reshape(nb, npair, 2, PAIR_ST), (1, 2, 0, 3)).reshape(npair, 2 * nb, PAIR_ST)


def rows_to_state(h, nb):
    npair = h.shape[0]
    h = jnp.transpose(h.reshape(npair, 2, nb, PAIR_ST), (2, 0, 1, 3))
    return h.reshape(nb, npair * 2 * PAIR_ST // SSM_STATE, SSM_STATE)


COL_CHUNK = 512


def _resident(block_shape, index_map):
    return pl.BlockSpec(block_shape, index_map, pipeline_mode=pl.Buffered(1))


def _glu_kernel(a_ref, w_ref, b_ref, x_ref, o_ref):
    a = a_ref[...]
    d = x_ref.shape[1]
    for c in range(0, d, COL_CHUNK):
        z1 = _dot(a, w_ref[:, c:c + COL_CHUNK]) + b_ref[:, c:c + COL_CHUNK]
        z2 = _dot(a, w_ref[:, d + c:d + c + COL_CHUNK]) + b_ref[:, d + c:d + c + COL_CHUNK]
        o_ref[:, c:c + COL_CHUNK] = x_ref[:, c:c + COL_CHUNK] + z1 * jax.nn.sigmoid(z2)


def glu_call(g2, w_glu, b_glu, x2, li, tm):
    m, d = x2.shape
    return pl.pallas_call(
        _glu_kernel, grid=(m // tm,),
        in_specs=[pl.BlockSpec((tm, d), lambda i: (i, 0)),
                  _resident((None, d, 2 * d), lambda i: (li, 0, 0)),
                  _resident((None, 1, 2 * d), lambda i: (li, 0, 0)),
                  pl.BlockSpec((tm, d), lambda i: (i, 0))],
        out_specs=pl.BlockSpec((tm, d), lambda i: (i, 0)),
        out_shape=jax.ShapeDtypeStruct(x2.shape, F32),
        compiler_params=_cparams(("parallel",)),
        name="s5_glu")(g2, w_glu, b_glu, x2)


def _ffn_kernel(x_ref, g_ref, wg_ref, wu_ref, wd_ref, o_ref, hn_s, acc_s, *, nf):
    f = pl.program_id(1)

    @pl.when(f == 0)
    def _():
        hn_s[...] = _rms(x_ref[...], g_ref[...]).astype(BF16)
        acc_s[...] = jnp.zeros_like(acc_s)

    h = hn_s[...]
    a = _dot(h, wg_ref[...])
    u = _dot(h, wu_ref[...])
    act = (a * jax.nn.sigmoid(a) * u).astype(BF16)
    acc_s[...] += _dot(act, wd_ref[...])

    @pl.when(f == nf - 1)
    def _():
        o_ref[...] = x_ref[...] + acc_s[...]


def ffn_call(x2, g, wg, wu, wd, li, tm, tf):
    m, d = x2.shape
    ff = wg.shape[-1]
    nf = ff // tf
    return pl.pallas_call(
        functools.partial(_ffn_kernel, nf=nf), grid=(m // tm, nf),
        in_specs=[pl.BlockSpec((tm, d), lambda i, f: (i, 0)),
                  pl.BlockSpec((None, 1, d), lambda i, f: (li, 0, 0)),
                  pl.BlockSpec((None, d, tf), lambda i, f: (li, 0, f)),
                  pl.BlockSpec((None, d, tf), lambda i, f: (li, 0, f)),
                  pl.BlockSpec((None, tf, d), lambda i, f: (li, f, 0))],
        out_specs=pl.BlockSpec((tm, d), lambda i, f: (i, 0)),
        out_shape=jax.ShapeDtypeStruct(x2.shape, F32),
        scratch_shapes=[pltpu.VMEM((tm, d), BF16), pltpu.VMEM((tm, d), F32)],
        compiler_params=_cparams(("parallel", "arbitrary")),
        name="ffn")(x2, g, wg, wu, wd)


def _ple_kernel(x_ref, g_ref, p_ref, wg_ref, wp_ref, gf_ref, o_ref, *, final_norm):
    hn = _rms(x_ref[...], g_ref[...]).astype(BF16)
    pe = p_ref[...].astype(BF16)
    for c in range(0, x_ref.shape[1], COL_CHUNK):
        cs = slice(c, c + COL_CHUNK)
        gate = jax.nn.sigmoid(_dot(hn, wg_ref[:, cs]))
        o_ref[:, cs] = x_ref[:, cs] + gate * _dot(pe, wp_ref[:, cs])
    if final_norm:
        o_ref[...] = _rms(o_ref[...], gf_ref[...])


def ple_call(x2, g, p3, wg, wp, gf, li, tm, final_norm):
    m, d = x2.shape
    pd = wp.shape[1]
    return pl.pallas_call(
        functools.partial(_ple_kernel, final_norm=final_norm), grid=(m // tm,),
        in_specs=[pl.BlockSpec((tm, d), lambda i: (i, 0)),
                  _resident((None, 1, d), lambda i: (li, 0, 0)),
                  pl.BlockSpec((None, tm, pd), lambda i: (li, i, 0)),
                  _resident((None, d, d), lambda i: (li, 0, 0)),
                  _resident((None, pd, d), lambda i: (li, 0, 0)),
                  _resident((1, d), lambda i: (0, 0))],
        out_specs=pl.BlockSpec((tm, d), lambda i: (i, 0)),
        out_shape=jax.ShapeDtypeStruct(x2.shape, F32),
        compiler_params=_cparams(("parallel",)),
        name="ple")(x2, g, p3, wg, wp, gf)


def _kv_kernel(x_ref, g_ref, w_ref, cos_ref, sin_ref, *outs, heads_out, nt):
    hn = _rms(x_ref[...], g_ref[...]).astype(BF16)
    tm = x_ref.shape[0]
    if heads_out:
        t_row = (pl.program_id(0) % nt) * tm + lax.broadcasted_iota(jnp.int32, (tm, LANES), 0)
        blk_onehot = jnp.where(lax.broadcasted_iota(jnp.int32, (tm, LANES), 1) == t_row // SLC_BLOCK, 1.0, 0.0)
        for h in range(N_KV_HEADS):
            outs[6][0, h, :, HEAD_DIM:] = blk_onehot.astype(BF16)
    for k in range(6):
        z = _dot(hn, w_ref[:, k * KV_LANES:(k + 1) * KV_LANES])
        for h in range(N_KV_HEADS):
            zh = z[:, h * HEAD_DIM:(h + 1) * HEAD_DIM]
            if k in (2, 4):
                zh = _rope(zh, cos_ref[...], sin_ref[...])
            outs[k][pl.ds(h, tm, stride=N_KV_HEADS), :] = zh
            if heads_out and k == 2:
                outs[6][0, h, :, :HEAD_DIM] = zh.astype(BF16)
            if heads_out and k == 4:
                outs[7][0, h] = zh.astype(BF16)
            if heads_out and k in (3, 5):
                outs[8 + (k == 5)][0, h] = zh.T.astype(BF16)


def kv_call(x2, g, kv_w, cosf, sinf, C, R, tm, heads_out):
    m, d = x2.shape
    nt = R // tm
    out_shape = [jax.ShapeDtypeStruct((m * N_KV_HEADS, HEAD_DIM), F32)] * 6
    out_specs = [pl.BlockSpec((tm * N_KV_HEADS, HEAD_DIM), lambda i: (i, 0))] * 6
    if heads_out:
        assert -(-R // SLC_BLOCK) <= LANES
        for width in (2 * HEAD_DIM, HEAD_DIM):
            out_shape += [jax.ShapeDtypeStruct((C, N_KV_HEADS, R, width), BF16)]
            out_specs += [pl.BlockSpec((1, N_KV_HEADS, tm, width), lambda i: (i // nt, 0, i % nt, 0))]
        out_shape += [jax.ShapeDtypeStruct((C, N_KV_HEADS, HEAD_DIM, R), BF16)] * 2
        out_specs += [pl.BlockSpec((1, N_KV_HEADS, HEAD_DIM, tm), lambda i: (i // nt, 0, 0, i % nt))] * 2
    return pl.pallas_call(
        functools.partial(_kv_kernel, heads_out=heads_out, nt=nt), grid=(m // tm,),
        in_specs=[pl.BlockSpec((tm, d), lambda i: (i, 0)),
                  _resident((1, d), lambda i: (0, 0)),
                  _resident((d, 6 * KV_LANES), lambda i: (0, 0)),
                  pl.BlockSpec((tm, HEAD_DIM), lambda i: (i % nt, 0)),
                  pl.BlockSpec((tm, HEAD_DIM), lambda i: (i % nt, 0))],
        out_specs=out_specs, out_shape=out_shape,
        compiler_params=_cparams(("parallel",)),
        name="kv_proj")(x2, g, kv_w, cosf, sinf)


def _q_kernel(x_ref, g_ref, w_ref, wgt_ref, bgt_ref, cos_ref, sin_ref, q_ref, gate_ref):
    hn = _rms(x_ref[...], g_ref[...]).astype(BF16)
    gate_ref[0] = jax.nn.sigmoid(_dot_nt(wgt_ref[...], hn) + bgt_ref[...])
    hpc = COL_CHUNK // HEAD_DIM
    for c in range(N_HEADS // hpc):
        z = _dot(hn, w_ref[:, c * COL_CHUNK:(c + 1) * COL_CHUNK])
        for h in range(hpc):
            zh = _rope(z[:, h * HEAD_DIM:(h + 1) * HEAD_DIM], cos_ref[...], sin_ref[...])
            q_ref[0, c * hpc + h] = (zh * EXP2_SCALE).astype(BF16)


def q_call(x2, g, w_q, wgt_t, bgt, cosf, sinf, li, wi, C, R, tm):
    m, d = x2.shape
    nt = R // tm
    return pl.pallas_call(
        _q_kernel, grid=(m // tm,),
        in_specs=[pl.BlockSpec((tm, d), lambda i: (i, 0)),
                  _resident((None, 1, d), lambda i: (li, 0, 0)),
                  _resident((None, d, N_HEADS * HEAD_DIM), lambda i: (wi, 0, 0)),
                  _resident((LANES, d), lambda i: (0, 0)),
                  _resident((LANES, 1), lambda i: (0, 0)),
                  pl.BlockSpec((tm, HEAD_DIM), lambda i: (i % nt, 0)),
                  pl.BlockSpec((tm, HEAD_DIM), lambda i: (i % nt, 0))],
        out_specs=[pl.BlockSpec((1, N_HEADS, tm, HEAD_DIM), lambda i: (i // nt, 0, i % nt, 0)),
                   pl.BlockSpec((1, LANES, tm), lambda i: (i // nt, 0, i % nt))],
        out_shape=[jax.ShapeDtypeStruct((C, N_HEADS, R, HEAD_DIM), BF16),
                   jax.ShapeDtypeStruct((C, LANES, R), F32)],
        compiler_params=_cparams(("parallel",)),
        name="q_proj")(x2, g, w_q, wgt_t, bgt, cosf, sinf)


def _oproj_kernel(o_ref, w_ref, x_ref, out_ref):
    o = o_ref[...]
    for c in range(0, x_ref.shape[1], COL_CHUNK):
        out_ref[:, c:c + COL_CHUNK] = x_ref[:, c:c + COL_CHUNK] + _dot(o, w_ref[:, c:c + COL_CHUNK])


def oproj_call(o2, w_o, x2, li, tm):
    m, d = x2.shape
    return pl.pallas_call(
        _oproj_kernel, grid=(m // tm,),
        in_specs=[pl.BlockSpec((tm, d), lambda i: (i, 0)),
                  _resident((None, d, d), lambda i: (li, 0, 0)),
                  pl.BlockSpec((tm, d), lambda i: (i, 0))],
        out_specs=pl.BlockSpec((tm, d), lambda i: (i, 0)),
        out_shape=jax.ShapeDtypeStruct(x2.shape, F32),
        compiler_params=_cparams(("parallel",)),
        name="o_proj")(o2, w_o, x2)


def _compress_kernel(*refs, npage, rope, nslab, upb, slab, paged, transpose_out):
    if paged:
        refs = refs[1:]
    pages = refs[:npage]
    w1_ref, b1_ref, w2_ref, cos_ref, sin_ref, o_ref, p_s = refs[npage:]
    i = pl.program_id(1)
    upp = upb // npage
    unit_rows = CMP_STRIDE * N_KV_HEADS
    swapped = [jnp.swapaxes(pg[0].reshape(upp, unit_rows, HEAD_DIM), 0, 1) for pg in pages]
    for g in range(N_KV_HEADS):
        cols = []
        for s in range(CMP_STRIDE):
            parts = [sw[s * N_KV_HEADS + g] for sw in swapped]
            cols.append(parts[0] if npage == 1 else jnp.concatenate(parts, axis=0))
        lhs = jnp.concatenate(cols, axis=1).astype(BF16)
        p_s[g, pl.ds(pl.multiple_of(i * upb, upb), upb), :] = _dot(lhs, w1_ref[...])

    @pl.when(i == nslab - 1)
    def _():
        nu = nslab * upb
        for g in range(N_KV_HEADS):
            p_s[g, nu:nu + 8, :] = jnp.zeros((8, 2 * HEAD_DIM), F32)
            h = b1_ref[...] + p_s[g, 0:nu, 0:HEAD_DIM] + p_s[g, 1:nu + 1, HEAD_DIM:2 * HEAD_DIM]
            a = jax.nn.gelu(h).astype(BF16)
            if transpose_out:
                o_ref[0, g] = _dot_nt(w2_ref[...], a).astype(BF16)
            else:
                c = _dot(a, w2_ref[...])
                if rope:
                    c = _rope(c, cos_ref[...], sin_ref[...])
                o_ref[0, g] = c.astype(BF16)


def compress_call(src, page_table, w1r, b1, w2, cosf, sinf, rope, nbatch, nslab, slab, transpose_out=False):
    upb = slab // CMP_STRIDE
    nu = nslab * upb
    paged = page_table is not None
    npage = slab // PAGE_SIZE if paged else 1
    const = lambda b, i, *_: (0, 0)
    if paged:
        page_specs = [pl.BlockSpec((1, PAGE_SIZE * N_KV_HEADS, HEAD_DIM),
                                   functools.partial(lambda b, i, pt, r: (pt[b, i * npage + r], 0, 0), r=r))
                      for r in range(npage)]
    else:
        page_specs = [pl.BlockSpec((1, slab * N_KV_HEADS, HEAD_DIM), lambda b, i: (b, i, 0))]
    in_specs = page_specs + [pl.BlockSpec((CMP_STRIDE * HEAD_DIM, 2 * HEAD_DIM), const),
                             pl.BlockSpec((1, HEAD_DIM), const),
                             pl.BlockSpec((HEAD_DIM, HEAD_DIM), const),
                             pl.BlockSpec((nu, HEAD_DIM), const),
                             pl.BlockSpec((nu, HEAD_DIM), const)]
    o_dims = (HEAD_DIM, nu) if transpose_out else (nu, HEAD_DIM)
    out_specs = pl.BlockSpec((1, N_KV_HEADS) + o_dims, lambda b, i, *_: (b, 0, 0, 0))
    kern = functools.partial(_compress_kernel, npage=npage, rope=rope, nslab=nslab, upb=upb, slab=slab,
                             paged=paged, transpose_out=transpose_out)
    scratch = [pltpu.VMEM((N_KV_HEADS, nu + 8, 2 * HEAD_DIM), F32)]
    out_shape = jax.ShapeDtypeStruct((nbatch, N_KV_HEADS) + o_dims, BF16)
    args = [src] * npage + [w1r, b1, w2, cosf, sinf]
    if paged:
        gs = pltpu.PrefetchScalarGridSpec(num_scalar_prefetch=1, grid=(nbatch, nslab), in_specs=in_specs,
                                          out_specs=out_specs, scratch_shapes=scratch)
        return pl.pallas_call(kern, grid_spec=gs, out_shape=out_shape,
                              compiler_params=_cparams(("parallel", "arbitrary")),
                              name="compress_paged")(page_table, *args)
    return pl.pallas_call(kern, grid=(nbatch, nslab), in_specs=in_specs, out_specs=out_specs,
                          out_shape=out_shape, scratch_shapes=scratch,
                          compiler_params=_cparams(("parallel", "arbitrary")),
                          name="compress_rows")(*args)


def cmp_to_slc_matrix(n_cmp, n_slc):
    units = np.arange(n_cmp)[:, None] + np.arange(CMP_PARTS)[None, :]
    sblk = (units * CMP_STRIDE) // SLC_BLOCK
    m = np.zeros((n_cmp, n_slc), np.float32)
    np.add.at(m, (np.repeat(np.arange(n_cmp), CMP_PARTS), sblk.reshape(-1)), 1.0)
    return m


def _split_bf16(x):
    hi = x.astype(BF16)
    return hi, (x - hi.astype(F32)).astype(BF16)


def _softmax_rows(s, mask):
    s = jnp.where(mask, s, NEG_INF)
    m = jnp.max(s, axis=-1, keepdims=True)
    p = jnp.where(mask, jnp.exp2(s - m), 0.0)
    return p / jnp.maximum(jnp.sum(p, axis=-1, keepdims=True), 1e-30)


def _nsa_prompt_kernel(q_ref, kc_ref, vct_ref, sk_ref, svt_ref, wk_ref, wvt_ref, gate_ref, mt_ref, o_ref,
                       *, tq, n_cmp, n_slc):
    g = pl.program_id(1)
    qi = pl.program_id(2)
    t0 = qi * tq
    rows = GQA_REP * tq
    q = q_ref[0].reshape(rows, HEAD_DIM)

    def tile4(a):
        return jnp.concatenate([a] * GQA_REP, axis=1)

    ncp = kc_ref.shape[2]
    n_id = lax.broadcasted_iota(jnp.int32, (ncp, tq), 0)
    tq_pos = t0 + lax.broadcasted_iota(jnp.int32, (ncp, tq), 1)
    cmask = tile4((n_id * CMP_STRIDE + (CMP_BLOCK - 1) <= tq_pos) & (n_id < n_cmp))
    s = jnp.where(cmask, _dot_nt(kc_ref[0, 0], q), NEG_INF)
    m = jnp.max(s, axis=0, keepdims=True)
    p = jnp.where(cmask, jnp.exp2(s - m), 0.0)
    p = p / jnp.maximum(jnp.sum(p, axis=0, keepdims=True), 1e-30)
    o_cmp = _dot(vct_ref[0, 0], p.astype(BF16))
    p_grp = p[:, 0:tq] + p[:, tq:2 * tq] + p[:, 2 * tq:3 * tq] + p[:, 3 * tq:4 * tq]

    p_hi, p_lo = _split_bf16(p_grp)
    nsp = 8 * (-(-n_slc // 8))
    sc =(_dot(mt_ref[...], p_hi) + _dot(mt_ref[...], p_lo))[0:nsp]
    j_id = lax.broadcasted_iota(jnp.int32, (nsp, tq), 0)
    tpos = t0 + lax.broadcasted_iota(jnp.int32, (nsp, tq), 1)
    cur = tpos // SLC_BLOCK
    valid = (j_id * SLC_BLOCK <= tpos) & (j_id < n_slc)
    forced = (j_id == 0) | (j_id == cur) | (j_id == cur - 1)
    sc = jnp.where(valid, jnp.where(forced, FORCED_SCORE, sc), NEG_INF)
    cnt = jnp.zeros((nsp, tq), F32)
    for i in range(n_slc):
        ri = sc[i:i + 1, :]
        beats = (ri > sc) | ((ri == sc) & (j_id > i))
        cnt = cnt + jnp.where(beats, 1.0, 0.0)
    sel_bias = jnp.where((cnt < float(min(N_SELECT, n_slc))) & (sc > 0.5 * NEG_INF), 0.0, NEG_INF)
    if nsp < LANES:
        sel_bias = jnp.concatenate([sel_bias, jnp.zeros((LANES - nsp, tq), F32)], axis=0)
    sel_rows = sel_bias.T.astype(BF16)
    q_slc = jnp.concatenate([q, jnp.concatenate([sel_rows] * GQA_REP, axis=0)], axis=1)

    kp_off = lax.broadcasted_iota(jnp.int32, (tq, tq), 0)
    tq_off = lax.broadcasted_iota(jnp.int32, (tq, tq), 1)

    def slc_bias(k0, diag):
        return jnp.where(kp_off <= tq_off, 0.0, NEG_INF) if diag else None

    def win_bias(k0, diag):
        dist = (t0 + tq_off) - (k0 + kp_off)
        return jnp.where((dist >= 0) & (dist < WINDOW), 0.0, NEG_INF)

    def flash(k_ref, vt_ref, qmat, nblk, bias_fn):
        def update(its, carry, diag):
            m, l, acc = carry
            k0s = [pl.multiple_of((qi - it) * tq, tq) for it in its]
            sbs = []
            for k0 in k0s:
                sb = _dot_nt(k_ref[0, 0, pl.ds(k0, tq), :], qmat)
                bias = bias_fn(k0, diag)
                sbs.append(sb if bias is None else sb + tile4(bias))
            m_new = m
            for sb in sbs:
                m_new = jnp.maximum(m_new, jnp.max(sb, axis=0, keepdims=True))
            alpha = jnp.exp2(m - m_new)
            l = alpha * l
            acc = alpha * acc
            for k0, sb in zip(k0s, sbs):
                pb = jnp.exp2(sb - m_new)
                l = l + jnp.sum(pb, axis=0, keepdims=True)
                acc = acc + _dot(vt_ref[0, 0, :, pl.ds(k0, tq)], pb.astype(BF16))
            return m_new, l, acc

        init = (jnp.full((1, rows), NEG_INF, F32), jnp.zeros((1, rows), F32), jnp.zeros((HEAD_DIM, rows), F32))
        carry = update([0], init, True)
        npair = (nblk - 1) // 2
        carry = lax.fori_loop(0, npair, lambda j, c: update([1 + 2 * j, 2 + 2 * j], c, False), carry)
        carry = lax.cond((nblk - 1) % 2 == 1, lambda c: update([nblk - 1], c, False), lambda c: c, carry)
        _, l, acc = carry
        return acc / jnp.maximum(l, 1e-30)

    o_slc = flash(sk_ref, svt_ref, q_slc, qi + 1, slc_bias)
    o_win = flash(wk_ref, wvt_ref, q, jnp.minimum(qi, WINDOW // tq) + 1, win_bias)

    for r in range(GQA_REP):
        cs = slice(r * tq, (r + 1) * tq)
        c0 = g * GQA_REP + r
        o = (gate_ref[0, pl.ds(c0, 1), :] * o_cmp[:, cs] + gate_ref[0, pl.ds(N_HEADS + c0, 1), :] * o_slc[:, cs]
             + gate_ref[0, pl.ds(2 * N_HEADS + c0, 1), :] * o_win[:, cs])
        o_ref[0, :, r * HEAD_DIM:(r + 1) * HEAD_DIM] = o.T.astype(BF16)


def nsa_prompt_call(q, k_cmp, v_cmp_t, sk, sv_t, wk, wv_t, gates_t, tq):
    B, _, T, _ = q.shape
    ncp = k_cmp.shape[2]
    n_cmp = T // CMP_STRIDE - CMP_PARTS + 1
    n_slc = -(-T // SLC_BLOCK)
    mt = np.zeros((LANES, ncp), np.float32)
    mt[:n_slc, :n_cmp] = cmp_to_slc_matrix(n_cmp, n_slc).T
    mt = jnp.asarray(mt, BF16)
    sk_spec = pl.BlockSpec((1, 1, T, 2 * HEAD_DIM), lambda b, g, i: (b, g, 0, 0))
    k_spec = pl.BlockSpec((1, 1, T, HEAD_DIM), lambda b, g, i: (b, g, 0, 0))
    vt_spec = pl.BlockSpec((1, 1, HEAD_DIM, T), lambda b, g, i: (b, g, 0, 0))
    kern = functools.partial(_nsa_prompt_kernel, tq=tq, n_cmp=n_cmp, n_slc=n_slc)
    return pl.pallas_call(
        kern, grid=(B, N_KV_HEADS, T // tq),
        in_specs=[pl.BlockSpec((1, GQA_REP, tq, HEAD_DIM), lambda b, g, i: (b, g, i, 0)),
                  pl.BlockSpec((1, 1, ncp, HEAD_DIM), lambda b, g, i: (b, g, 0, 0)),
                  pl.BlockSpec((1, 1, HEAD_DIM, ncp), lambda b, g, i: (b, g, 0, 0)),
                  sk_spec, vt_spec, k_spec, vt_spec,
                  pl.BlockSpec((1, LANES, tq), lambda b, g, i: (b, 0, i)),
                  pl.BlockSpec((LANES, ncp), lambda b, g, i: (0, 0))],
        out_specs=pl.BlockSpec((1, tq, GQA_REP * HEAD_DIM), lambda b, g, i: (b, i, g)),
        out_shape=jax.ShapeDtypeStruct((B, T, N_HEADS * HEAD_DIM), BF16),
        compiler_params=_cparams(("parallel", "parallel", "arbitrary")),
        name="nsa_prompt")(q, k_cmp, v_cmp_t, sk, sv_t, wk, wv_t, gates_t, mt)


def _nsa_sample_select_kernel(q_ref, kc_ref, vc_ref, m_ref, ocmp_ref, idx_ref, ok_ref, *, nb, n_cmp, n_slc, qpos):
    rows = GQA_REP * nb
    q = q_ref[0].astype(F32).reshape(rows, HEAD_DIM).astype(BF16)
    ncp = kc_ref.shape[2]
    row_b = lax.broadcasted_iota(jnp.int32, (rows, 1), 0) % nb
    n_id = lax.broadcasted_iota(jnp.int32, (rows, ncp), 1)
    cmask = (n_id * CMP_STRIDE + (CMP_BLOCK - 1) <= qpos) & (n_id < n_cmp)
    o_acc = jnp.zeros((rows, HEAD_DIM), F32)
    pg = jnp.zeros((nb, ncp), F32)
    b_id = lax.broadcasted_iota(jnp.int32, (nb, 1), 0)
    for b in range(nb):
        p = _softmax_rows(_dot_nt(q, kc_ref[b, 0]), cmask)
        o_b = _dot(p.astype(BF16), vc_ref[b, 0])
        o_acc = o_acc + jnp.where(row_b == b, o_b, 0.0)
        p_sum = jnp.sum(p.reshape(GQA_REP, nb, ncp), axis=0)
        pg = pg + jnp.where(b_id == b, p_sum, 0.0)
    ocmp_ref[0] = o_acc.reshape(GQA_REP, nb, HEAD_DIM)

    p_hi, p_lo = _split_bf16(pg)
    sc = _dot(p_hi, m_ref[...]) + _dot(p_lo, m_ref[...])
    nsp = sc.shape[1]
    j_id = lax.broadcasted_iota(jnp.int32, (nb, nsp), 1)
    cur = qpos // SLC_BLOCK
    valid = (j_id * SLC_BLOCK <= qpos) & (j_id < n_slc)
    forced = (j_id == 0) | (j_id == cur) | (j_id == cur - 1)
    sc = jnp.where(valid, jnp.where(forced, FORCED_SCORE, sc), NEG_INF)
    lane = lax.broadcasted_iota(jnp.int32, (nb, LANES), 1)
    idx = jnp.zeros((nb, LANES), jnp.int32)
    okv = jnp.zeros((nb, LANES), jnp.int32)
    for i in range(min(N_SELECT, n_slc)):
        m = jnp.max(sc, axis=-1, keepdims=True)
        first = jnp.min(jnp.where(sc == m, j_id.astype(F32), float(nsp)), axis=-1, keepdims=True).astype(jnp.int32)
        idx = jnp.where(lane == i, first, idx)
        okv = jnp.where(lane == i, jnp.where(m > 0.5 * NEG_INF, 1, 0), okv)
        sc = jnp.where(j_id == first, -3.0e38, sc)
    idx_ref[0] = idx
    ok_ref[0] = okv


def nsa_sample_select_call(q, k_cmp, v_cmp, qpos):
    nb = q.shape[2]
    ncp = k_cmp.shape[2]
    n_cmp = (qpos + 1) // CMP_STRIDE - CMP_PARTS + 1
    n_slc = -(-(qpos + 1) // SLC_BLOCK)
    nsp = LANES * (-(-n_slc // LANES))
    m = np.zeros((ncp, nsp), np.float32)
    m[:n_cmp, :n_slc] = cmp_to_slc_matrix(n_cmp, n_slc)
    m = jnp.asarray(m, BF16)
    cmp_spec = pl.BlockSpec((nb, 1, ncp, HEAD_DIM), lambda g: (0, g, 0, 0))
    kern = functools.partial(_nsa_sample_select_kernel, nb=nb, n_cmp=n_cmp, n_slc=n_slc, qpos=qpos)
    return pl.pallas_call(
        kern, grid=(N_KV_HEADS,),
        in_specs=[pl.BlockSpec((1, GQA_REP, nb, HEAD_DIM), lambda g: (0, g, 0, 0)),
                  cmp_spec, cmp_spec,
                  pl.BlockSpec((ncp, nsp), lambda g: (0, 0))],
        out_specs=[pl.BlockSpec((1, GQA_REP, nb, HEAD_DIM), lambda g: (g, 0, 0, 0)),
                   pl.BlockSpec((1, nb, LANES), lambda g: (g, 0, 0)),
                   pl.BlockSpec((1, nb, LANES), lambda g: (g, 0, 0))],
        out_shape=[jax.ShapeDtypeStruct((N_KV_HEADS, GQA_REP, nb, HEAD_DIM), F32),
                   jax.ShapeDtypeStruct((N_KV_HEADS, nb, LANES), jnp.int32),
                   jax.ShapeDtypeStruct((N_KV_HEADS, nb, LANES), jnp.int32)],
        compiler_params=_cparams(("parallel",)),
        name="nsa_sample_select")(q, k_cmp, v_cmp, m)


def _nsa_sample_attend_kernel(idx_ref, ok_ref, pt_ref, q_ref, *rest, nb, nsel, n_past_blk, qpos):
    kblks, vblks = rest[:nsel], rest[nsel:2 * nsel]
    (wk_ref, wv_ref, skn_ref, svn_ref, wkn_ref, wvn_ref, ocmp_ref, gate_ref, o_ref) = rest[2 * nsel:]
    g = pl.program_id(0)
    b = pl.program_id(1)
    rows = GQA_REP * nb
    qf = q_ref[0].astype(F32).reshape(rows, HEAD_DIM)
    q = qf.astype(BF16)
    row_b = lax.broadcasted_iota(jnp.int32, (rows, 1), 0) % nb

    @pl.when(b == 0)
    def _():
        o_ref[...] = jnp.zeros_like(o_ref)

    head_rows = pl.ds(g, SLC_BLOCK, stride=N_KV_HEADS)
    kcat = jnp.concatenate([kb[0, head_rows, :] for kb in kblks], axis=0).astype(BF16)
    vcat = jnp.concatenate([vb[0, head_rows, :] for vb in vblks], axis=0).astype(BF16)
    nk = nsel * SLC_BLOCK
    slot = lax.broadcasted_iota(jnp.int32, (1, nk), 1) // SLC_BLOCK
    within = lax.broadcasted_iota(jnp.int32, (1, nk), 1) % SLC_BLOCK
    kpos = jnp.zeros((1, nk), jnp.int32)
    kok = jnp.zeros((1, nk), jnp.int32)
    new_ok = jnp.zeros((1, 1), jnp.int32)
    base = (g * nb + b) * LANES
    for s in range(nsel):
        bi = idx_ref[base + s]
        oks = ok_ref[base + s]
        kpos = jnp.where(slot == s, bi * SLC_BLOCK + within, kpos)
        kok = jnp.where(slot == s, jnp.where(bi < n_past_blk, oks, 0), kok)
        new_ok = jnp.maximum(new_ok, jnp.where(bi * SLC_BLOCK <= qpos, jnp.where(bi >= n_past_blk, oks, 0), 0))
    mask = (kok > 0) & (kpos <= qpos)
    s_past = jnp.where(mask, _dot_nt(q, kcat), NEG_INF)
    has_new = new_ok > 0
    s_new = jnp.where(has_new, jnp.sum(qf * skn_ref[0], axis=-1, keepdims=True), NEG_INF)
    m = jnp.maximum(jnp.max(s_past, axis=-1, keepdims=True), s_new)
    p_past = jnp.where(mask, jnp.exp2(s_past - m), 0.0)
    p_new = jnp.where(has_new, jnp.exp2(s_new - m), 0.0)
    den = jnp.maximum(jnp.sum(p_past, axis=-1, keepdims=True) + p_new, 1e-30)
    o_slc = (_dot(p_past.astype(BF16), vcat)
             + p_new * svn_ref[0]) / den

    nw = wk_ref.shape[1] // N_KV_HEADS
    win_rows = pl.ds(g, nw, stride=N_KV_HEADS)
    wpos = qpos - nw + lax.broadcasted_iota(jnp.int32, (1, nw), 1)
    wdist = qpos - wpos
    wmask = (wpos >= 0) & (wdist >= 0) & (wdist < WINDOW)
    s_w = jnp.where(wmask, _dot_nt(q, wk_ref[0, win_rows, :].astype(BF16)), NEG_INF)
    s_wn = jnp.sum(qf * wkn_ref[0], axis=-1, keepdims=True)
    mw = jnp.maximum(jnp.max(s_w, axis=-1, keepdims=True), s_wn)
    p_w = jnp.where(wmask, jnp.exp2(s_w - mw), 0.0)
    p_wn = jnp.exp2(s_wn - mw)
    den_w = jnp.maximum(jnp.sum(p_w, axis=-1, keepdims=True) + p_wn, 1e-30)
    o_win = (_dot(p_w.astype(BF16), wv_ref[0, win_rows, :].astype(BF16))
             + p_wn * wvn_ref[0]) / den_w

    gates = jnp.concatenate([gate_ref[...]] * GQA_REP, axis=0)
    lane = lax.broadcasted_iota(jnp.int32, (rows, LANES), 1)
    row_r = lax.broadcasted_iota(jnp.int32, (rows, 1), 0) // nb

    def gate_col(branch):
        c = branch * N_HEADS + g * GQA_REP + row_r
        return jnp.sum(jnp.where(lane == c, gates, 0.0), axis=-1, keepdims=True)

    o = (gate_col(0) * ocmp_ref[0].reshape(rows, HEAD_DIM) + gate_col(1) * o_slc + gate_col(2) * o_win)
    o_ref[0] += jnp.where(row_b == b, o, 0.0).reshape(GQA_REP, nb, HEAD_DIM)


def nsa_sample_attend_call(idx, ok, page_table, q, pool_k, pool_v, win_k, win_v, sk_new, sv_new, wk_new, wv_new,
                           o_cmp, gates, qpos):
    nb = q.shape[2]
    nsel = min(N_SELECT, -(-(qpos + 1) // SLC_BLOCK))
    n_past_blk = page_table.shape[1] * PAGE_SIZE // SLC_BLOCK
    half = PAGE_SIZE // SLC_BLOCK

    def blk_map(g, b, idx_r, ok_r, pt_r, s):
        bi = jnp.minimum(idx_r[(g * nb + b) * LANES + s], n_past_blk - 1)
        return (pt_r[b, bi // half], bi % half, 0)

    blk_specs = [pl.BlockSpec((1, SLC_BLOCK * N_KV_HEADS, HEAD_DIM), functools.partial(blk_map, s=s))
                 for s in range(nsel)]
    win_spec = pl.BlockSpec((1, win_k.shape[1], HEAD_DIM), lambda g, b, *_: (b, 0, 0))
    new_spec = pl.BlockSpec((1, 1, HEAD_DIM), lambda g, b, *_: (b * N_KV_HEADS + g, 0, 0))
    qo_spec = pl.BlockSpec((1, GQA_REP, nb, HEAD_DIM), lambda g, b, *_: (0, g, 0, 0))
    in_specs = ([qo_spec] + blk_specs + blk_specs + [win_spec, win_spec, new_spec, new_spec, new_spec, new_spec,
                pl.BlockSpec((1, GQA_REP, nb, HEAD_DIM), lambda g, b, *_: (g, 0, 0, 0)),
                pl.BlockSpec((nb, LANES), lambda g, b, *_: (0, 0))])
    gs = pltpu.PrefetchScalarGridSpec(
        num_scalar_prefetch=3, grid=(N_KV_HEADS, nb), in_specs=in_specs,
        out_specs=pl.BlockSpec((1, GQA_REP, nb, HEAD_DIM), lambda g, b, *_: (g, 0, 0, 0)))
    kern = functools.partial(_nsa_sample_attend_kernel, nb=nb, nsel=nsel, n_past_blk=n_past_blk, qpos=qpos)
    return pl.pallas_call(
        kern, grid_spec=gs,
        out_shape=jax.ShapeDtypeStruct((N_KV_HEADS, GQA_REP, nb, HEAD_DIM), F32),
        compiler_params=_cparams(("parallel", "arbitrary")),
        name="nsa_sample_attend")(idx.reshape(-1), ok.reshape(-1), page_table, q,
                                  *([pool_k] * nsel), *([pool_v] * nsel), win_k, win_v,
                                  sk_new, sv_new, wk_new, wv_new, o_cmp, gates)


def _rope_tables(pos):
    half = HEAD_DIM // 2
    inv_freq = ROPE_THETA ** (-jnp.arange(half, dtype=F32) / half)
    ang = pos.astype(F32)[:, None] * inv_freq[None, :]
    cos, sin = jnp.cos(ang), jnp.sin(ang)
    return jnp.concatenate([cos, cos], axis=-1), jnp.concatenate([-sin, sin], axis=-1)


def _trunk(x2, p3, pos, h0r, h0i, w, lay, kv_ctx):
    nb, T, tc, C, R, tm = lay
    depth = w["norm_mix_g"].shape[0]
    n_a = w["ssm_a_re"].shape[0]
    d = x2.shape[1]
    cosf, sinf = _rope_tables(pos)
    ssm_re, ssm_im = [], []
    kv = None
    for i in range(depth):
        if i < n_a:
            prm = s5_params(w["ssm_a_re"][i], w["ssm_a_im"][i], w["ssm_log_dt"][i], w["ssm_b_re"][i],
                            w["ssm_b_im"][i], w["ssm_c_re"][i], w["ssm_c_im"][i], w["ssm_d"][i], nb)
            g3, hr, hi = s5_mixer(x2.reshape(nb, T, d), w["norm_mix_g"][i][None], prm, h0r[i], h0i[i], tc)
            ssm_re.append(rows_to_state(hr, nb))
            ssm_im.append(rows_to_state(hi, nb))
            x2 = glu_call(g3.reshape(nb * T, d), w["ssm_w_glu"], w["ssm_b_glu"], x2, i, tm)
        else:
            j = i - n_a
            q, gates_t = q_call(x2, w["norm_mix_g3"], w["nsa_w_q"], w["nsa_wgt_t"][j], w["nsa_bgt"][j],
                                cosf, sinf, i, j, C, R, tm)
            o2 = kv_ctx["attend"](q, gates_t, kv)
            x2 = oproj_call(o2, w["nsa_w_o"], x2, j, tm)
        x2 = ffn_call(x2, w["norm_ffn_g3"], w["ffn_w_gate"], w["ffn_w_up"], w["ffn_w_down"], i, tm, 512)
        x2 = ple_call(x2, w["norm_ple_g3"], p3, w["ple_w_gate"], w["ple_w_proj"], w["norm_final_g"][None], i, tm,
                      final_norm=(i == depth - 1))
        if i == n_a - 1:
            kv = kv_ctx["build"](x2, cosf, sinf)
    return x2, jnp.stack(ssm_re), jnp.stack(ssm_im), kv["new"]


def _cmp_weights(w1, b1, w2):
    w1p = w1.reshape(CMP_PARTS, CMP_STRIDE * HEAD_DIM, HEAD_DIM)
    return jnp.concatenate([w1p[0], w1p[1]], axis=1).astype(BF16), b1[None], w2.astype(BF16)


def kernel(x_prompt, x_sample, state_ssm_re, state_ssm_im, cache_cmp_k, cache_cmp_v, cache_slc_k, cache_slc_v, state_win_k, state_win_v, page_table, p_prompt, p_sample, norm_mix_g, norm_ffn_g, norm_ple_g, norm_kv_g, norm_final_g, ssm_a_re, ssm_a_im, ssm_log_dt, ssm_b_re, ssm_b_im, ssm_c_re, ssm_c_im, ssm_d, ssm_w_glu, ssm_b_glu, kv_w, cmp_k_w1, cmp_k_b1, cmp_k_w2, cmp_v_w1, cmp_v_b1, cmp_v_w2, nsa_w_q, nsa_w_gate, nsa_b_gate, nsa_w_o, ffn_w_gate, ffn_w_up, ffn_w_down, ple_w_gate, ple_w_proj):
    B, T, D = x_prompt.shape
    nbs, ts, _ = x_sample.shape
    assert ts == 1
    n_a = ssm_a_re.shape[0]
    past = page_table.shape[1] * PAGE_SIZE
    gate_pad = LANES - nsa_w_gate.shape[-1]
    w = dict(norm_mix_g=norm_mix_g, norm_mix_g3=norm_mix_g[:, None], norm_ffn_g3=norm_ffn_g[:, None],
             norm_ple_g3=norm_ple_g[:, None], norm_final_g=norm_final_g,
             ssm_a_re=ssm_a_re, ssm_a_im=ssm_a_im, ssm_log_dt=ssm_log_dt, ssm_b_re=ssm_b_re, ssm_b_im=ssm_b_im,
             ssm_c_re=ssm_c_re, ssm_c_im=ssm_c_im, ssm_d=ssm_d,
             ssm_w_glu=ssm_w_glu.astype(BF16), ssm_b_glu=ssm_b_glu[:, None],
             nsa_w_q=nsa_w_q.astype(BF16),
             nsa_wgt_t=jnp.pad(jnp.swapaxes(nsa_w_gate, 1, 2), ((0, 0), (0, gate_pad), (0, 0))).astype(BF16),
             nsa_bgt=jnp.pad(nsa_b_gate, ((0, 0), (0, gate_pad)))[..., None],
             nsa_w_o=nsa_w_o.astype(BF16),
             ffn_w_gate=ffn_w_gate.astype(BF16), ffn_w_up=ffn_w_up.astype(BF16), ffn_w_down=ffn_w_down.astype(BF16),
             ple_w_gate=ple_w_gate.astype(BF16), ple_w_proj=ple_w_proj.astype(BF16))
    kvw = kv_w.astype(BF16)
    kvg = norm_kv_g[None]
    ck_w = _cmp_weights(cmp_k_w1, cmp_k_b1, cmp_k_w2)
    cv_w = _cmp_weights(cmp_v_w1, cmp_v_b1, cmp_v_w2)

    tm_p = min(512, T)
    tq = min(256, T)
    pos_p = jnp.arange(T, dtype=jnp.int32)

    def build_prompt(x2, cosf, sinf):
        ck, cv, sk, sv, wk, wv, skh, wkh, svt, wvt = kv_call(x2, kvg, kvw, cosf, sinf, B, T, tm_p, True)
        slab = min(SLAB_ROWS, T)
        nslab = T // slab
        nu = T // CMP_STRIDE
        ccos, csin = _rope_tables(jnp.arange(nu, dtype=jnp.int32) * CMP_STRIDE + (CMP_BLOCK - 1))
        rows3 = lambda a: a.reshape(B, T * N_KV_HEADS, HEAD_DIM)
        k_cmp = compress_call(rows3(ck), None, *ck_w, ccos, csin, True, B, nslab, slab)
        v_cmp_t = compress_call(rows3(cv), None, cv_w[0], cv_w[1], cv_w[2].T, ccos, csin, False, B, nslab, slab,
                                transpose_out=True)
        r4 = lambda a: a.reshape(B, T, N_KV_HEADS, HEAD_DIM)
        ck, cv, sk, sv, wk, wv = [r4(a) for a in (ck, cv, sk, sv, wk, wv)]
        wb = min(WINDOW, T)
        new = (ck, cv, sk, sv, wk[:, T - wb:], wv[:, T - wb:])
        return dict(k_cmp=k_cmp, v_cmp_t=v_cmp_t, sk=skh, sv_t=svt, wk=wkh, wv_t=wvt, new=new)

    def attend_prompt(q, gates_t, kv):
        o = nsa_prompt_call(q, kv["k_cmp"], kv["v_cmp_t"], kv["sk"], kv["sv_t"], kv["wk"], kv["wv_t"], gates_t, tq)
        return o.reshape(B * T, N_HEADS * HEAD_DIM)

    h0 = jnp.zeros((n_a, D // PAIR_CH, 2 * B, PAIR_ST), F32)
    y_p, sre_p, sim_p, new_p = _trunk(x_prompt.reshape(B * T, D), p_prompt.reshape(-1, B * T, p_prompt.shape[-1]),
                                      pos_p, h0, h0, w, (B, T, min(128, T), B, T, tm_p),
                                      dict(build=build_prompt, attend=attend_prompt))
    y_p = y_p.reshape(B, T, D)

    pool3 = lambda a: a.reshape(a.shape[0], -1, HEAD_DIM)
    qpos = past
    pos_s = jnp.full((nbs,), past, jnp.int32)

    def build_sample(x2, cosf, sinf):
        ck, cv, sk, sv, wk, wv = kv_call(x2, kvg, kvw, cosf, sinf, 1, nbs, nbs, False)
        slab = min(SLAB_ROWS, past)
        nslab = past // slab
        nu = past // CMP_STRIDE
        ccos, csin = _rope_tables(jnp.arange(nu, dtype=jnp.int32) * CMP_STRIDE + (CMP_BLOCK - 1))
        k_cmp = compress_call(pool3(cache_cmp_k), page_table, *ck_w, ccos, csin, True, nbs, nslab, slab)
        v_cmp = compress_call(pool3(cache_cmp_v), page_table, *cv_w, ccos, csin, False, nbs, nslab, slab)
        r4 = lambda a: a.reshape(nbs, 1, N_KV_HEADS, HEAD_DIM)
        win_k = jnp.concatenate([state_win_k[:, 1:], r4(wk)], axis=1)
        win_v = jnp.concatenate([state_win_v[:, 1:], r4(wv)], axis=1)
        new = (r4(ck), r4(cv), r4(sk), r4(sv), win_k, win_v)
        return dict(k_cmp=k_cmp, v_cmp=v_cmp, sk=sk, sv=sv, wk=wk, wv=wv, new=new)

    def attend_sample(q, gates_t, kv):
        o_cmp, idx, ok = nsa_sample_select_call(q, kv["k_cmp"], kv["v_cmp"], qpos)
        rows1 = lambda a: a.reshape(nbs * N_KV_HEADS, 1, HEAD_DIM)
        o = nsa_sample_attend_call(idx, ok, page_table, q, pool3(cache_slc_k), pool3(cache_slc_v),
                                   pool3(state_win_k), pool3(state_win_v),
                                   rows1(kv["sk"]), rows1(kv["sv"]), rows1(kv["wk"]), rows1(kv["wv"]),
                                   o_cmp, gates_t[0].T, qpos)
        return jnp.transpose(o, (2, 0, 1, 3)).reshape(nbs, N_HEADS * HEAD_DIM).astype(BF16)

    h0r = jnp.stack([state_to_rows(state_ssm_re[i], nbs) for i in range(n_a)])
    h0i = jnp.stack([state_to_rows(state_ssm_im[i], nbs) for i in range(n_a)])
    y_s, sre_s, sim_s, new_s = _trunk(x_sample.reshape(nbs, D), p_sample.reshape(-1, nbs, p_sample.shape[-1]),
                                      pos_s, h0r, h0i, w, (nbs, 1, 1, 1, nbs, nbs),
                                      dict(build=build_sample, attend=attend_sample))
    y_s = y_s.reshape(nbs, 1, D)

    ck_p, cv_p, sk_p, sv_p, wk_p, wv_p = new_p
    ck_s, cv_s, sk_s, sv_s, wk_s, wv_s = new_s
    return (y_p, y_s, sre_p, sim_p, ck_p, cv_p, sk_p, sv_p, wk_p, wv_p,
            sre_s, sim_s, ck_s, cv_s, sk_s, sv_s, wk_s, wv_s)
```

```python
import functools

import numpy as np
import jax
import jax.numpy as jnp
from jax import lax
from jax.experimental import pallas as pl
from jax.experimental.pallas import tpu as pltpu

F32 = jnp.float32
BF16 = jnp.bfloat16

N_HEADS = 16
N_KV_HEADS = 4
GQA_REP = N_HEADS // N_KV_HEADS
HEAD_DIM = 128
SSM_GROUP = 16
SSM_STATE = 64
CMP_BLOCK = 32
CMP_STRIDE = 16
CMP_PARTS = CMP_BLOCK // CMP_STRIDE
SLC_BLOCK = 64
N_SELECT = 16
WINDOW = 512
PAGE_SIZE = 128
ROPE_THETA = 10000.0
ATTN_SCALE = HEAD_DIM ** -0.5
EXP2_SCALE = ATTN_SCALE * float(np.log2(np.e))
NORM_EPS = 1e-6
NEG_INF = -1e30
FORCED_SCORE = 1e9

LANES = 128
KV_LANES = N_KV_HEADS * HEAD_DIM
PAIR_CH = 2 * LANES
PAIR_ST = (LANES // SSM_GROUP) * SSM_STATE
SLAB_ROWS = 2048
VMEM_LIMIT = 56 * 1024 * 1024


def _cparams(sem, vmem=VMEM_LIMIT):
    return pltpu.CompilerParams(dimension_semantics=sem, vmem_limit_bytes=vmem)


def _rms(x, g):
    ms = jnp.mean(x * x, axis=-1, keepdims=True)
    return x * lax.rsqrt(ms + NORM_EPS) * g


def _rope(x, cosf, sinf):
    return x * cosf + pltpu.roll(x, HEAD_DIM // 2, axis=1) * sinf


def _dot(a, b):
    return jnp.dot(a, b, preferred_element_type=F32)


def _dot_nt(a, b):
    return lax.dot_general(a, b, (((1,), (1,)), ((), ())), preferred_element_type=F32)


def _s5_kernel(x_ref, gn_ref, wre_ref, wim_ref, cre_ref, cim_ref, ar_ref, ai_ref, d_ref, h0r_ref, h0i_ref,
               g_ref, hr_ref, hi_ref, hn_s, xr_s, xi_s, sr_s, si_s, *, tc, nb, npair):
    @pl.when(pl.program_id(0) == 0)
    def _():
        hr_ref[...] = h0r_ref[...]
        hi_ref[...] = h0i_ref[...]

    r2 = 2 * nb
    rows = tc * r2
    for b in range(nb):
        hn_s[b] = _rms(x_ref[b], gn_ref[...])
    keep = ((lax.broadcasted_iota(jnp.int32, (r2, PAIR_CH), 0) // nb == 1)
            == (lax.broadcasted_iota(jnp.int32, (r2, PAIR_CH), 1) >= LANES))
    lane_hi = lax.broadcasted_iota(jnp.int32, (tc, PAIR_CH), 1) >= LANES

    for k in range(npair):
        cols = slice(k * PAIR_CH, (k + 1) * PAIR_CH)

        if tc > 1:
            hd = jnp.concatenate([hn_s[:, :, cols], hn_s[:, :, cols]], axis=0)
            lhs = jnp.where(keep[None], jnp.swapaxes(hd, 0, 1), 0.0).reshape(rows, PAIR_CH).astype(BF16)
        else:
            v = hn_s[:, 0, cols]
            lhs = jnp.where(keep, jnp.concatenate([v, v], axis=0), 0.0).astype(BF16)
        xr_s[...] = _dot(lhs, wre_ref[k]).reshape(tc, r2, PAIR_ST)
        xi_s[...] = _dot(lhs, wim_ref[k]).reshape(tc, r2, PAIR_ST)
        ar = ar_ref[k]
        ai = ai_ref[k]

        def step(t, carry):
            hr, hi = carry
            nr = ar * hr - ai * hi + xr_s[t]
            ni = ar * hi + ai * hr + xi_s[t]
            sr_s[t] = nr
            si_s[t] = ni
            return nr, ni

        hr, hi = lax.fori_loop(0, tc, step, (hr_ref[k], hi_ref[k]), unroll=min(8, tc))
        hr_ref[k] = hr
        hi_ref[k] = hi
        h_re = sr_s[...].reshape(rows, PAIR_ST).astype(BF16)
        h_im = si_s[...].reshape(rows, PAIR_ST).astype(BF16)
        y = (_dot(h_re, cre_ref[k]) - _dot(h_im, cim_ref[k])).reshape(tc, r2, PAIR_CH)
        if tc > 1:
            y = jnp.swapaxes(y, 0, 1)
            per_row = lambda r: y[r]
        else:
            per_row = lambda r: y[:, r, :]
        for b in range(nb):
            yb = jnp.where(lane_hi, per_row(nb + b), per_row(b))
            yb = yb + d_ref[:, cols] * hn_s[b, :, cols]
            g_ref[b, :, cols] = jax.nn.gelu(yb).astype(BF16)


def s5_mixer(x3, gn, prm, h0r, h0i, tc):
    nb, T, d = x3.shape
    r2 = 2 * nb
    npair = d // PAIR_CH
    full3 = lambda s: (0, 0, 0)
    st_shape = (npair, r2, PAIR_ST)
    kern = functools.partial(_s5_kernel, tc=tc, nb=nb, npair=npair)
    return pl.pallas_call(
        kern, grid=(T // tc,),
        in_specs=[pl.BlockSpec((nb, tc, d), lambda s: (0, s, 0)),
                  pl.BlockSpec((1, d), lambda s: (0, 0)),
                  pl.BlockSpec((npair, PAIR_CH, PAIR_ST), full3),
                  pl.BlockSpec((npair, PAIR_CH, PAIR_ST), full3),
                  pl.BlockSpec((npair, PAIR_ST, PAIR_CH), full3),
                  pl.BlockSpec((npair, PAIR_ST, PAIR_CH), full3),
                  pl.BlockSpec(st_shape, full3),
                  pl.BlockSpec(st_shape, full3),
                  pl.BlockSpec((1, d), lambda s: (0, 0)),
                  pl.BlockSpec(st_shape, full3),
                  pl.BlockSpec(st_shape, full3)],
        out_specs=[pl.BlockSpec((nb, tc, d), lambda s: (0, s, 0)),
                   pl.BlockSpec(st_shape, full3),
                   pl.BlockSpec(st_shape, full3)],
        out_shape=[jax.ShapeDtypeStruct((nb, T, d), BF16),
                   jax.ShapeDtypeStruct(st_shape, F32),
                   jax.ShapeDtypeStruct(st_shape, F32)],
        scratch_shapes=[pltpu.VMEM((nb, tc, d), F32)] + [pltpu.VMEM((tc, r2, PAIR_ST), F32)] * 4,
        compiler_params=_cparams(("arbitrary",)),
        name="s5_mixer")(x3, gn, prm["wre"], prm["wim"], prm["cre"], prm["cim"],
                         prm["ar"], prm["ai"], prm["d"], h0r, h0i)


def s5_params(a_re, a_im, log_dt, b_re, b_im, c_re, c_im, d, nb):
    G, P, H = b_re.shape
    gpc = LANES // H
    nchunk = G // gpc
    dt = jnp.exp(log_dt)[:, None]
    mag = jnp.exp(dt * a_re)
    ab_re, ab_im = mag * jnp.cos(dt * a_im), mag * jnp.sin(dt * a_im)
    den = a_re * a_re + a_im * a_im
    nr, ni = ab_re - 1.0, ab_im
    cf_re = (nr * a_re + ni * a_im) / den
    cf_im = (ni * a_re - nr * a_im) / den
    bb_re = cf_re[..., None] * b_re - cf_im[..., None] * b_im
    bb_im = cf_re[..., None] * b_im + cf_im[..., None] * b_re
    eye = jnp.eye(gpc, dtype=F32)

    def blk_in(bb):
        w = jnp.einsum("cgph,gk->cghkp", bb.reshape(nchunk, gpc, P, H), eye)
        return w.reshape(nchunk // 2, PAIR_CH, gpc * P).astype(BF16)

    def blk_out(cc):
        w = jnp.einsum("cghp,gk->ckpgh", cc.reshape(nchunk, gpc, H, P), eye)
        w = w.reshape(nchunk // 2, 2, gpc * P, LANES)
        return jnp.swapaxes(w, 1, 2).reshape(nchunk // 2, gpc * P, PAIR_CH).astype(BF16)

    def rows(a):
        a = a.reshape(nchunk // 2, 2, 1, gpc * P)
        return jnp.broadcast_to(a, (nchunk // 2, 2, nb, gpc * P)).reshape(nchunk // 2, 2 * nb, gpc * P)

    return dict(wre=blk_in(bb_re), wim=blk_in(bb_im), cre=blk_out(c_re), cim=blk_out(c_im),
                ar=rows(ab_re), ai=rows(ab_im), d=d[None])


def state_to_rows(h, nb):
    npair = h.shape[1] * h.shape[2] // PAIR_ST // 2
    return jnp.transpose(h.reshape(nb, npair, 2, PAIR_ST), (1, 2, 0, 3)).reshape(npair, 2 * nb, PAIR_ST)


def rows_to_state(h, nb):
    npair = h.shape[0]
    h = jnp.transpose(h.reshape(npair, 2, nb, PAIR_ST), (2, 0, 1, 3))
    return h.reshape(nb, npair * 2 * PAIR_ST // SSM_STATE, SSM_STATE)


COL_CHUNK = 512


def _resident(block_shape, index_map):
    return pl.BlockSpec(block_shape, index_map, pipeline_mode=pl.Buffered(1))


def _glu_kernel(a_ref, w_ref, b_ref, x_ref, o_ref):
    a = a_ref[...]
    d = x_ref.shape[1]
    for c in range(0, d, COL_CHUNK):
        z1 = _dot(a, w_ref[:, c:c + COL_CHUNK]) + b_ref[:, c:c + COL_CHUNK]
        z2 = _dot(a, w_ref[:, d + c:d + c + COL_CHUNK]) + b_ref[:, d + c:d + c + COL_CHUNK]
        o_ref[:, c:c + COL_CHUNK] = x_ref[:, c:c + COL_CHUNK] + z1 * jax.nn.sigmoid(z2)


def glu_call(g2, w_glu, b_glu, x2, li, tm):
    m, d = x2.shape
    return pl.pallas_call(
        _glu_kernel, grid=(m // tm,),
        in_specs=[pl.BlockSpec((tm, d), lambda i: (i, 0)),
                  _resident((None, d, 2 * d), lambda i: (li, 0, 0)),
                  _resident((None, 1, 2 * d), lambda i: (li, 0, 0)),
                  pl.BlockSpec((tm, d), lambda i: (i, 0))],
        out_specs=pl.BlockSpec((tm, d), lambda i: (i, 0)),
        out_shape=jax.ShapeDtypeStruct(x2.shape, F32),
        compiler_params=_cparams(("parallel",)),
        name="s5_glu")(g2, w_glu, b_glu, x2)


def _ffn_kernel(x_ref, g_ref, wg_ref, wu_ref, wd_ref, o_ref, hn_s, acc_s, *, nf):
    f = pl.program_id(1)

    @pl.when(f == 0)
    def _():
        hn_s[...] = _rms(x_ref[...], g_ref[...]).astype(BF16)
        acc_s[...] = jnp.zeros_like(acc_s)

    h = hn_s[...]
    a = _dot(h, wg_ref[...])
    u = _dot(h, wu_ref[...])
    act = (a * jax.nn.sigmoid(a) * u).astype(BF16)
    acc_s[...] += _dot(act, wd_ref[...])

    @pl.when(f == nf - 1)
    def _():
        o_ref[...] = x_ref[...] + acc_s[...]


def ffn_call(x2, g, wg, wu, wd, li, tm, tf):
    m, d = x2.shape
    ff = wg.shape[-1]
    nf = ff // tf
    return pl.pallas_call(
        functools.partial(_ffn_kernel, nf=nf), grid=(m // tm, nf),
        in_specs=[pl.BlockSpec((tm, d), lambda i, f: (i, 0)),
                  pl.BlockSpec((None, 1, d), lambda i, f: (li, 0, 0)),
                  pl.BlockSpec((None, d, tf), lambda i, f: (li, 0, f)),
                  pl.BlockSpec((None, d, tf), lambda i, f: (li, 0, f)),
                  pl.BlockSpec((None, tf, d), lambda i, f: (li, f, 0))],
        out_specs=pl.BlockSpec((tm, d), lambda i, f: (i, 0)),
        out_shape=jax.ShapeDtypeStruct(x2.shape, F32),
        scratch_shapes=[pltpu.VMEM((tm, d), BF16), pltpu.VMEM((tm, d), F32)],
        compiler_params=_cparams(("parallel", "arbitrary")),
        name="ffn")(x2, g, wg, wu, wd)


def _ple_kernel(x_ref, g_ref, p_ref, wg_ref, wp_ref, gf_ref, o_ref, *, final_norm):
    hn = _rms(x_ref[...], g_ref[...]).astype(BF16)
    pe = p_ref[...].astype(BF16)
    for c in range(0, x_ref.shape[1], COL_CHUNK):
        cs = slice(c, c + COL_CHUNK)
        gate = jax.nn.sigmoid(_dot(hn, wg_ref[:, cs]))
        o_ref[:, cs] = x_ref[:, cs] + gate * _dot(pe, wp_ref[:, cs])
    if final_norm:
        o_ref[...] = _rms(o_ref[...], gf_ref[...])


def ple_call(x2, g, p3, wg, wp, gf, li, tm, final_norm):
    m, d = x2.shape
    pd = wp.shape[1]
    return pl.pallas_call(
        functools.partial(_ple_kernel, final_norm=final_norm), grid=(m // tm,),
        in_specs=[pl.BlockSpec((tm, d), lambda i: (i, 0)),
                  _resident((None, 1, d), lambda i: (li, 0, 0)),
                  pl.BlockSpec((None, tm, pd), lambda i: (li, i, 0)),
                  _resident((None, d, d), lambda i: (li, 0, 0)),
                  _resident((None, pd, d), lambda i: (li, 0, 0)),
                  _resident((1, d), lambda i: (0, 0))],
        out_specs=pl.BlockSpec((tm, d), lambda i: (i, 0)),
        out_shape=jax.ShapeDtypeStruct(x2.shape, F32),
        compiler_params=_cparams(("parallel",)),
        name="ple")(x2, g, p3, wg, wp, gf)


def _kv_kernel(x_ref, g_ref, w_ref, cos_ref, sin_ref, *outs, heads_out, nt):
    hn = _rms(x_ref[...], g_ref[...]).astype(BF16)
    tm = x_ref.shape[0]
    if heads_out:
        t_row = (pl.program_id(0) % nt) * tm + lax.broadcasted_iota(jnp.int32, (tm, LANES), 0)
        blk_onehot = jnp.where(lax.broadcasted_iota(jnp.int32, (tm, LANES), 1) == t_row // SLC_BLOCK, 1.0, 0.0)
        for h in range(N_KV_HEADS):
            outs[6][0, h, :, HEAD_DIM:] = blk_onehot.astype(BF16)
    for k in range(6):
        z = _dot(hn, w_ref[:, k * KV_LANES:(k + 1) * KV_LANES])
        for h in range(N_KV_HEADS):
            zh = z[:, h * HEAD_DIM:(h + 1) * HEAD_DIM]
            if k in (2, 4):
                zh = _rope(zh, cos_ref[...], sin_ref[...])
            outs[k][pl.ds(h, tm, stride=N_KV_HEADS), :] = zh
            if heads_out and k == 2:
                outs[6][0, h, :, :HEAD_DIM] = zh.astype(BF16)
            if heads_out and k == 4:
                outs[7][0, h] = zh.astype(BF16)
            if heads_out and k in (3, 5):
                outs[8 + (k == 5)][0, h] = zh.T.astype(BF16)


def kv_call(x2, g, kv_w, cosf, sinf, C, R, tm, heads_out):
    m, d = x2.shape
    nt = R // tm
    out_shape = [jax.ShapeDtypeStruct((m * N_KV_HEADS, HEAD_DIM), F32)] * 6
    out_specs = [pl.BlockSpec((tm * N_KV_HEADS, HEAD_DIM), lambda i: (i, 0))] * 6
    if heads_out:
        assert -(-R // SLC_BLOCK) <= LANES
        for width in (2 * HEAD_DIM, HEAD_DIM):
            out_shape += [jax.ShapeDtypeStruct((C, N_KV_HEADS, R, width), BF16)]
            out_specs += [pl.BlockSpec((1, N_KV_HEADS, tm, width), lambda i: (i // nt, 0, i % nt, 0))]
        out_shape += [jax.ShapeDtypeStruct((C, N_KV_HEADS, HEAD_DIM, R), BF16)] * 2
        out_specs += [pl.BlockSpec((1, N_KV_HEADS, HEAD_DIM, tm), lambda i: (i // nt, 0, 0, i % nt))] * 2
    return pl.pallas_call(
        functools.partial(_kv_kernel, heads_out=heads_out, nt=nt), grid=(m // tm,),
        in_specs=[pl.BlockSpec((tm, d), lambda i: (i, 0)),
                  _resident((1, d), lambda i: (0, 0)),
                  _resident((d, 6 * KV_LANES), lambda i: (0, 0)),
                  pl.BlockSpec((tm, HEAD_DIM), lambda i: (i % nt, 0)),
                  pl.BlockSpec((tm, HEAD_DIM), lambda i: (i % nt, 0))],
        out_specs=out_specs, out_shape=out_shape,
        compiler_params=_cparams(("parallel",)),
        name="kv_proj")(x2, g, kv_w, cosf, sinf)


def _q_kernel(x_ref, g_ref, w_ref, wgt_ref, bgt_ref, cos_ref, sin_ref, q_ref, gate_ref):
    hn = _rms(x_ref[...], g_ref[...]).astype(BF16)
    gate_ref[0] = jax.nn.sigmoid(_dot_nt(wgt_ref[...], hn) + bgt_ref[...])
    hpc = COL_CHUNK // HEAD_DIM
    for c in range(N_HEADS // hpc):
        z = _dot(hn, w_ref[:, c * COL_CHUNK:(c + 1) * COL_CHUNK])
        for h in range(hpc):
            zh = _rope(z[:, h * HEAD_DIM:(h + 1) * HEAD_DIM], cos_ref[...], sin_ref[...])
            q_ref[0, c * hpc + h] = (zh * EXP2_SCALE).astype(BF16)


def q_call(x2, g, w_q, wgt_t, bgt, cosf, sinf, li, wi, C, R, tm):
    m, d = x2.shape
    nt = R // tm
    return pl.pallas_call(
        _q_kernel, grid=(m // tm,),
        in_specs=[pl.BlockSpec((tm, d), lambda i: (i, 0)),
                  _resident((None, 1, d), lambda i: (li, 0, 0)),
                  _resident((None, d, N_HEADS * HEAD_DIM), lambda i: (wi, 0, 0)),
                  _resident((LANES, d), lambda i: (0, 0)),
                  _resident((LANES, 1), lambda i: (0, 0)),
                  pl.BlockSpec((tm, HEAD_DIM), lambda i: (i % nt, 0)),
                  pl.BlockSpec((tm, HEAD_DIM), lambda i: (i % nt, 0))],
        out_specs=[pl.BlockSpec((1, N_HEADS, tm, HEAD_DIM), lambda i: (i // nt, 0, i % nt, 0)),
                   pl.BlockSpec((1, LANES, tm), lambda i: (i // nt, 0, i % nt))],
        out_shape=[jax.ShapeDtypeStruct((C, N_HEADS, R, HEAD_DIM), BF16),
                   jax.ShapeDtypeStruct((C, LANES, R), F32)],
        compiler_params=_cparams(("parallel",)),
        name="q_proj")(x2, g, w_q, wgt_t, bgt, cosf, sinf)


def _oproj_kernel(o_ref, w_ref, x_ref, out_ref):
    o = o_ref[...]
    for c in range(0, x_ref.shape[1], COL_CHUNK):
        out_ref[:, c:c + COL_CHUNK] = x_ref[:, c:c + COL_CHUNK] + _dot(o, w_ref[:, c:c + COL_CHUNK])


def oproj_call(o2, w_o, x2, li, tm):
    m, d = x2.shape
    return pl.pallas_call(
        _oproj_kernel, grid=(m // tm,),
        in_specs=[pl.BlockSpec((tm, d), lambda i: (i, 0)),
                  _resident((None, d, d), lambda i: (li, 0, 0)),
                  pl.BlockSpec((tm, d), lambda i: (i, 0))],
        out_specs=pl.BlockSpec((tm, d), lambda i: (i, 0)),
        out_shape=jax.ShapeDtypeStruct(x2.shape, F32),
        compiler_params=_cparams(("parallel",)),
        name="o_proj")(o2, w_o, x2)


def _compress_kernel(*refs, npage, rope, nslab, upb, slab, paged, transpose_out):
    if paged:
        refs = refs[1:]
    pages = refs[:npage]
    w1_ref, b1_ref, w2_ref, cos_ref, sin_ref, o_ref, p_s = refs[npage:]
    i = pl.program_id(1)
    upp = upb // npage
    unit_rows = CMP_STRIDE * N_KV_HEADS
    swapped = [jnp.swapaxes(pg[0].reshape(upp, unit_rows, HEAD_DIM), 0, 1) for pg in pages]
    for g in range(N_KV_HEADS):
        cols = []
        for s in range(CMP_STRIDE):
            parts = [sw[s * N_KV_HEADS + g] for sw in swapped]
            cols.append(parts[0] if npage == 1 else jnp.concatenate(parts, axis=0))
        lhs = jnp.concatenate(cols, axis=1).astype(BF16)
        p_s[g, pl.ds(pl.multiple_of(i * upb, upb), upb), :] = _dot(lhs, w1_ref[...])

    @pl.when(i == nslab - 1)
    def _():
        nu = nslab * upb
        for g in range(N_KV_HEADS):
            p_s[g, nu:nu + 8, :] = jnp.zeros((8, 2 * HEAD_DIM), F32)
            h = b1_ref[...] + p_s[g, 0:nu, 0:HEAD_DIM] + p_s[g, 1:nu + 1, HEAD_DIM:2 * HEAD_DIM]
            a = jax.nn.gelu(h).astype(BF16)
            if transpose_out:
                o_ref[0, g] = _dot_nt(w2_ref[...], a).astype(BF16)
            else:
                c = _dot(a, w2_ref[...])
                if rope:
                    c = _rope(c, cos_ref[...], sin_ref[...])
                o_ref[0, g] = c.astype(BF16)


def compress_call(src, page_table, w1r, b1, w2, cosf, sinf, rope, nbatch, nslab, slab, transpose_out=False):
    upb = slab // CMP_STRIDE
    nu = nslab * upb
    paged = page_table is not None
    npage = slab // PAGE_SIZE if paged else 1
    const = lambda b, i, *_: (0, 0)
    if paged:
        page_specs = [pl.BlockSpec((1, PAGE_SIZE * N_KV_HEADS, HEAD_DIM),
                                   functools.partial(lambda b, i, pt, r: (pt[b, i * npage + r], 0, 0), r=r))
                      for r in range(npage)]
    else:
        page_specs = [pl.BlockSpec((1, slab * N_KV_HEADS, HEAD_DIM), lambda b, i: (b, i, 0))]
    in_specs = page_specs + [pl.BlockSpec((CMP_STRIDE * HEAD_DIM, 2 * HEAD_DIM), const),
                             pl.BlockSpec((1, HEAD_DIM), const),
                             pl.BlockSpec((HEAD_DIM, HEAD_DIM), const),
                             pl.BlockSpec((nu, HEAD_DIM), const),
                             pl.BlockSpec((nu, HEAD_DIM), const)]
    o_dims = (HEAD_DIM, nu) if transpose_out else (nu, HEAD_DIM)
    out_specs = pl.BlockSpec((1, N_KV_HEADS) + o_dims, lambda b, i, *_: (b, 0, 0, 0))
    kern = functools.partial(_compress_kernel, npage=npage, rope=rope, nslab=nslab, upb=upb, slab=slab,
                             paged=paged, transpose_out=transpose_out)
    scratch = [pltpu.VMEM((N_KV_HEADS, nu + 8, 2 * HEAD_DIM), F32)]
    out_shape = jax.ShapeDtypeStruct((nbatch, N_KV_HEADS) + o_dims, BF16)
    args = [src] * npage + [w1r, b1, w2, cosf, sinf]
    if paged:
        gs = pltpu.PrefetchScalarGridSpec(num_scalar_prefetch=1, grid=(nbatch, nslab), in_specs=in_specs,
                                          out_specs=out_specs, scratch_shapes=scratch)
        return pl.pallas_call(kern, grid_spec=gs, out_shape=out_shape,
                              compiler_params=_cparams(("parallel", "arbitrary")),
                              name="compress_paged")(page_table, *args)
    return pl.pallas_call(kern, grid=(nbatch, nslab), in_specs=in_specs, out_specs=out_specs,
                          out_shape=out_shape, scratch_shapes=scratch,
                          compiler_params=_cparams(("parallel", "arbitrary")),
                          name="compress_rows")(*args)


def cmp_to_slc_matrix(n_cmp, n_slc):
    units = np.arange(n_cmp)[:, None] + np.arange(CMP_PARTS)[None, :]
    sblk = (units * CMP_STRIDE) // SLC_BLOCK
    m = np.zeros((n_cmp, n_slc), np.float32)
    np.add.at(m, (np.repeat(np.arange(n_cmp), CMP_PARTS), sblk.reshape(-1)), 1.0)
    return m


def _split_bf16(x):
    hi = x.astype(BF16)
    return hi, (x - hi.astype(F32)).astype(BF16)


def _softmax_rows(s, mask):
    s = jnp.where(mask, s, NEG_INF)
    m = jnp.max(s, axis=-1, keepdims=True)
    p = jnp.where(mask, jnp.exp2(s - m), 0.0)
    return p / jnp.maximum(jnp.sum(p, axis=-1, keepdims=True), 1e-30)


def _nsa_prompt_kernel(q_ref, kc_ref, vct_ref, sk_ref, svt_ref, wk_ref, wvt_ref, gate_ref, mt_ref, o_ref,
                       *, tq, n_cmp, n_slc):
    g = pl.program_id(1)
    qi = pl.program_id(2)
    t0 = qi * tq
    rows = GQA_REP * tq
    q = q_ref[0].reshape(rows, HEAD_DIM)

    def tile4(a):
        return jnp.concatenate([a] * GQA_REP, axis=1)

    ncp = kc_ref.shape[2]
    n_id = lax.broadcasted_iota(jnp.int32, (ncp, tq), 0)
    tq_pos = t0 + lax.broadcasted_iota(jnp.int32, (ncp, tq), 1)
    cmask = tile4((n_id * CMP_STRIDE + (CMP_BLOCK - 1) <= tq_pos) & (n_id < n_cmp))
    s = jnp.where(cmask, _dot_nt(kc_ref[0, 0], q), NEG_INF)
    m = jnp.max(s, axis=0, keepdims=True)
    p = jnp.where(cmask, jnp.exp2(s - m), 0.0)
    p = p / jnp.maximum(jnp.sum(p, axis=0, keepdims=True), 1e-30)
    o_cmp = _dot(vct_ref[0, 0], p.astype(BF16))
    p_grp = p[:, 0:tq] + p[:, tq:2 * tq] + p[:, 2 * tq:3 * tq] + p[:, 3 * tq:4 * tq]

    p_hi, p_lo = _split_bf16(p_grp)
    nsp = 8 * (-(-n_slc // 8))
    sc =(_dot(mt_ref[...], p_hi) + _dot(mt_ref[...], p_lo))[0:nsp]
    j_id = lax.broadcasted_iota(jnp.int32, (nsp, tq), 0)
    tpos = t0 + lax.broadcasted_iota(jnp.int32, (nsp, tq), 1)
    cur = tpos // SLC_BLOCK
    valid = (j_id * SLC_BLOCK <= tpos) & (j_id < n_slc)
    forced = (j_id == 0) | (j_id == cur) | (j_id == cur - 1)
    sc = jnp.where(valid, jnp.where(forced, FORCED_SCORE, sc), NEG_INF)
    cnt = jnp.zeros((nsp, tq), F32)
    for i in range(n_slc):
        ri = sc[i:i + 1, :]
        beats = (ri > sc) | ((ri == sc) & (j_id > i))
        cnt = cnt + jnp.where(beats, 1.0, 0.0)
    sel_bias = jnp.where((cnt < float(min(N_SELECT, n_slc))) & (sc > 0.5 * NEG_INF), 0.0, NEG_INF)
    if nsp < LANES:
        sel_bias = jnp.concatenate([sel_bias, jnp.zeros((LANES - nsp, tq), F32)], axis=0)
    sel_rows = sel_bias.T.astype(BF16)
    q_slc = jnp.concatenate([q, jnp.concatenate([sel_rows] * GQA_REP, axis=0)], axis=1)

    kp_off = lax.broadcasted_iota(jnp.int32, (tq, tq), 0)
    tq_off = lax.broadcasted_iota(jnp.int32, (tq, tq), 1)

    def slc_bias(k0, diag):
        return jnp.where(kp_off <= tq_off, 0.0, NEG_INF) if diag else None

    def win_bias(k0, diag):
        dist = (t0 + tq_off) - (k0 + kp_off)
        return jnp.where((dist >= 0) & (dist < WINDOW), 0.0, NEG_INF)

    def flash(k_ref, vt_ref, qmat, nblk, bias_fn):
        def update(its, carry, diag):
            m, l, acc = carry
            k0s = [pl.multiple_of((qi - it) * tq, tq) for it in its]
            sbs = []
            for k0 in k0s:
                sb = _dot_nt(k_ref[0, 0, pl.ds(k0, tq), :], qmat)
                bias = bias_fn(k0, diag)
                sbs.append(sb if bias is None else sb + tile4(bias))
            m_new = m
            for sb in sbs:
                m_new = jnp.maximum(m_new, jnp.max(sb, axis=0, keepdims=True))
            alpha = jnp.exp2(m - m_new)
            l = alpha * l
            acc = alpha * acc
            for k0, sb in zip(k0s, sbs):
                pb = jnp.exp2(sb - m_new)
                l = l + jnp.sum(pb, axis=0, keepdims=True)
                acc = acc + _dot(vt_ref[0, 0, :, pl.ds(k0, tq)], pb.astype(BF16))
            return m_new, l, acc

        init = (jnp.full((1, rows), NEG_INF, F32), jnp.zeros((1, rows), F32), jnp.zeros((HEAD_DIM, rows), F32))
        carry = update([0], init, True)
        npair = (nblk - 1) // 2
        carry = lax.fori_loop(0, npair, lambda j, c: update([1 + 2 * j, 2 + 2 * j], c, False), carry)
        carry = lax.cond((nblk - 1) % 2 == 1, lambda c: update([nblk - 1], c, False), lambda c: c, carry)
        _, l, acc = carry
        return acc / jnp.maximum(l, 1e-30)

    o_slc = flash(sk_ref, svt_ref, q_slc, qi + 1, slc_bias)
    o_win = flash(wk_ref, wvt_ref, q, jnp.minimum(qi, WINDOW // tq) + 1, win_bias)

    for r in range(GQA_REP):
        cs = slice(r * tq, (r + 1) * tq)
        c0 = g * GQA_REP + r
        o = (gate_ref[0, pl.ds(c0, 1), :] * o_cmp[:, cs] + gate_ref[0, pl.ds(N_HEADS + c0, 1), :] * o_slc[:, cs]
             + gate_ref[0, pl.ds(2 * N_HEADS + c0, 1), :] * o_win[:, cs])
        o_ref[0, :, r * HEAD_DIM:(r + 1) * HEAD_DIM] = o.T.astype(BF16)


def nsa_prompt_call(q, k_cmp, v_cmp_t, sk, sv_t, wk, wv_t, gates_t, tq):
    B, _, T, _ = q.shape
    ncp = k_cmp.shape[2]
    n_cmp = T // CMP_STRIDE - CMP_PARTS + 1
    n_slc = -(-T // SLC_BLOCK)
    mt = np.zeros((LANES, ncp), np.float32)
    mt[:n_slc, :n_cmp] = cmp_to_slc_matrix(n_cmp, n_slc).T
    mt = jnp.asarray(mt, BF16)
    sk_spec = pl.BlockSpec((1, 1, T, 2 * HEAD_DIM), lambda b, g, i: (b, g, 0, 0))
    k_spec = pl.BlockSpec((1, 1, T, HEAD_DIM), lambda b, g, i: (b, g, 0, 0))
    vt_spec = pl.BlockSpec((1, 1, HEAD_DIM, T), lambda b, g, i: (b, g, 0, 0))
    kern = functools.partial(_nsa_prompt_kernel, tq=tq, n_cmp=n_cmp, n_slc=n_slc)
    return pl.pallas_call(
        kern, grid=(B, N_KV_HEADS, T // tq),
        in_specs=[pl.BlockSpec((1, GQA_REP, tq, HEAD_DIM), lambda b, g, i: (b, g, i, 0)),
                  pl.BlockSpec((1, 1, ncp, HEAD_DIM), lambda b, g, i: (b, g, 0, 0)),
                  pl.BlockSpec((1, 1, HEAD_DIM, ncp), lambda b, g, i: (b, g, 0, 0)),
                  sk_spec, vt_spec, k_spec, vt_spec,
                  pl.BlockSpec((1, LANES, tq), lambda b, g, i: (b, 0, i)),
                  pl.BlockSpec((LANES, ncp), lambda b, g, i: (0, 0))],
        out_specs=pl.BlockSpec((1, tq, GQA_REP * HEAD_DIM), lambda b, g, i: (b, i, g)),
        out_shape=jax.ShapeDtypeStruct((B, T, N_HEADS * HEAD_DIM), BF16),
        compiler_params=_cparams(("parallel", "parallel", "arbitrary")),
        name="nsa_prompt")(q, k_cmp, v_cmp_t, sk, sv_t, wk, wv_t, gates_t, mt)


def _nsa_sample_select_kernel(q_ref, kc_ref, vc_ref, m_ref, ocmp_ref, idx_ref, ok_ref, *, nb, n_cmp, n_slc, qpos):
    rows = GQA_REP * nb
    q = q_ref[0].astype(F32).reshape(rows, HEAD_DIM).astype(BF16)
    ncp = kc_ref.shape[2]
    row_b = lax.broadcasted_iota(jnp.int32, (rows, 1), 0) % nb
    n_id = lax.broadcasted_iota(jnp.int32, (rows, ncp), 1)
    cmask = (n_id * CMP_STRIDE + (CMP_BLOCK - 1) <= qpos) & (n_id < n_cmp)
    o_acc = jnp.zeros((rows, HEAD_DIM), F32)
    pg = jnp.zeros((nb, ncp), F32)
    b_id = lax.broadcasted_iota(jnp.int32, (nb, 1), 0)
    for b in range(nb):
        p = _softmax_rows(_dot_nt(q, kc_ref[b, 0]), cmask)
        o_b = _dot(p.astype(BF16), vc_ref[b, 0])
        o_acc = o_acc + jnp.where(row_b == b, o_b, 0.0)
        p_sum = jnp.sum(p.reshape(GQA_REP, nb, ncp), axis=0)
        pg = pg + jnp.where(b_id == b, p_sum, 0.0)
    ocmp_ref[0] = o_acc.reshape(GQA_REP, nb, HEAD_DIM)

    p_hi, p_lo = _split_bf16(pg)
    sc = _dot(p_hi, m_ref[...]) + _dot(p_lo, m_ref[...])
    nsp = sc.shape[1]
    j_id = lax.broadcasted_iota(jnp.int32, (nb, nsp), 1)
    cur = qpos // SLC_BLOCK
    valid = (j_id * SLC_BLOCK <= qpos) & (j_id < n_slc)
    forced = (j_id == 0) | (j_id == cur) | (j_id == cur - 1)
    sc = jnp.where(valid, jnp.where(forced, FORCED_SCORE, sc), NEG_INF)
    lane = lax.broadcasted_iota(jnp.int32, (nb, LANES), 1)
    idx = jnp.zeros((nb, LANES), jnp.int32)
    okv = jnp.zeros((nb, LANES), jnp.int32)
    for i in range(min(N_SELECT, n_slc)):
        m = jnp.max(sc, axis=-1, keepdims=True)
        first = jnp.min(jnp.where(sc == m, j_id.astype(F32), float(nsp)), axis=-1, keepdims=True).astype(jnp.int32)
        idx = jnp.where(lane == i, first, idx)
        okv = jnp.where(lane == i, jnp.where(m > 0.5 * NEG_INF, 1, 0), okv)
        sc = jnp.where(j_id == first, -3.0e38, sc)
    idx_ref[0] = idx
    ok_ref[0] = okv


def nsa_sample_select_call(q, k_cmp, v_cmp, qpos):
    nb = q.shape[2]
    ncp = k_cmp.shape[2]
    n_cmp = (qpos + 1) // CMP_STRIDE - CMP_PARTS + 1
    n_slc = -(-(qpos + 1) // SLC_BLOCK)
    nsp = LANES * (-(-n_slc // LANES))
    m = np.zeros((ncp, nsp), np.float32)
    m[:n_cmp, :n_slc] = cmp_to_slc_matrix(n_cmp, n_slc)
    m = jnp.asarray(m, BF16)
    cmp_spec = pl.BlockSpec((nb, 1, ncp, HEAD_DIM), lambda g: (0, g, 0, 0))
    kern = functools.partial(_nsa_sample_select_kernel, nb=nb, n_cmp=n_cmp, n_slc=n_slc, qpos=qpos)
    return pl.pallas_call(
        kern, grid=(N_KV_HEADS,),
        in_specs=[pl.BlockSpec((1, GQA_REP, nb, HEAD_DIM), lambda g: (0, g, 0, 0)),
                  cmp_spec, cmp_spec,
                  pl.BlockSpec((ncp, nsp), lambda g: (0, 0))],
        out_specs=[pl.BlockSpec((1, GQA_REP, nb, HEAD_DIM), lambda g: (g, 0, 0, 0)),
                   pl.BlockSpec((1, nb, LANES), lambda g: (g, 0, 0)),
                   pl.BlockSpec((1, nb, LANES), lambda g: (g, 0, 0))],
        out_shape=[jax.ShapeDtypeStruct((N_KV_HEADS, GQA_REP, nb, HEAD_DIM), F32),
                   jax.ShapeDtypeStruct((N_KV_HEADS, nb, LANES), jnp.int32),
                   jax.ShapeDtypeStruct((N_KV_HEADS, nb, LANES), jnp.int32)],
        compiler_params=_cparams(("parallel",)),
        name="nsa_sample_select")(q, k_cmp, v_cmp, m)


def _nsa_sample_attend_kernel(idx_ref, ok_ref, pt_ref, q_ref, *rest, nb, nsel, n_past_blk, qpos):
    kblks, vblks = rest[:nsel], rest[nsel:2 * nsel]
    (wk_ref, wv_ref, skn_ref, svn_ref, wkn_ref, wvn_ref, ocmp_ref, gate_ref, o_ref) = rest[2 * nsel:]
    g = pl.program_id(0)
    b = pl.program_id(1)
    rows = GQA_REP * nb
    qf = q_ref[0].astype(F32).reshape(rows, HEAD_DIM)
    q = qf.astype(BF16)
    row_b = lax.broadcasted_iota(jnp.int32, (rows, 1), 0) % nb

    @pl.when(b == 0)
    def _():
        o_ref[...] = jnp.zeros_like(o_ref)

    head_rows = pl.ds(g, SLC_BLOCK, stride=N_KV_HEADS)
    kcat = jnp.concatenate([kb[0, head_rows, :] for kb in kblks], axis=0).astype(BF16)
    vcat = jnp.concatenate([vb[0, head_rows, :] for vb in vblks], axis=0).astype(BF16)
    nk = nsel * SLC_BLOCK
    slot = lax.broadcasted_iota(jnp.int32, (1, nk), 1) // SLC_BLOCK
    within = lax.broadcasted_iota(jnp.int32, (1, nk), 1) % SLC_BLOCK
    kpos = jnp.zeros((1, nk), jnp.int32)
    kok = jnp.zeros((1, nk), jnp.int32)
    new_ok = jnp.zeros((1, 1), jnp.int32)
    base = (g * nb + b) * LANES
    for s in range(nsel):
        bi = idx_ref[base + s]
        oks = ok_ref[base + s]
        kpos = jnp.where(slot == s, bi * SLC_BLOCK + within, kpos)
        kok = jnp.where(slot == s, jnp.where(bi < n_past_blk, oks, 0), kok)
        new_ok = jnp.maximum(new_ok, jnp.where(bi * SLC_BLOCK <= qpos, jnp.where(bi >= n_past_blk, oks, 0), 0))
    mask = (kok > 0) & (kpos <= qpos)
    s_past = jnp.where(mask, _dot_nt(q, kcat), NEG_INF)
    has_new = new_ok > 0
    s_new = jnp.where(has_new, jnp.sum(qf * skn_ref[0], axis=-1, keepdims=True), NEG_INF)
    m = jnp.maximum(jnp.max(s_past, axis=-1, keepdims=True), s_new)
    p_past = jnp.where(mask, jnp.exp2(s_past - m), 0.0)
    p_new = jnp.where(has_new, jnp.exp2(s_new - m), 0.0)
    den = jnp.maximum(jnp.sum(p_past, axis=-1, keepdims=True) + p_new, 1e-30)
    o_slc = (_dot(p_past.astype(BF16), vcat)
             + p_new * svn_ref[0]) / den

    nw = wk_ref.shape[1] // N_KV_HEADS
    win_rows = pl.ds(g, nw, stride=N_KV_HEADS)
    wpos = qpos - nw + lax.broadcasted_iota(jnp.int32, (1, nw), 1)
    wdist = qpos - wpos
    wmask = (wpos >= 0) & (wdist >= 0) & (wdist < WINDOW)
    s_w = jnp.where(wmask, _dot_nt(q, wk_ref[0, win_rows, :].astype(BF16)), NEG_INF)
    s_wn = jnp.sum(qf * wkn_ref[0], axis=-1, keepdims=True)
    mw = jnp.maximum(jnp.max(s_w, axis=-1, keepdims=True), s_wn)
    p_w = jnp.where(wmask, jnp.exp2(s_w - mw), 0.0)
    p_wn = jnp.exp2(s_wn - mw)
    den_w = jnp.maximum(jnp.sum(p_w, axis=-1, keepdims=True) + p_wn, 1e-30)
    o_win = (_dot(p_w.astype(BF16), wv_ref[0, win_rows, :].astype(BF16))
             + p_wn * wvn_ref[0]) / den_w

    gates = jnp.concatenate([gate_ref[...]] * GQA_REP, axis=0)
    lane = lax.broadcasted_iota(jnp.int32, (rows, LANES), 1)
    row_r = lax.broadcasted_iota(jnp.int32, (rows, 1), 0) // nb

    def gate_col(branch):
        c = branch * N_HEADS + g * GQA_REP + row_r
        return jnp.sum(jnp.where(lane == c, gates, 0.0), axis=-1, keepdims=True)

    o = (gate_col(0) * ocmp_ref[0].reshape(rows, HEAD_DIM) + gate_col(1) * o_slc + gate_col(2) * o_win)
    o_ref[0] += jnp.where(row_b == b, o, 0.0).reshape(GQA_REP, nb, HEAD_DIM)


def nsa_sample_attend_call(idx, ok, page_table, q, pool_k, pool_v, win_k, win_v, sk_new, sv_new, wk_new, wv_new,
                           o_cmp, gates, qpos):
    nb = q.shape[2]
    nsel = min(N_SELECT, -(-(qpos + 1) // SLC_BLOCK))
    n_past_blk = page_table.shape[1] * PAGE_SIZE // SLC_BLOCK
    half = PAGE_SIZE // SLC_BLOCK

    def blk_map(g, b, idx_r, ok_r, pt_r, s):
        bi = jnp.minimum(idx_r[(g * nb + b) * LANES + s], n_past_blk - 1)
        return (pt_r[b, bi // half], bi % half, 0)

    blk_specs = [pl.BlockSpec((1, SLC_BLOCK * N_KV_HEADS, HEAD_DIM), functools.partial(blk_map, s=s))
                 for s in range(nsel)]
    win_spec = pl.BlockSpec((1, win_k.shape[1], HEAD_DIM), lambda g, b, *_: (b, 0, 0))
    new_spec = pl.BlockSpec((1, 1, HEAD_DIM), lambda g, b, *_: (b * N_KV_HEADS + g, 0, 0))
    qo_spec = pl.BlockSpec((1, GQA_REP, nb, HEAD_DIM), lambda g, b, *_: (0, g, 0, 0))
    in_specs = ([qo_spec] + blk_specs + blk_specs + [win_spec, win_spec, new_spec, new_spec, new_spec, new_spec,
                pl.BlockSpec((1, GQA_REP, nb, HEAD_DIM), lambda g, b, *_: (g, 0, 0, 0)),
                pl.BlockSpec((nb, LANES), lambda g, b, *_: (0, 0))])
    gs = pltpu.PrefetchScalarGridSpec(
        num_scalar_prefetch=3, grid=(N_KV_HEADS, nb), in_specs=in_specs,
        out_specs=pl.BlockSpec((1, GQA_REP, nb, HEAD_DIM), lambda g, b, *_: (g, 0, 0, 0)))
    kern = functools.partial(_nsa_sample_attend_kernel, nb=nb, nsel=nsel, n_past_blk=n_past_blk, qpos=qpos)
    return pl.pallas_call(
        kern, grid_spec=gs,
        out_shape=jax.ShapeDtypeStruct((N_KV_HEADS, GQA_REP, nb, HEAD_DIM), F32),
        compiler_params=_cparams(("parallel", "arbitrary")),
        name="nsa_sample_attend")(idx.reshape(-1), ok.reshape(-1), page_table, q,
                                  *([pool_k] * nsel), *([pool_v] * nsel), win_k, win_v,
                                  sk_new, sv_new, wk_new, wv_new, o_cmp, gates)


def _rope_tables(pos):
    half = HEAD_DIM // 2
    inv_freq = ROPE_THETA ** (-jnp.arange(half, dtype=F32) / half)
    ang = pos.astype(F32)[:, None] * inv_freq[None, :]
    cos, sin = jnp.cos(ang), jnp.sin(ang)
    return jnp.concatenate([cos, cos], axis=-1), jnp.concatenate([-sin, sin], axis=-1)


def _trunk(x2, p3, pos, h0r, h0i, w, lay, kv_ctx):
    nb, T, tc, C, R, tm = lay
    depth = w["norm_mix_g"].shape[0]
    n_a = w["ssm_a_re"].shape[0]
    d = x2.shape[1]
    cosf, sinf = _rope_tables(pos)
    ssm_re, ssm_im = [], []
    kv = None
    for i in range(depth):
        if i < n_a:
            prm = s5_params(w["ssm_a_re"][i], w["ssm_a_im"][i], w["ssm_log_dt"][i], w["ssm_b_re"][i],
                            w["ssm_b_im"][i], w["ssm_c_re"][i], w["ssm_c_im"][i], w["ssm_d"][i], nb)
            g3, hr, hi = s5_mixer(x2.reshape(nb, T, d), w["norm_mix_g"][i][None], prm, h0r[i], h0i[i], tc)
            ssm_re.append(rows_to_state(hr, nb))
            ssm_im.append(rows_to_state(hi, nb))
            x2 = glu_call(g3.reshape(nb * T, d), w["ssm_w_glu"], w["ssm_b_glu"], x2, i, tm)
        else:
            j = i - n_a
            q, gates_t = q_call(x2, w["norm_mix_g3"], w["nsa_w_q"], w["nsa_wgt_t"][j], w["nsa_bgt"][j],
                                cosf, sinf, i, j, C, R, tm)
            o2 = kv_ctx["attend"](q, gates_t, kv)
            x2 = oproj_call(o2, w["nsa_w_o"], x2, j, tm)
        x2 = ffn_call(x2, w["norm_ffn_g3"], w["ffn_w_gate"], w["ffn_w_up"], w["ffn_w_down"], i, tm, 512)
        x2 = ple_call(x2, w["norm_ple_g3"], p3, w["ple_w_gate"], w["ple_w_proj"], w["norm_final_g"][None], i, tm,
                      final_norm=(i == depth - 1))
        if i == n_a - 1:
            kv = kv_ctx["build"](x2, cosf, sinf)
    return x2, jnp.stack(ssm_re), jnp.stack(ssm_im), kv["new"]


def _cmp_weights(w1, b1, w2):
    w1p = w1.reshape(CMP_PARTS, CMP_STRIDE * HEAD_DIM, HEAD_DIM)
    return jnp.concatenate([w1p[0], w1p[1]], axis=1).astype(BF16), b1[None], w2.astype(BF16)


def kernel(x_prompt, x_sample, state_ssm_re, state_ssm_im, cache_cmp_k, cache_cmp_v, cache_slc_k, cache_slc_v, state_win_k, state_win_v, page_table, p_prompt, p_sample, norm_mix_g, norm_ffn_g, norm_ple_g, norm_kv_g, norm_final_g, ssm_a_re, ssm_a_im, ssm_log_dt, ssm_b_re, ssm_b_im, ssm_c_re, ssm_c_im, ssm_d, ssm_w_glu, ssm_b_glu, kv_w, cmp_k_w1, cmp_k_b1, cmp_k_w2, cmp_v_w1, cmp_v_b1, cmp_v_w2, nsa_w_q, nsa_w_gate, nsa_b_gate, nsa_w_o, ffn_w_gate, ffn_w_up, ffn_w_down, ple_w_gate, ple_w_proj):
    B, T, D = x_prompt.shape
    nbs, ts, _ = x_sample.shape
    assert ts == 1
    n_a = ssm_a_re.shape[0]
    past = page_table.shape[1] * PAGE_SIZE
    gate_pad = LANES - nsa_w_gate.shape[-1]
    w = dict(norm_mix_g=norm_mix_g, norm_mix_g3=norm_mix_g[:, None], norm_ffn_g3=norm_ffn_g[:, None],
             norm_ple_g3=norm_ple_g[:, None], norm_final_g=norm_final_g,
             ssm_a_re=ssm_a_re, ssm_a_im=ssm_a_im, ssm_log_dt=ssm_log_dt, ssm_b_re=ssm_b_re, ssm_b_im=ssm_b_im,
             ssm_c_re=ssm_c_re, ssm_c_im=ssm_c_im, ssm_d=ssm_d,
             ssm_w_glu=ssm_w_glu.astype(BF16), ssm_b_glu=ssm_b_glu[:, None],
             nsa_w_q=nsa_w_q.astype(BF16),
             nsa_wgt_t=jnp.pad(jnp.swapaxes(nsa_w_gate, 1, 2), ((0, 0), (0, gate_pad), (0, 0))).astype(BF16),
             nsa_bgt=jnp.pad(nsa_b_gate, ((0, 0), (0, gate_pad)))[..., None],
             nsa_w_o=nsa_w_o.astype(BF16),
             ffn_w_gate=ffn_w_gate.astype(BF16), ffn_w_up=ffn_w_up.astype(BF16), ffn_w_down=ffn_w_down.astype(BF16),
             ple_w_gate=ple_w_gate.astype(BF16), ple_w_proj=ple_w_proj.astype(BF16))
    kvw = kv_w.astype(BF16)
    kvg = norm_kv_g[None]
    ck_w = _cmp_weights(cmp_k_w1, cmp_k_b1, cmp_k_w2)
    cv_w = _cmp_weights(cmp_v_w1, cmp_v_b1, cmp_v_w2)

    tm_p = min(512, T)
    tq = min(256, T)
    pos_p = jnp.arange(T, dtype=jnp.int32)

    def build_prompt(x2, cosf, sinf):
        ck, cv, sk, sv, wk, wv, skh, wkh, svt, wvt = kv_call(x2, kvg, kvw, cosf, sinf, B, T, tm_p, True)
        slab = min(SLAB_ROWS, T)
        nslab = T // slab
        nu = T // CMP_STRIDE
        ccos, csin = _rope_tables(jnp.arange(nu, dtype=jnp.int32) * CMP_STRIDE + (CMP_BLOCK - 1))
        rows3 = lambda a: a.reshape(B, T * N_KV_HEADS, HEAD_DIM)
        k_cmp = compress_call(rows3(ck), None, *ck_w, ccos, csin, True, B, nslab, slab)
        v_cmp_t = compress_call(rows3(cv), None, cv_w[0], cv_w[1], cv_w[2].T, ccos, csin, False, B, nslab, slab,
                                transpose_out=True)
        r4 = lambda a: a.reshape(B, T, N_KV_HEADS, HEAD_DIM)
        ck, cv, sk, sv, wk, wv = [r4(a) for a in (ck, cv, sk, sv, wk, wv)]
        wb = min(WINDOW, T)
        new = (ck, cv, sk, sv, wk[:, T - wb:], wv[:, T - wb:])
        return dict(k_cmp=k_cmp, v_cmp_t=v_cmp_t, sk=skh, sv_t=svt, wk=wkh, wv_t=wvt, new=new)

    def attend_prompt(q, gates_t, kv):
        o = nsa_prompt_call(q, kv["k_cmp"], kv["v_cmp_t"], kv["sk"], kv["sv_t"], kv["wk"], kv["wv_t"], gates_t, tq)
        return o.reshape(B * T, N_HEADS * HEAD_DIM)

    h0 = jnp.zeros((n_a, D // PAIR_CH, 2 * B, PAIR_ST), F32)
    y_p, sre_p, sim_p, new_p = _trunk(x_prompt.reshape(B * T, D), p_prompt.reshape(-1, B * T, p_prompt.shape[-1]),
                                      pos_p, h0, h0, w, (B, T, min(128, T), B, T, tm_p),
                                      dict(build=build_prompt, attend=attend_prompt))
    y_p = y_p.reshape(B, T, D)

    pool3 = lambda a: a.reshape(a.shape[0], -1, HEAD_DIM)
    qpos = past
    pos_s = jnp.full((nbs,), past, jnp.int32)

    def build_sample(x2, cosf, sinf):
        ck, cv, sk, sv, wk, wv = kv_call(x2, kvg, kvw, cosf, sinf, 1, nbs, nbs, False)
        slab = min(SLAB_ROWS, past)
        nslab = past // slab
        nu = past // CMP_STRIDE
        ccos, csin = _rope_tables(jnp.arange(nu, dtype=jnp.int32) * CMP_STRIDE + (CMP_BLOCK - 1))
        k_cmp = compress_call(pool3(cache_cmp_k), page_table, *ck_w, ccos, csin, True, nbs, nslab, slab)
        v_cmp = compress_call(pool3(cache_cmp_v), page_table, *cv_w, ccos, csin, False, nbs, nslab, slab)
        r4 = lambda a: a.reshape(nbs, 1, N_KV_HEADS, HEAD_DIM)
        win_k = jnp.concatenate([state_win_k[:, 1:], r4(wk)], axis=1)
        win_v = jnp.concatenate([state_win_v[:, 1:], r4(wv)], axis=1)
        new = (r4(ck), r4(cv), r4(sk), r4(sv), win_k, win_v)
        return dict(k_cmp=k_cmp, v_cmp=v_cmp, sk=sk, sv=sv, wk=wk, wv=wv, new=new)

    def attend_sample(q, gates_t, kv):
        o_cmp, idx, ok = nsa_sample_select_call(q, kv["k_cmp"], kv["v_cmp"], qpos)
        rows1 = lambda a: a.reshape(nbs * N_KV_HEADS, 1, HEAD_DIM)
        o = nsa_sample_attend_call(idx, ok, page_table, q, pool3(cache_slc_k), pool3(cache_slc_v),
                                   pool3(state_win_k), pool3(state_win_v),
                                   rows1(kv["sk"]), rows1(kv["sv"]), rows1(kv["wk"]), rows1(kv["wv"]),
                                   o_cmp, gates_t[0].T, qpos)
        return jnp.transpose(o, (2, 0, 1, 3)).reshape(nbs, N_HEADS * HEAD_DIM).astype(BF16)

    h0r = jnp.stack([state_to_rows(state_ssm_re[i], nbs) for i in range(n_a)])
    h0i = jnp.stack([state_to_rows(state_ssm_im[i], nbs) for i in range(n_a)])
    y_s, sre_s, sim_s, new_s = _trunk(x_sample.reshape(nbs, D), p_sample.reshape(-1, nbs, p_sample.shape[-1]),
                                      pos_s, h0r, h0i, w, (nbs, 1, 1, 1, nbs, nbs),
                                      dict(build=build_sample, attend=attend_sample))
    y_s = y_s.reshape(nbs, 1, D)

    ck_p, cv_p, sk_p, sv_p, wk_p, wv_p = new_p
    ck_s, cv_s, sk_s, sv_s, wk_s, wv_s = new_s
    return (y_p, y_s, sre_p, sim_p, ck_p, cv_p, sk_p, sv_p, wk_p, wv_p,
            sre_s, sim_s, ck_s, cv_s, sk_s, sv_s, wk_s, wv_s)
```

```python
import functools

import numpy as np
import jax
import jax.numpy as jnp
from jax import lax
from jax.experimental import pallas as pl
from jax.experimental.pallas import tpu as pltpu

F32 = jnp.float32
BF16 = jnp.bfloat16

N_HEADS = 16
N_KV_HEADS = 4
GQA_REP = N_HEADS // N_KV_HEADS
HEAD_DIM = 128
SSM_GROUP = 16
SSM_STATE = 64
CMP_BLOCK = 32
CMP_STRIDE = 16
CMP_PARTS = CMP_BLOCK // CMP_STRIDE
SLC_BLOCK = 64
N_SELECT = 16
WINDOW = 512
PAGE_SIZE = 128
ROPE_THETA = 10000.0
ATTN_SCALE = HEAD_DIM ** -0.5
EXP2_SCALE = ATTN_SCALE * float(np.log2(np.e))
NORM_EPS = 1e-6
NEG_INF = -1e30
FORCED_SCORE = 1e9

LANES = 128
KV_LANES = N_KV_HEADS * HEAD_DIM
PAIR_CH = 2 * LANES
PAIR_ST = (LANES // SSM_GROUP) * SSM_STATE
SLAB_ROWS = 2048
VMEM_LIMIT = 56 * 1024 * 1024


def _cparams(sem, vmem=VMEM_LIMIT):
    return pltpu.CompilerParams(dimension_semantics=sem, vmem_limit_bytes=vmem)


def _rms(x, g):
    ms = jnp.mean(x * x, axis=-1, keepdims=True)
    return x * lax.rsqrt(ms + NORM_EPS) * g


def _rope(x, cosf, sinf):
    return x * cosf + pltpu.roll(x, HEAD_DIM // 2, axis=1) * sinf


def _dot(a, b):
    return jnp.dot(a, b, preferred_element_type=F32)


def _dot_nt(a, b):
    return lax.dot_general(a, b, (((1,), (1,)), ((), ())), preferred_element_type=F32)


def _s5_kernel(x_ref, gn_ref, wre_ref, wim_ref, cre_ref, cim_ref, ar_ref, ai_ref, d_ref, h0r_ref, h0i_ref,
               g_ref, hr_ref, hi_ref, hn_s, xr_s, xi_s, sr_s, si_s, *, tc, nb, npair):
    @pl.when(pl.program_id(0) == 0)
    def _():
        hr_ref[...] = h0r_ref[...]
        hi_ref[...] = h0i_ref[...]

    r2 = 2 * nb
    rows = tc * r2
    for b in range(nb):
        hn_s[b] = _rms(x_ref[b], gn_ref[...])
    keep = ((lax.broadcasted_iota(jnp.int32, (r2, PAIR_CH), 0) // nb == 1)
            == (lax.broadcasted_iota(jnp.int32, (r2, PAIR_CH), 1) >= LANES))
    lane_hi = lax.broadcasted_iota(jnp.int32, (tc, PAIR_CH), 1) >= LANES

    for k in range(npair):
        cols = slice(k * PAIR_CH, (k + 1) * PAIR_CH)

        if tc > 1:
            hd = jnp.concatenate([hn_s[:, :, cols], hn_s[:, :, cols]], axis=0)
            lhs = jnp.where(keep[None], jnp.swapaxes(hd, 0, 1), 0.0).reshape(rows, PAIR_CH).astype(BF16)
        else:
            v = hn_s[:, 0, cols]
            lhs = jnp.where(keep, jnp.concatenate([v, v], axis=0), 0.0).astype(BF16)
        xr_s[...] = _dot(lhs, wre_ref[k]).reshape(tc, r2, PAIR_ST)
        xi_s[...] = _dot(lhs, wim_ref[k]).reshape(tc, r2, PAIR_ST)
        ar = ar_ref[k]
        ai = ai_ref[k]

        def step(t, carry):
            hr, hi = carry
            nr = ar * hr - ai * hi + xr_s[t]
            ni = ar * hi + ai * hr + xi_s[t]
            sr_s[t] = nr
            si_s[t] = ni
            return nr, ni

        hr, hi = lax.fori_loop(0, tc, step, (hr_ref[k], hi_ref[k]), unroll=min(8, tc))
        hr_ref[k] = hr
        hi_ref[k] = hi
        h_re = sr_s[...].reshape(rows, PAIR_ST).astype(BF16)
        h_im = si_s[...].reshape(rows, PAIR_ST).astype(BF16)
        y = (_dot(h_re, cre_ref[k]) - _dot(h_im, cim_ref[k])).reshape(tc, r2, PAIR_CH)
        if tc > 1:
            y = jnp.swapaxes(y, 0, 1)
            per_row = lambda r: y[r]
        else:
            per_row = lambda r: y[:, r, :]
        for b in range(nb):
            yb = jnp.where(lane_hi, per_row(nb + b), per_row(b))
            yb = yb + d_ref[:, cols] * hn_s[b, :, cols]
            g_ref[b, :, cols] = jax.nn.gelu(yb).astype(BF16)


def s5_mixer(x3, gn, prm, h0r, h0i, tc):
    nb, T, d = x3.shape
    r2 = 2 * nb
    npair = d // PAIR_CH
    full3 = lambda s: (0, 0, 0)
    st_shape = (npair, r2, PAIR_ST)
    kern = functools.partial(_s5_kernel, tc=tc, nb=nb, npair=npair)
    return pl.pallas_call(
        kern, grid=(T // tc,),
        in_specs=[pl.BlockSpec((nb, tc, d), lambda s: (0, s, 0)),
                  pl.BlockSpec((1, d), lambda s: (0, 0)),
                  pl.BlockSpec((npair, PAIR_CH, PAIR_ST), full3),
                  pl.BlockSpec((npair, PAIR_CH, PAIR_ST), full3),
                  pl.BlockSpec((npair, PAIR_ST, PAIR_CH), full3),
                  pl.BlockSpec((npair, PAIR_ST, PAIR_CH), full3),
                  pl.BlockSpec(st_shape, full3),
                  pl.BlockSpec(st_shape, full3),
                  pl.BlockSpec((1, d), lambda s: (0, 0)),
                  pl.BlockSpec(st_shape, full3),
                  pl.BlockSpec(st_shape, full3)],
        out_specs=[pl.BlockSpec((nb, tc, d), lambda s: (0, s, 0)),
                   pl.BlockSpec(st_shape, full3),
                   pl.BlockSpec(st_shape, full3)],
        out_shape=[jax.ShapeDtypeStruct((nb, T, d), BF16),
                   jax.ShapeDtypeStruct(st_shape, F32),
                   jax.ShapeDtypeStruct(st_shape, F32)],
        scratch_shapes=[pltpu.VMEM((nb, tc, d), F32)] + [pltpu.VMEM((tc, r2, PAIR_ST), F32)] * 4,
        compiler_params=_cparams(("arbitrary",)),
        name="s5_mixer")(x3, gn, prm["wre"], prm["wim"], prm["cre"], prm["cim"],
                         prm["ar"], prm["ai"], prm["d"], h0r, h0i)


def s5_params(a_re, a_im, log_dt, b_re, b_im, c_re, c_im, d, nb):
    G, P, H = b_re.shape
    gpc = LANES // H
    nchunk = G // gpc
    dt = jnp.exp(log_dt)[:, None]
    mag = jnp.exp(dt * a_re)
    ab_re, ab_im = mag * jnp.cos(dt * a_im), mag * jnp.sin(dt * a_im)
    den = a_re * a_re + a_im * a_im
    nr, ni = ab_re - 1.0, ab_im
    cf_re = (nr * a_re + ni * a_im) / den
    cf_im = (ni * a_re - nr * a_im) / den
    bb_re = cf_re[..., None] * b_re - cf_im[..., None] * b_im
    bb_im = cf_re[..., None] * b_im + cf_im[..., None] * b_re
    eye = jnp.eye(gpc, dtype=F32)

    def blk_in(bb):
        w = jnp.einsum("cgph,gk->cghkp", bb.reshape(nchunk, gpc, P, H), eye)
        return w.reshape(nchunk // 2, PAIR_CH, gpc * P).astype(BF16)

    def blk_out(cc):
        w = jnp.einsum("cghp,gk->ckpgh", cc.reshape(nchunk, gpc, H, P), eye)
        w = w.reshape(nchunk // 2, 2, gpc * P, LANES)
        return jnp.swapaxes(w, 1, 2).reshape(nchunk // 2, gpc * P, PAIR_CH).astype(BF16)

    def rows(a):
        a = a.reshape(nchunk // 2, 2, 1, gpc * P)
        return jnp.broadcast_to(a, (nchunk // 2, 2, nb, gpc * P)).reshape(nchunk // 2, 2 * nb, gpc * P)

    return dict(wre=blk_in(bb_re), wim=blk_in(bb_im), cre=blk_out(c_re), cim=blk_out(c_im),
                ar=rows(ab_re), ai=rows(ab_im), d=d[None])


def state_to_rows(h, nb):
    npair = h.shape[1] * h.shape[2] // PAIR_ST // 2
    return jnp.transpose(h.reshape(nb, npair, 2, PAIR_ST), (1, 2, 0, 3)).reshape(npair, 2 * nb, PAIR_ST)


def rows_to_state(h, nb):
    npair = h.shape[0]
    h = jnp.transpose(h.reshape(npair, 2, nb, PAIR_ST), (2, 0, 1, 3))
    return h.reshape(nb, npair * 2 * PAIR_ST // SSM_STATE, SSM_STATE)


COL_CHUNK = 512


def _resident(block_shape, index_map):
    return pl.BlockSpec(block_shape, index_map, pipeline_mode=pl.Buffered(1))


def _glu_kernel(a_ref, w_ref, b_ref, x_ref, o_ref):
    a = a_ref[...]
    d = x_ref.shape[1]
    for c in range(0, d, COL_CHUNK):
        z1 = _dot(a, w_ref[:, c:c + COL_CHUNK]) + b_ref[:, c:c + COL_CHUNK]
        z2 = _dot(a, w_ref[:, d + c:d + c + COL_CHUNK]) + b_ref[:, d + c:d + c + COL_CHUNK]
        o_ref[:, c:c + COL_CHUNK] = x_ref[:, c:c + COL_CHUNK] + z1 * jax.nn.sigmoid(z2)


def glu_call(g2, w_glu, b_glu, x2, li, tm):
    m, d = x2.shape
    return pl.pallas_call(
        _glu_kernel, grid=(m // tm,),
        in_specs=[pl.BlockSpec((tm, d), lambda i: (i, 0)),
                  _resident((None, d, 2 * d), lambda i: (li, 0, 0)),
                  _resident((None, 1, 2 * d), lambda i: (li, 0, 0)),
                  pl.BlockSpec((tm, d), lambda i: (i, 0))],
        out_specs=pl.BlockSpec((tm, d), lambda i: (i, 0)),
        out_shape=jax.ShapeDtypeStruct(x2.shape, F32),
        compiler_params=_cparams(("parallel",)),
        name="s5_glu")(g2, w_glu, b_glu, x2)


def _ffn_kernel(x_ref, g_ref, wg_ref, wu_ref, wd_ref, o_ref, *rest, nf, emit_bf16):
    hn_s, acc_s = rest[-2:]
    f = pl.program_id(1)

    @pl.when(f == 0)
    def _():
        hn_s[...] = _rms(x_ref[...], g_ref[...]).astype(BF16)
        acc_s[...] = jnp.zeros_like(acc_s)

    wg, wu, wd = wg_ref[...].astype(BF16), wu_ref[...].astype(BF16), wd_ref[...].astype(BF16)
    if emit_bf16:
        for ref, val in zip(rest[:3], (wg, wu, wd)):
            ref[...] = val
    h = hn_s[...]
    a = _dot(h, wg)
    u = _dot(h, wu)
    act = (a * jax.nn.sigmoid(a) * u).astype(BF16)
    acc_s[...] += _dot(act, wd)

    @pl.when(f == nf - 1)
    def _():
        o_ref[...] = x_ref[...] + acc_s[...]


def ffn_call(x2, g, wg, wu, wd, li, tm, tf, emit_bf16=False):
    m, d = x2.shape
    ff = wg.shape[-1]
    nf = ff // tf
    if emit_bf16:
        assert m == tm
        w_specs = [pl.BlockSpec((None, d, tf), lambda i, f: (li, 0, f)),
                   pl.BlockSpec((None, d, tf), lambda i, f: (li, 0, f)),
                   pl.BlockSpec((None, tf, d), lambda i, f: (li, f, 0))]
    else:
        w_specs = [pl.BlockSpec((d, tf), lambda i, f: (0, f)),
                   pl.BlockSpec((d, tf), lambda i, f: (0, f)),
                   pl.BlockSpec((tf, d), lambda i, f: (f, 0))]
    out_specs = [pl.BlockSpec((tm, d), lambda i, f: (i, 0))]
    out_shape = [jax.ShapeDtypeStruct(x2.shape, F32)]
    if emit_bf16:
        out_specs += [pl.BlockSpec((d, tf), lambda i, f: (0, f)), pl.BlockSpec((d, tf), lambda i, f: (0, f)),
                      pl.BlockSpec((tf, d), lambda i, f: (f, 0))]
        out_shape += [jax.ShapeDtypeStruct((d, ff), BF16)] * 2 + [jax.ShapeDtypeStruct((ff, d), BF16)]
    return pl.pallas_call(
        functools.partial(_ffn_kernel, nf=nf, emit_bf16=emit_bf16), grid=(m // tm, nf),
        in_specs=[pl.BlockSpec((tm, d), lambda i, f: (i, 0)),
                  pl.BlockSpec((None, 1, d), lambda i, f: (li, 0, 0))] + w_specs,
        out_specs=out_specs, out_shape=out_shape,
        scratch_shapes=[pltpu.VMEM((tm, d), BF16), pltpu.VMEM((tm, d), F32)],
        compiler_params=_cparams(("parallel", "arbitrary")),
        name="ffn")(x2, g, wg, wu, wd)


def _ple_kernel(x_ref, g_ref, p_ref, wg_ref, wp_ref, gf_ref, o_ref, *, final_norm):
    hn = _rms(x_ref[...], g_ref[...]).astype(BF16)
    pe = p_ref[...].astype(BF16)
    for c in range(0, x_ref.shape[1], COL_CHUNK):
        cs = slice(c, c + COL_CHUNK)
        gate = jax.nn.sigmoid(_dot(hn, wg_ref[:, cs]))
        o_ref[:, cs] = x_ref[:, cs] + gate * _dot(pe, wp_ref[:, cs])
    if final_norm:
        o_ref[...] = _rms(o_ref[...], gf_ref[...])


def ple_call(x2, g, p3, wg, wp, gf, li, tm, final_norm):
    m, d = x2.shape
    pd = wp.shape[1]
    return pl.pallas_call(
        functools.partial(_ple_kernel, final_norm=final_norm), grid=(m // tm,),
        in_specs=[pl.BlockSpec((tm, d), lambda i: (i, 0)),
                  _resident((None, 1, d), lambda i: (li, 0, 0)),
                  pl.BlockSpec((None, tm, pd), lambda i: (li, i, 0)),
                  _resident((None, d, d), lambda i: (li, 0, 0)),
                  _resident((None, pd, d), lambda i: (li, 0, 0)),
                  _resident((1, d), lambda i: (0, 0))],
        out_specs=pl.BlockSpec((tm, d), lambda i: (i, 0)),
        out_shape=jax.ShapeDtypeStruct(x2.shape, F32),
        compiler_params=_cparams(("parallel",)),
        name="ple")(x2, g, p3, wg, wp, gf)


def _kv_kernel(x_ref, g_ref, w_ref, cos_ref, sin_ref, *outs, heads_out, nt):
    hn = _rms(x_ref[...], g_ref[...]).astype(BF16)
    tm = x_ref.shape[0]
    if heads_out:
        t_row = (pl.program_id(0) % nt) * tm + lax.broadcasted_iota(jnp.int32, (tm, LANES), 0)
        blk_onehot = jnp.where(lax.broadcasted_iota(jnp.int32, (tm, LANES), 1) == t_row // SLC_BLOCK, 1.0, 0.0)
        for h in range(N_KV_HEADS):
            outs[6][0, h, :, HEAD_DIM:] = blk_onehot.astype(BF16)
    for k in range(6):
        z = _dot(hn, w_ref[:, k * KV_LANES:(k + 1) * KV_LANES])
        for h in range(N_KV_HEADS):
            zh = z[:, h * HEAD_DIM:(h + 1) * HEAD_DIM]
            if k in (2, 4):
                zh = _rope(zh, cos_ref[...], sin_ref[...])
            outs[k][pl.ds(h, tm, stride=N_KV_HEADS), :] = zh
            if heads_out and k == 2:
                outs[6][0, h, :, :HEAD_DIM] = zh.astype(BF16)
            if heads_out and k == 4:
                outs[7][0, h] = zh.astype(BF16)
            if heads_out and k in (3, 5):
                outs[8 + (k == 5)][0, h] = zh.T.astype(BF16)


def kv_call(x2, g, kv_w, cosf, sinf, C, R, tm, heads_out):
    m, d = x2.shape
    nt = R // tm
    out_shape = [jax.ShapeDtypeStruct((m * N_KV_HEADS, HEAD_DIM), F32)] * 6
    out_specs = [pl.BlockSpec((tm * N_KV_HEADS, HEAD_DIM), lambda i: (i, 0))] * 6
    if heads_out:
        assert -(-R // SLC_BLOCK) <= LANES
        for width in (2 * HEAD_DIM, HEAD_DIM):
            out_shape += [jax.ShapeDtypeStruct((C, N_KV_HEADS, R, width), BF16)]
            out_specs += [pl.BlockSpec((1, N_KV_HEADS, tm, width), lambda i: (i // nt, 0, i % nt, 0))]
        out_shape += [jax.ShapeDtypeStruct((C, N_KV_HEADS, HEAD_DIM, R), BF16)] * 2
        out_specs += [pl.BlockSpec((1, N_KV_HEADS, HEAD_DIM, tm), lambda i: (i // nt, 0, 0, i % nt))] * 2
    return pl.pallas_call(
        functools.partial(_kv_kernel, heads_out=heads_out, nt=nt), grid=(m // tm,),
        in_specs=[pl.BlockSpec((tm, d), lambda i: (i, 0)),
                  _resident((1, d), lambda i: (0, 0)),
                  _resident((d, 6 * KV_LANES), lambda i: (0, 0)),
                  pl.BlockSpec((tm, HEAD_DIM), lambda i: (i % nt, 0)),
                  pl.BlockSpec((tm, HEAD_DIM), lambda i: (i % nt, 0))],
        out_specs=out_specs, out_shape=out_shape,
        compiler_params=_cparams(("parallel",)),
        name="kv_proj")(x2, g, kv_w, cosf, sinf)


def _q_kernel(x_ref, g_ref, w_ref, wgt_ref, bgt_ref, cos_ref, sin_ref, q_ref, gate_ref):
    hn = _rms(x_ref[...], g_ref[...]).astype(BF16)
    gate_ref[0] = jax.nn.sigmoid(_dot_nt(wgt_ref[...], hn) + bgt_ref[...])
    hpc = COL_CHUNK // HEAD_DIM
    for c in range(N_HEADS // hpc):
        z = _dot(hn, w_ref[:, c * COL_CHUNK:(c + 1) * COL_CHUNK])
        for h in range(hpc):
            zh = _rope(z[:, h * HEAD_DIM:(h + 1) * HEAD_DIM], cos_ref[...], sin_ref[...])
            q_ref[0, c * hpc + h] = (zh * EXP2_SCALE).astype(BF16)


def q_call(x2, g, w_q, wgt_t, bgt, cosf, sinf, li, wi, C, R, tm):
    m, d = x2.shape
    nt = R // tm
    return pl.pallas_call(
        _q_kernel, grid=(m // tm,),
        in_specs=[pl.BlockSpec((tm, d), lambda i: (i, 0)),
                  _resident((None, 1, d), lambda i: (li, 0, 0)),
                  _resident((None, d, N_HEADS * HEAD_DIM), lambda i: (wi, 0, 0)),
                  _resident((LANES, d), lambda i: (0, 0)),
                  _resident((LANES, 1), lambda i: (0, 0)),
                  pl.BlockSpec((tm, HEAD_DIM), lambda i: (i % nt, 0)),
                  pl.BlockSpec((tm, HEAD_DIM), lambda i: (i % nt, 0))],
        out_specs=[pl.BlockSpec((1, N_HEADS, tm, HEAD_DIM), lambda i: (i // nt, 0, i % nt, 0)),
                   pl.BlockSpec((1, LANES, tm), lambda i: (i // nt, 0, i % nt))],
        out_shape=[jax.ShapeDtypeStruct((C, N_HEADS, R, HEAD_DIM), BF16),
                   jax.ShapeDtypeStruct((C, LANES, R), F32)],
        compiler_params=_cparams(("parallel",)),
        name="q_proj")(x2, g, w_q, wgt_t, bgt, cosf, sinf)


def _oproj_kernel(o_ref, w_ref, x_ref, out_ref):
    o = o_ref[...]
    for c in range(0, x_ref.shape[1], COL_CHUNK):
        out_ref[:, c:c + COL_CHUNK] = x_ref[:, c:c + COL_CHUNK] + _dot(o, w_ref[:, c:c + COL_CHUNK])


def oproj_call(o2, w_o, x2, li, tm):
    m, d = x2.shape
    return pl.pallas_call(
        _oproj_kernel, grid=(m // tm,),
        in_specs=[pl.BlockSpec((tm, d), lambda i: (i, 0)),
                  _resident((None, d, d), lambda i: (li, 0, 0)),
                  pl.BlockSpec((tm, d), lambda i: (i, 0))],
        out_specs=pl.BlockSpec((tm, d), lambda i: (i, 0)),
        out_shape=jax.ShapeDtypeStruct(x2.shape, F32),
        compiler_params=_cparams(("parallel",)),
        name="o_proj")(o2, w_o, x2)


def _compress_kernel(*refs, npage, rope, nslab, upb, slab, paged, transpose_out):
    if paged:
        refs = refs[1:]
    pages = refs[:npage]
    w1_ref, b1_ref, w2_ref, cos_ref, sin_ref, o_ref, p_s = refs[npage:]
    i = pl.program_id(1)
    upp = upb // npage
    unit_rows = CMP_STRIDE * N_KV_HEADS
    swapped = [jnp.swapaxes(pg[0].reshape(upp, unit_rows, HEAD_DIM), 0, 1) for pg in pages]
    for g in range(N_KV_HEADS):
        cols = []
        for s in range(CMP_STRIDE):
            parts = [sw[s * N_KV_HEADS + g] for sw in swapped]
            cols.append(parts[0] if npage == 1 else jnp.concatenate(parts, axis=0))
        lhs = jnp.concatenate(cols, axis=1).astype(BF16)
        p_s[g, pl.ds(pl.multiple_of(i * upb, upb), upb), :] = _dot(lhs, w1_ref[...])

    @pl.when(i == nslab - 1)
    def _():
        nu = nslab * upb
        for g in range(N_KV_HEADS):
            p_s[g, nu:nu + 8, :] = jnp.zeros((8, 2 * HEAD_DIM), F32)
            h = b1_ref[...] + p_s[g, 0:nu, 0:HEAD_DIM] + p_s[g, 1:nu + 1, HEAD_DIM:2 * HEAD_DIM]
            a = jax.nn.gelu(h).astype(BF16)
            if transpose_out:
                o_ref[0, g] = _dot_nt(w2_ref[...], a).astype(BF16)
            else:
                c = _dot(a, w2_ref[...])
                if rope:
                    c = _rope(c, cos_ref[...], sin_ref[...])
                o_ref[0, g] = c.astype(BF16)


def compress_call(src, page_table, w1r, b1, w2, cosf, sinf, rope, nbatch, nslab, slab, transpose_out=False):
    upb = slab // CMP_STRIDE
    nu = nslab * upb
    paged = page_table is not None
    npage = slab // PAGE_SIZE if paged else 1
    const = lambda b, i, *_: (0, 0)
    if paged:
        page_specs = [pl.BlockSpec((1, PAGE_SIZE * N_KV_HEADS, HEAD_DIM),
                                   functools.partial(lambda b, i, pt, r: (pt[b, i * npage + r], 0, 0), r=r))
                      for r in range(npage)]
    else:
        page_specs = [pl.BlockSpec((1, slab * N_KV_HEADS, HEAD_DIM), lambda b, i: (b, i, 0))]
    in_specs = page_specs + [pl.BlockSpec((CMP_STRIDE * HEAD_DIM, 2 * HEAD_DIM), const),
                             pl.BlockSpec((1, HEAD_DIM), const),
                             pl.BlockSpec((HEAD_DIM, HEAD_DIM), const),
                             pl.BlockSpec((nu, HEAD_DIM), const),
                             pl.BlockSpec((nu, HEAD_DIM), const)]
    o_dims = (HEAD_DIM, nu) if transpose_out else (nu, HEAD_DIM)
    out_specs = pl.BlockSpec((1, N_KV_HEADS) + o_dims, lambda b, i, *_: (b, 0, 0, 0))
    kern = functools.partial(_compress_kernel, npage=npage, rope=rope, nslab=nslab, upb=upb, slab=slab,
                             paged=paged, transpose_out=transpose_out)
    scratch = [pltpu.VMEM((N_KV_HEADS, nu + 8, 2 * HEAD_DIM), F32)]
    out_shape = jax.ShapeDtypeStruct((nbatch, N_KV_HEADS) + o_dims, BF16)
    args = [src] * npage + [w1r, b1, w2, cosf, sinf]
    if paged:
        gs = pltpu.PrefetchScalarGridSpec(num_scalar_prefetch=1, grid=(nbatch, nslab), in_specs=in_specs,
                                          out_specs=out_specs, scratch_shapes=scratch)
        return pl.pallas_call(kern, grid_spec=gs, out_shape=out_shape,
                              compiler_params=_cparams(("parallel", "arbitrary")),
                              name="compress_paged")(page_table, *args)
    return pl.pallas_call(kern, grid=(nbatch, nslab), in_specs=in_specs, out_specs=out_specs,
                          out_shape=out_shape, scratch_shapes=scratch,
                          compiler_params=_cparams(("parallel", "arbitrary")),
                          name="compress_rows")(*args)


def cmp_to_slc_matrix(n_cmp, n_slc):
    units = np.arange(n_cmp)[:, None] + np.arange(CMP_PARTS)[None, :]
    sblk = (units * CMP_STRIDE) // SLC_BLOCK
    m = np.zeros((n_cmp, n_slc), np.float32)
    np.add.at(m, (np.repeat(np.arange(n_cmp), CMP_PARTS), sblk.reshape(-1)), 1.0)
    return m


def _split_bf16(x):
    hi = x.astype(BF16)
    return hi, (x - hi.astype(F32)).astype(BF16)


def _softmax_rows(s, mask):
    s = jnp.where(mask, s, NEG_INF)
    m = jnp.max(s, axis=-1, keepdims=True)
    p = jnp.where(mask, jnp.exp2(s - m), 0.0)
    return p / jnp.maximum(jnp.sum(p, axis=-1, keepdims=True), 1e-30)


def _nsa_prompt_kernel(q_ref, kc_ref, vct_ref, sk_ref, svt_ref, wk_ref, wvt_ref, gate_ref, mt_ref, o_ref,
                       *, tq, n_cmp, n_slc):
    g = pl.program_id(1)
    qi = pl.program_id(2)
    t0 = qi * tq
    rows = GQA_REP * tq
    q = q_ref[0].reshape(rows, HEAD_DIM)

    def tile4(a):
        return jnp.concatenate([a] * GQA_REP, axis=1)

    ncp = kc_ref.shape[2]
    n_id = lax.broadcasted_iota(jnp.int32, (ncp, tq), 0)
    tq_pos = t0 + lax.broadcasted_iota(jnp.int32, (ncp, tq), 1)
    cmask = tile4((n_id * CMP_STRIDE + (CMP_BLOCK - 1) <= tq_pos) & (n_id < n_cmp))
    s = jnp.where(cmask, _dot_nt(kc_ref[0, 0], q), NEG_INF)
    m = jnp.max(s, axis=0, keepdims=True)
    p = jnp.where(cmask, jnp.exp2(s - m), 0.0)
    p = p / jnp.maximum(jnp.sum(p, axis=0, keepdims=True), 1e-30)
    o_cmp = _dot(vct_ref[0, 0], p.astype(BF16))
    p_grp = p[:, 0:tq] + p[:, tq:2 * tq] + p[:, 2 * tq:3 * tq] + p[:, 3 * tq:4 * tq]

    p_hi, p_lo = _split_bf16(p_grp)
    nsp = 8 * (-(-n_slc // 8))
    sc =(_dot(mt_ref[...], p_hi) + _dot(mt_ref[...], p_lo))[0:nsp]
    j_id = lax.broadcasted_iota(jnp.int32, (nsp, tq), 0)
    tpos = t0 + lax.broadcasted_iota(jnp.int32, (nsp, tq), 1)
    cur = tpos // SLC_BLOCK
    valid = (j_id * SLC_BLOCK <= tpos) & (j_id < n_slc)
    forced = (j_id == 0) | (j_id == cur) | (j_id == cur - 1)
    sc = jnp.where(valid, jnp.where(forced, FORCED_SCORE, sc), NEG_INF)
    cnt = jnp.zeros((nsp, tq), F32)
    for i in range(n_slc):
        ri = sc[i:i + 1, :]
        beats = (ri > sc) | ((ri == sc) & (j_id > i))
        cnt = cnt + jnp.where(beats, 1.0, 0.0)
    sel_bias = jnp.where((cnt < float(min(N_SELECT, n_slc))) & (sc > 0.5 * NEG_INF), 0.0, NEG_INF)
    if nsp < LANES:
        sel_bias = jnp.concatenate([sel_bias, jnp.zeros((LANES - nsp, tq), F32)], axis=0)
    sel_rows = sel_bias.T.astype(BF16)
    q_slc = jnp.concatenate([q, jnp.concatenate([sel_rows] * GQA_REP, axis=0)], axis=1)

    kp_off = lax.broadcasted_iota(jnp.int32, (tq, tq), 0)
    tq_off = lax.broadcasted_iota(jnp.int32, (tq, tq), 1)

    def slc_bias(k0, diag):
        return jnp.where(kp_off <= tq_off, 0.0, NEG_INF) if diag else None

    def win_bias(k0, diag):
        dist = (t0 + tq_off) - (k0 + kp_off)
        return jnp.where((dist >= 0) & (dist < WINDOW), 0.0, NEG_INF)

    def flash(k_ref, vt_ref, qmat, nblk, bias_fn):
        def update(its, carry, diag):
            m, l, acc = carry
            k0s = [pl.multiple_of((qi - it) * tq, tq) for it in its]
            sbs = []
            for n, k0 in enumerate(k0s):
                sb = _dot_nt(k_ref[0, 0, pl.ds(k0, tq), :], qmat)
                bias = bias_fn(k0, diag and n == 0)
                sbs.append(sb if bias is None else sb + tile4(bias))
            m_new = m
            for sb in sbs:
                m_new = jnp.maximum(m_new, jnp.max(sb, axis=0, keepdims=True))
            alpha = jnp.exp2(m - m_new)
            l = alpha * l
            acc = alpha * acc
            for k0, sb in zip(k0s, sbs):
                pb = jnp.exp2(sb - m_new)
                l = l + jnp.sum(pb, axis=0, keepdims=True)
                acc = acc + _dot(vt_ref[0, 0, :, pl.ds(k0, tq)], pb.astype(BF16))
            return m_new, l, acc

        init = (jnp.full((1, rows), NEG_INF, F32), jnp.zeros((1, rows), F32), jnp.zeros((HEAD_DIM, rows), F32))
        carry = lax.cond(nblk >= 2, lambda c: update([0, 1], c, True), lambda c: update([0], c, True), init)
        rest = jnp.maximum(nblk - 2, 0)
        carry = lax.fori_loop(0, rest // 2, lambda j, c: update([2 + 2 * j, 3 + 2 * j], c, False), carry)
        carry = lax.cond(rest % 2 == 1, lambda c: update([nblk - 1], c, False), lambda c: c, carry)
        _, l, acc = carry
        return acc / jnp.maximum(l, 1e-30)

    o_slc = flash(sk_ref, svt_ref, q_slc, qi + 1, slc_bias)
    o_win = flash(wk_ref, wvt_ref, q, jnp.minimum(qi, WINDOW // tq) + 1, win_bias)

    for r in range(GQA_REP):
        cs = slice(r * tq, (r + 1) * tq)
        c0 = g * GQA_REP + r
        o = (gate_ref[0, pl.ds(c0, 1), :] * o_cmp[:, cs] + gate_ref[0, pl.ds(N_HEADS + c0, 1), :] * o_slc[:, cs]
             + gate_ref[0, pl.ds(2 * N_HEADS + c0, 1), :] * o_win[:, cs])
        o_ref[0, :, r * HEAD_DIM:(r + 1) * HEAD_DIM] = o.T.astype(BF16)


def nsa_prompt_call(q, k_cmp, v_cmp_t, sk, sv_t, wk, wv_t, gates_t, tq):
    B, _, T, _ = q.shape
    ncp = k_cmp.shape[2]
    n_cmp = T // CMP_STRIDE - CMP_PARTS + 1
    n_slc = -(-T // SLC_BLOCK)
    mt = np.zeros((LANES, ncp), np.float32)
    mt[:n_slc, :n_cmp] = cmp_to_slc_matrix(n_cmp, n_slc).T
    mt = jnp.asarray(mt, BF16)
    sk_spec = pl.BlockSpec((1, 1, T, 2 * HEAD_DIM), lambda b, g, i: (b, g, 0, 0))
    k_spec = pl.BlockSpec((1, 1, T, HEAD_DIM), lambda b, g, i: (b, g, 0, 0))
    vt_spec = pl.BlockSpec((1, 1, HEAD_DIM, T), lambda b, g, i: (b, g, 0, 0))
    kern = functools.partial(_nsa_prompt_kernel, tq=tq, n_cmp=n_cmp, n_slc=n_slc)
    return pl.pallas_call(
        kern, grid=(B, N_KV_HEADS, T // tq),
        in_specs=[pl.BlockSpec((1, GQA_REP, tq, HEAD_DIM), lambda b, g, i: (b, g, i, 0)),
                  pl.BlockSpec((1, 1, ncp, HEAD_DIM), lambda b, g, i: (b, g, 0, 0)),
                  pl.BlockSpec((1, 1, HEAD_DIM, ncp), lambda b, g, i: (b, g, 0, 0)),
                  sk_spec, vt_spec, k_spec, vt_spec,
                  pl.BlockSpec((1, LANES, tq), lambda b, g, i: (b, 0, i)),
                  pl.BlockSpec((LANES, ncp), lambda b, g, i: (0, 0))],
        out_specs=pl.BlockSpec((1, tq, GQA_REP * HEAD_DIM), lambda b, g, i: (b, i, g)),
        out_shape=jax.ShapeDtypeStruct((B, T, N_HEADS * HEAD_DIM), BF16),
        compiler_params=_cparams(("parallel", "parallel", "arbitrary")),
        name="nsa_prompt")(q, k_cmp, v_cmp_t, sk, sv_t, wk, wv_t, gates_t, mt)


def _nsa_sample_select_kernel(q_ref, kc_ref, vc_ref, m_ref, ocmp_ref, idx_ref, ok_ref, *, nb, n_cmp, n_slc, qpos):
    rows = GQA_REP * nb
    q = q_ref[0].astype(F32).reshape(rows, HEAD_DIM).astype(BF16)
    ncp = kc_ref.shape[2]
    row_b = lax.broadcasted_iota(jnp.int32, (rows, 1), 0) % nb
    n_id = lax.broadcasted_iota(jnp.int32, (rows, ncp), 1)
    cmask = (n_id * CMP_STRIDE + (CMP_BLOCK - 1) <= qpos) & (n_id < n_cmp)
    o_acc = jnp.zeros((rows, HEAD_DIM), F32)
    pg = jnp.zeros((nb, ncp), F32)
    b_id = lax.broadcasted_iota(jnp.int32, (nb, 1), 0)
    for b in range(nb):
        p = _softmax_rows(_dot_nt(q, kc_ref[b, 0]), cmask)
        o_b = _dot(p.astype(BF16), vc_ref[b, 0])
        o_acc = o_acc + jnp.where(row_b == b, o_b, 0.0)
        p_sum = jnp.sum(p.reshape(GQA_REP, nb, ncp), axis=0)
        pg = pg + jnp.where(b_id == b, p_sum, 0.0)
    ocmp_ref[0] = o_acc.reshape(GQA_REP, nb, HEAD_DIM)

    p_hi, p_lo = _split_bf16(pg)
    sc = _dot(p_hi, m_ref[...]) + _dot(p_lo, m_ref[...])
    nsp = sc.shape[1]
    j_id = lax.broadcasted_iota(jnp.int32, (nb, nsp), 1)
    cur = qpos // SLC_BLOCK
    valid = (j_id * SLC_BLOCK <= qpos) & (j_id < n_slc)
    forced = (j_id == 0) | (j_id == cur) | (j_id == cur - 1)
    sc = jnp.where(valid, jnp.where(forced, FORCED_SCORE, sc), NEG_INF)
    lane = lax.broadcasted_iota(jnp.int32, (nb, LANES), 1)
    idx = jnp.zeros((nb, LANES), jnp.int32)
    okv = jnp.zeros((nb, LANES), jnp.int32)
    for i in range(min(N_SELECT, n_slc)):
        m = jnp.max(sc, axis=-1, keepdims=True)
        first = jnp.min(jnp.where(sc == m, j_id.astype(F32), float(nsp)), axis=-1, keepdims=True).astype(jnp.int32)
        idx = jnp.where(lane == i, first, idx)
        okv = jnp.where(lane == i, jnp.where(m > 0.5 * NEG_INF, 1, 0), okv)
        sc = jnp.where(j_id == first, -3.0e38, sc)
    idx_ref[0] = idx
    ok_ref[0] = okv


def nsa_sample_select_call(q, k_cmp, v_cmp, qpos):
    nb = q.shape[2]
    ncp = k_cmp.shape[2]
    n_cmp = (qpos + 1) // CMP_STRIDE - CMP_PARTS + 1
    n_slc = -(-(qpos + 1) // SLC_BLOCK)
    nsp = LANES * (-(-n_slc // LANES))
    m = np.zeros((ncp, nsp), np.float32)
    m[:n_cmp, :n_slc] = cmp_to_slc_matrix(n_cmp, n_slc)
    m = jnp.asarray(m, BF16)
    cmp_spec = pl.BlockSpec((nb, 1, ncp, HEAD_DIM), lambda g: (0, g, 0, 0))
    kern = functools.partial(_nsa_sample_select_kernel, nb=nb, n_cmp=n_cmp, n_slc=n_slc, qpos=qpos)
    return pl.pallas_call(
        kern, grid=(N_KV_HEADS,),
        in_specs=[pl.BlockSpec((1, GQA_REP, nb, HEAD_DIM), lambda g: (0, g, 0, 0)),
                  cmp_spec, cmp_spec,
                  pl.BlockSpec((ncp, nsp), lambda g: (0, 0))],
        out_specs=[pl.BlockSpec((1, GQA_REP, nb, HEAD_DIM), lambda g: (g, 0, 0, 0)),
                   pl.BlockSpec((1, nb, LANES), lambda g: (g, 0, 0)),
                   pl.BlockSpec((1, nb, LANES), lambda g: (g, 0, 0))],
        out_shape=[jax.ShapeDtypeStruct((N_KV_HEADS, GQA_REP, nb, HEAD_DIM), F32),
                   jax.ShapeDtypeStruct((N_KV_HEADS, nb, LANES), jnp.int32),
                   jax.ShapeDtypeStruct((N_KV_HEADS, nb, LANES), jnp.int32)],
        compiler_params=_cparams(("parallel",)),
        name="nsa_sample_select")(q, k_cmp, v_cmp, m)


def _nsa_sample_attend_kernel(idx_ref, ok_ref, pt_ref, q_ref, *rest, nb, nsel, n_past_blk, qpos):
    kblks, vblks = rest[:nsel], rest[nsel:2 * nsel]
    (wk_ref, wv_ref, skn_ref, svn_ref, wkn_ref, wvn_ref, ocmp_ref, gate_ref, o_ref) = rest[2 * nsel:]
    g = pl.program_id(0)
    b = pl.program_id(1)
    rows = GQA_REP * nb
    qf = q_ref[0].astype(F32).reshape(rows, HEAD_DIM)
    q = qf.astype(BF16)
    row_b = lax.broadcasted_iota(jnp.int32, (rows, 1), 0) % nb

    @pl.when(b == 0)
    def _():
        o_ref[...] = jnp.zeros_like(o_ref)

    head_rows = pl.ds(g, SLC_BLOCK, stride=N_KV_HEADS)
    kcat = jnp.concatenate([kb[0, head_rows, :] for kb in kblks], axis=0).astype(BF16)
    vcat = jnp.concatenate([vb[0, head_rows, :] for vb in vblks], axis=0).astype(BF16)
    nk = nsel * SLC_BLOCK
    slot = lax.broadcasted_iota(jnp.int32, (1, nk), 1) // SLC_BLOCK
    within = lax.broadcasted_iota(jnp.int32, (1, nk), 1) % SLC_BLOCK
    kpos = jnp.zeros((1, nk), jnp.int32)
    kok = jnp.zeros((1, nk), jnp.int32)
    new_ok = jnp.zeros((1, 1), jnp.int32)
    base = (g * nb + b) * LANES
    for s in range(nsel):
        bi = idx_ref[base + s]
        oks = ok_ref[base + s]
        kpos = jnp.where(slot == s, bi * SLC_BLOCK + within, kpos)
        kok = jnp.where(slot == s, jnp.where(bi < n_past_blk, oks, 0), kok)
        new_ok = jnp.maximum(new_ok, jnp.where(bi * SLC_BLOCK <= qpos, jnp.where(bi >= n_past_blk, oks, 0), 0))
    mask = (kok > 0) & (kpos <= qpos)
    s_past = jnp.where(mask, _dot_nt(q, kcat), NEG_INF)
    has_new = new_ok > 0
    s_new = jnp.where(has_new, jnp.sum(qf * skn_ref[0], axis=-1, keepdims=True), NEG_INF)
    m = jnp.maximum(jnp.max(s_past, axis=-1, keepdims=True), s_new)
    p_past = jnp.where(mask, jnp.exp2(s_past - m), 0.0)
    p_new = jnp.where(has_new, jnp.exp2(s_new - m), 0.0)
    den = jnp.maximum(jnp.sum(p_past, axis=-1, keepdims=True) + p_new, 1e-30)
    o_slc = (_dot(p_past.astype(BF16), vcat)
             + p_new * svn_ref[0]) / den

    nw = wk_ref.shape[1] // N_KV_HEADS
    win_rows = pl.ds(g, nw, stride=N_KV_HEADS)
    wpos = qpos - nw + lax.broadcasted_iota(jnp.int32, (1, nw), 1)
    wdist = qpos - wpos
    wmask = (wpos >= 0) & (wdist >= 0) & (wdist < WINDOW)
    s_w = jnp.where(wmask, _dot_nt(q, wk_ref[0, win_rows, :].astype(BF16)), NEG_INF)
    s_wn = jnp.sum(qf * wkn_ref[0], axis=-1, keepdims=True)
    mw = jnp.maximum(jnp.max(s_w, axis=-1, keepdims=True), s_wn)
    p_w = jnp.where(wmask, jnp.exp2(s_w - mw), 0.0)
    p_wn = jnp.exp2(s_wn - mw)
    den_w = jnp.maximum(jnp.sum(p_w, axis=-1, keepdims=True) + p_wn, 1e-30)
    o_win = (_dot(p_w.astype(BF16), wv_ref[0, win_rows, :].astype(BF16))
             + p_wn * wvn_ref[0]) / den_w

    gates = jnp.concatenate([gate_ref[...]] * GQA_REP, axis=0)
    lane = lax.broadcasted_iota(jnp.int32, (rows, LANES), 1)
    row_r = lax.broadcasted_iota(jnp.int32, (rows, 1), 0) // nb

    def gate_col(branch):
        c = branch * N_HEADS + g * GQA_REP + row_r
        return jnp.sum(jnp.where(lane == c, gates, 0.0), axis=-1, keepdims=True)

    o = (gate_col(0) * ocmp_ref[0].reshape(rows, HEAD_DIM) + gate_col(1) * o_slc + gate_col(2) * o_win)
    o_ref[0] += jnp.where(row_b == b, o, 0.0).reshape(GQA_REP, nb, HEAD_DIM)


def nsa_sample_attend_call(idx, ok, page_table, q, pool_k, pool_v, win_k, win_v, sk_new, sv_new, wk_new, wv_new,
                           o_cmp, gates, qpos):
    nb = q.shape[2]
    nsel = min(N_SELECT, -(-(qpos + 1) // SLC_BLOCK))
    n_past_blk = page_table.shape[1] * PAGE_SIZE // SLC_BLOCK
    half = PAGE_SIZE // SLC_BLOCK

    def blk_map(g, b, idx_r, ok_r, pt_r, s):
        bi = jnp.minimum(idx_r[(g * nb + b) * LANES + s], n_past_blk - 1)
        return (pt_r[b, bi // half], bi % half, 0)

    blk_specs = [pl.BlockSpec((1, SLC_BLOCK * N_KV_HEADS, HEAD_DIM), functools.partial(blk_map, s=s))
                 for s in range(nsel)]
    win_spec = pl.BlockSpec((1, win_k.shape[1], HEAD_DIM), lambda g, b, *_: (b, 0, 0))
    new_spec = pl.BlockSpec((1, 1, HEAD_DIM), lambda g, b, *_: (b * N_KV_HEADS + g, 0, 0))
    qo_spec = pl.BlockSpec((1, GQA_REP, nb, HEAD_DIM), lambda g, b, *_: (0, g, 0, 0))
    in_specs = ([qo_spec] + blk_specs + blk_specs + [win_spec, win_spec, new_spec, new_spec, new_spec, new_spec,
                pl.BlockSpec((1, GQA_REP, nb, HEAD_DIM), lambda g, b, *_: (g, 0, 0, 0)),
                pl.BlockSpec((nb, LANES), lambda g, b, *_: (0, 0))])
    gs = pltpu.PrefetchScalarGridSpec(
        num_scalar_prefetch=3, grid=(N_KV_HEADS, nb), in_specs=in_specs,
        out_specs=pl.BlockSpec((1, GQA_REP, nb, HEAD_DIM), lambda g, b, *_: (g, 0, 0, 0)))
    kern = functools.partial(_nsa_sample_attend_kernel, nb=nb, nsel=nsel, n_past_blk=n_past_blk, qpos=qpos)
    return pl.pallas_call(
        kern, grid_spec=gs,
        out_shape=jax.ShapeDtypeStruct((N_KV_HEADS, GQA_REP, nb, HEAD_DIM), F32),
        compiler_params=_cparams(("parallel", "arbitrary")),
        name="nsa_sample_attend")(idx.reshape(-1), ok.reshape(-1), page_table, q,
                                  *([pool_k] * nsel), *([pool_v] * nsel), win_k, win_v,
                                  sk_new, sv_new, wk_new, wv_new, o_cmp, gates)


def _rope_tables(pos):
    half = HEAD_DIM // 2
    inv_freq = ROPE_THETA ** (-jnp.arange(half, dtype=F32) / half)
    ang = pos.astype(F32)[:, None] * inv_freq[None, :]
    cos, sin = jnp.cos(ang), jnp.sin(ang)
    return jnp.concatenate([cos, cos], axis=-1), jnp.concatenate([-sin, sin], axis=-1)


def _trunk(x2, p3, pos, h0r, h0i, w, lay, kv_ctx):
    nb, T, tc, C, R, tm = lay
    depth = w["norm_mix_g"].shape[0]
    n_a = w["ssm_a_re"].shape[0]
    d = x2.shape[1]
    cosf, sinf = _rope_tables(pos)
    ssm_re, ssm_im = [], []
    kv = None
    for i in range(depth):
        if i < n_a:
            prm = s5_params(w["ssm_a_re"][i], w["ssm_a_im"][i], w["ssm_log_dt"][i], w["ssm_b_re"][i],
                            w["ssm_b_im"][i], w["ssm_c_re"][i], w["ssm_c_im"][i], w["ssm_d"][i], nb)
            g3, hr, hi = s5_mixer(x2.reshape(nb, T, d), w["norm_mix_g"][i][None], prm, h0r[i], h0i[i], tc)
            ssm_re.append(rows_to_state(hr, nb))
            ssm_im.append(rows_to_state(hi, nb))
            x2 = glu_call(g3.reshape(nb * T, d), w["ssm_w_glu"], w["ssm_b_glu"], x2, i, tm)
        else:
            j = i - n_a
            q, gates_t = q_call(x2, w["norm_mix_g3"], w["nsa_w_q"], w["nsa_wgt_t"][j], w["nsa_bgt"][j],
                                cosf, sinf, i, j, C, R, tm)
            o2 = kv_ctx["attend"](q, gates_t, kv)
            x2 = oproj_call(o2, w["nsa_w_o"], x2, j, tm)
        x2 = kv_ctx["ffn"](i, x2)
        x2 = ple_call(x2, w["norm_ple_g3"], p3, w["ple_w_gate"], w["ple_w_proj"], w["norm_final_g"][None], i, tm,
                      final_norm=(i == depth - 1))
        if i == n_a - 1:
            kv = kv_ctx["build"](x2, cosf, sinf)
    return x2, jnp.stack(ssm_re), jnp.stack(ssm_im), kv["new"]


def _cmp_weights(w1, b1, w2):
    w1p = w1.reshape(CMP_PARTS, CMP_STRIDE * HEAD_DIM, HEAD_DIM)
    return jnp.concatenate([w1p[0], w1p[1]], axis=1).astype(BF16), b1[None], w2.astype(BF16)


def kernel(x_prompt, x_sample, state_ssm_re, state_ssm_im, cache_cmp_k, cache_cmp_v, cache_slc_k, cache_slc_v, state_win_k, state_win_v, page_table, p_prompt, p_sample, norm_mix_g, norm_ffn_g, norm_ple_g, norm_kv_g, norm_final_g, ssm_a_re, ssm_a_im, ssm_log_dt, ssm_b_re, ssm_b_im, ssm_c_re, ssm_c_im, ssm_d, ssm_w_glu, ssm_b_glu, kv_w, cmp_k_w1, cmp_k_b1, cmp_k_w2, cmp_v_w1, cmp_v_b1, cmp_v_w2, nsa_w_q, nsa_w_gate, nsa_b_gate, nsa_w_o, ffn_w_gate, ffn_w_up, ffn_w_down, ple_w_gate, ple_w_proj):
    B, T, D = x_prompt.shape
    nbs, ts, _ = x_sample.shape
    assert ts == 1
    n_a = ssm_a_re.shape[0]
    past = page_table.shape[1] * PAGE_SIZE
    gate_pad = LANES - nsa_w_gate.shape[-1]
    w = dict(norm_mix_g=norm_mix_g, norm_mix_g3=norm_mix_g[:, None], norm_ffn_g3=norm_ffn_g[:, None],
             norm_ple_g3=norm_ple_g[:, None], norm_final_g=norm_final_g,
             ssm_a_re=ssm_a_re, ssm_a_im=ssm_a_im, ssm_log_dt=ssm_log_dt, ssm_b_re=ssm_b_re, ssm_b_im=ssm_b_im,
             ssm_c_re=ssm_c_re, ssm_c_im=ssm_c_im, ssm_d=ssm_d,
             ssm_w_glu=ssm_w_glu.astype(BF16), ssm_b_glu=ssm_b_glu[:, None],
             nsa_w_q=nsa_w_q.astype(BF16),
             nsa_wgt_t=jnp.pad(jnp.swapaxes(nsa_w_gate, 1, 2), ((0, 0), (0, gate_pad), (0, 0))).astype(BF16),
             nsa_bgt=jnp.pad(nsa_b_gate, ((0, 0), (0, gate_pad)))[..., None],
             nsa_w_o=nsa_w_o.astype(BF16),
             ple_w_gate=ple_w_gate.astype(BF16), ple_w_proj=ple_w_proj.astype(BF16))
    kvw = kv_w.astype(BF16)
    kvg = norm_kv_g[None]
    ck_w = _cmp_weights(cmp_k_w1, cmp_k_b1, cmp_k_w2)
    cv_w = _cmp_weights(cmp_v_w1, cmp_v_b1, cmp_v_w2)

    tm_p = min(512, T)
    tq = min(256, T)
    pos_p = jnp.arange(T, dtype=jnp.int32)

    def build_prompt(x2, cosf, sinf):
        ck, cv, sk, sv, wk, wv, skh, wkh, svt, wvt = kv_call(x2, kvg, kvw, cosf, sinf, B, T, tm_p, True)
        slab = min(SLAB_ROWS, T)
        nslab = T // slab
        nu = T // CMP_STRIDE
        ccos, csin = _rope_tables(jnp.arange(nu, dtype=jnp.int32) * CMP_STRIDE + (CMP_BLOCK - 1))
        rows3 = lambda a: a.reshape(B, T * N_KV_HEADS, HEAD_DIM)
        k_cmp = compress_call(rows3(ck), None, *ck_w, ccos, csin, True, B, nslab, slab)
        v_cmp_t = compress_call(rows3(cv), None, cv_w[0], cv_w[1], cv_w[2].T, ccos, csin, False, B, nslab, slab,
                                transpose_out=True)
        r4 = lambda a: a.reshape(B, T, N_KV_HEADS, HEAD_DIM)
        ck, cv, sk, sv, wk, wv = [r4(a) for a in (ck, cv, sk, sv, wk, wv)]
        wb = min(WINDOW, T)
        new = (ck, cv, sk, sv, wk[:, T - wb:], wv[:, T - wb:])
        return dict(k_cmp=k_cmp, v_cmp_t=v_cmp_t, sk=skh, sv_t=svt, wk=wkh, wv_t=wvt, new=new)

    def attend_prompt(q, gates_t, kv):
        o = nsa_prompt_call(q, kv["k_cmp"], kv["v_cmp_t"], kv["sk"], kv["sv_t"], kv["wk"], kv["wv_t"], gates_t, tq)
        return o.reshape(B * T, N_HEADS * HEAD_DIM)

    ffn_bf16 = {}

    def ffn_sample(i, x2):
        x2, *ffn_bf16[i] = ffn_call(x2, w["norm_ffn_g3"], ffn_w_gate, ffn_w_up, ffn_w_down, i, nbs, 512,
                                    emit_bf16=True)
        return x2

    def ffn_prompt(i, x2):
        return ffn_call(x2, w["norm_ffn_g3"], *ffn_bf16[i], i, tm_p, 512)[0]

    pool3 = lambda a: a.reshape(a.shape[0], -1, HEAD_DIM)
    qpos = past
    pos_s = jnp.full((nbs,), past, jnp.int32)

    def build_sample(x2, cosf, sinf):
        ck, cv, sk, sv, wk, wv = kv_call(x2, kvg, kvw, cosf, sinf, 1, nbs, nbs, False)
        slab = min(SLAB_ROWS, past)
        nslab = past // slab
        nu = past // CMP_STRIDE
        ccos, csin = _rope_tables(jnp.arange(nu, dtype=jnp.int32) * CMP_STRIDE + (CMP_BLOCK - 1))
        k_cmp = compress_call(pool3(cache_cmp_k), page_table, *ck_w, ccos, csin, True, nbs, nslab, slab)
        v_cmp = compress_call(pool3(cache_cmp_v), page_table, *cv_w, ccos, csin, False, nbs, nslab, slab)
        r4 = lambda a: a.reshape(nbs, 1, N_KV_HEADS, HEAD_DIM)
        win_k = jnp.concatenate([state_win_k[:, 1:], r4(wk)], axis=1)
        win_v = jnp.concatenate([state_win_v[:, 1:], r4(wv)], axis=1)
        new = (r4(ck), r4(cv), r4(sk), r4(sv), win_k, win_v)
        return dict(k_cmp=k_cmp, v_cmp=v_cmp, sk=sk, sv=sv, wk=wk, wv=wv, new=new)

    def attend_sample(q, gates_t, kv):
        o_cmp, idx, ok = nsa_sample_select_call(q, kv["k_cmp"], kv["v_cmp"], qpos)
        rows1 = lambda a: a.reshape(nbs * N_KV_HEADS, 1, HEAD_DIM)
        o = nsa_sample_attend_call(idx, ok, page_table, q, pool3(cache_slc_k), pool3(cache_slc_v),
                                   pool3(state_win_k), pool3(state_win_v),
                                   rows1(kv["sk"]), rows1(kv["sv"]), rows1(kv["wk"]), rows1(kv["wv"]),
                                   o_cmp, gates_t[0].T, qpos)
        return jnp.transpose(o, (2, 0, 1, 3)).reshape(nbs, N_HEADS * HEAD_DIM).astype(BF16)

    h0r = jnp.stack([state_to_rows(state_ssm_re[i], nbs) for i in range(n_a)])
    h0i = jnp.stack([state_to_rows(state_ssm_im[i], nbs) for i in range(n_a)])
    y_s, sre_s, sim_s, new_s = _trunk(x_sample.reshape(nbs, D), p_sample.reshape(-1, nbs, p_sample.shape[-1]),
                                      pos_s, h0r, h0i, w, (nbs, 1, 1, 1, nbs, nbs),
                                      dict(build=build_sample, attend=attend_sample, ffn=ffn_sample))
    y_s = y_s.reshape(nbs, 1, D)

    h0 = jnp.zeros((n_a, D // PAIR_CH, 2 * B, PAIR_ST), F32)
    y_p, sre_p, sim_p, new_p = _trunk(x_prompt.reshape(B * T, D), p_prompt.reshape(-1, B * T, p_prompt.shape[-1]),
                                      pos_p, h0, h0, w, (B, T, min(128, T), B, T, tm_p),
                                      dict(build=build_prompt, attend=attend_prompt, ffn=ffn_prompt))
    y_p = y_p.reshape(B, T, D)

    ck_p, cv_p, sk_p, sv_p, wk_p, wv_p = new_p
    ck_s, cv_s, sk_s, sv_s, wk_s, wv_s = new_s
    return (y_p, y_s, sre_p, sim_p, ck_p, cv_p, sk_p, sv_p, wk_p, wv_p,
            sre_s, sim_s, ck_s, cv_s, sk_s, sv_s, wk_s, wv_s)
```

```python
import functools

import numpy as np
import jax
import jax.numpy as jnp
from jax import lax
from jax.experimental import pallas as pl
from jax.experimental.pallas import tpu as pltpu

F32 = jnp.float32
BF16 = jnp.bfloat16

N_HEADS = 16
N_KV_HEADS = 4
GQA_REP = N_HEADS // N_KV_HEADS
HEAD_DIM = 128
SSM_GROUP = 16
SSM_STATE = 64
CMP_BLOCK = 32
CMP_STRIDE = 16
CMP_PARTS = CMP_BLOCK // CMP_STRIDE
SLC_BLOCK = 64
N_SELECT = 16
WINDOW = 512
PAGE_SIZE = 128
ROPE_THETA = 10000.0
ATTN_SCALE = HEAD_DIM ** -0.5
EXP2_SCALE = ATTN_SCALE * float(np.log2(np.e))
NORM_EPS = 1e-6
NEG_INF = -1e30
FORCED_SCORE = 1e9

LANES = 128
KV_LANES = N_KV_HEADS * HEAD_DIM
PAIR_CH = 2 * LANES
PAIR_ST = (LANES // SSM_GROUP) * SSM_STATE
SLAB_ROWS = 2048
PAGED_SLAB_ROWS = 4096
ROW_TILE = 512
FFN_TILE = 512
Q_TILE = 256
SCAN_STEPS = 128
VMEM_LIMIT = 56 * 1024 * 1024


def _cparams(sem, vmem=VMEM_LIMIT):
    return pltpu.CompilerParams(dimension_semantics=sem, vmem_limit_bytes=vmem)


def _rms(x, g):
    ms = jnp.mean(x * x, axis=-1, keepdims=True)
    return x * lax.rsqrt(ms + NORM_EPS) * g


def _rope(x, cosf, sinf):
    return x * cosf + pltpu.roll(x, HEAD_DIM // 2, axis=1) * sinf


def _dot(a, b):
    return jnp.dot(a, b, preferred_element_type=F32)


def _dot_nt(a, b):
    return lax.dot_general(a, b, (((1,), (1,)), ((), ())), preferred_element_type=F32)


def _s5_kernel(x_ref, gn_ref, wre_ref, wim_ref, cre_ref, cim_ref, ar_ref, ai_ref, d_ref, h0r_ref, h0i_ref,
               g_ref, hr_ref, hi_ref, hn_s, xr_s, xi_s, sr_s, si_s, *, tc, nb, npair):
    @pl.when(pl.program_id(0) == 0)
    def _():
        hr_ref[...] = h0r_ref[...]
        hi_ref[...] = h0i_ref[...]

    r2 = 2 * nb
    rows = tc * r2
    for b in range(nb):
        hn_s[b] = _rms(x_ref[b], gn_ref[...])
    keep = ((lax.broadcasted_iota(jnp.int32, (r2, PAIR_CH), 0) // nb == 1)
            == (lax.broadcasted_iota(jnp.int32, (r2, PAIR_CH), 1) >= LANES))
    lane_hi = lax.broadcasted_iota(jnp.int32, (tc, PAIR_CH), 1) >= LANES

    for k in range(npair):
        cols = slice(k * PAIR_CH, (k + 1) * PAIR_CH)

        if tc > 1:
            hd = jnp.concatenate([hn_s[:, :, cols], hn_s[:, :, cols]], axis=0)
            lhs = jnp.where(keep[None], jnp.swapaxes(hd, 0, 1), 0.0).reshape(rows, PAIR_CH).astype(BF16)
        else:
            v = hn_s[:, 0, cols]
            lhs = jnp.where(keep, jnp.concatenate([v, v], axis=0), 0.0).astype(BF16)
        xr_s[...] = _dot(lhs, wre_ref[k]).reshape(tc, r2, PAIR_ST)
        xi_s[...] = _dot(lhs, wim_ref[k]).reshape(tc, r2, PAIR_ST)
        ar = ar_ref[k]
        ai = ai_ref[k]

        def step(t, carry):
            hr, hi = carry
            nr = ar * hr - ai * hi + xr_s[t]
            ni = ar * hi + ai * hr + xi_s[t]
            sr_s[t] = nr
            si_s[t] = ni
            return nr, ni

        hr, hi = lax.fori_loop(0, tc, step, (hr_ref[k], hi_ref[k]), unroll=min(8, tc))
        hr_ref[k] = hr
        hi_ref[k] = hi
        h_re = sr_s[...].reshape(rows, PAIR_ST).astype(BF16)
        h_im = si_s[...].reshape(rows, PAIR_ST).astype(BF16)
        y = (_dot(h_re, cre_ref[k]) - _dot(h_im, cim_ref[k])).reshape(tc, r2, PAIR_CH)
        if tc > 1:
            y = jnp.swapaxes(y, 0, 1)
            per_row = lambda r: y[r]
        else:
            per_row = lambda r: y[:, r, :]
        for b in range(nb):
            yb = jnp.where(lane_hi, per_row(nb + b), per_row(b))
            yb = yb + d_ref[:, cols] * hn_s[b, :, cols]
            g_ref[b, :, cols] = jax.nn.gelu(yb).astype(BF16)


def s5_mixer(x3, gn, prm, h0r, h0i, tc):
    nb, T, d = x3.shape
    r2 = 2 * nb
    npair = d // PAIR_CH
    full3 = lambda s: (0, 0, 0)
    st_shape = (npair, r2, PAIR_ST)
    kern = functools.partial(_s5_kernel, tc=tc, nb=nb, npair=npair)
    return pl.pallas_call(
        kern, grid=(T // tc,),
        in_specs=[pl.BlockSpec((nb, tc, d), lambda s: (0, s, 0)),
                  pl.BlockSpec((1, d), lambda s: (0, 0)),
                  pl.BlockSpec((npair, PAIR_CH, PAIR_ST), full3),
                  pl.BlockSpec((npair, PAIR_CH, PAIR_ST), full3),
                  pl.BlockSpec((npair, PAIR_ST, PAIR_CH), full3),
                  pl.BlockSpec((npair, PAIR_ST, PAIR_CH), full3),
                  pl.BlockSpec(st_shape, full3),
                  pl.BlockSpec(st_shape, full3),
                  pl.BlockSpec((1, d), lambda s: (0, 0)),
                  pl.BlockSpec(st_shape, full3),
                  pl.BlockSpec(st_shape, full3)],
        out_specs=[pl.BlockSpec((nb, tc, d), lambda s: (0, s, 0)),
                   pl.BlockSpec(st_shape, full3),
                   pl.BlockSpec(st_shape, full3)],
        out_shape=[jax.ShapeDtypeStruct((nb, T, d), BF16),
                   jax.ShapeDtypeStruct(st_shape, F32),
                   jax.ShapeDtypeStruct(st_shape, F32)],
        scratch_shapes=[pltpu.VMEM((nb, tc, d), F32)] + [pltpu.VMEM((tc, r2, PAIR_ST), F32)] * 4,
        compiler_params=_cparams(("arbitrary",)),
        name="s5_mixer")(x3, gn, prm["wre"], prm["wim"], prm["cre"], prm["cim"],
                         prm["ar"], prm["ai"], prm["d"], h0r, h0i)


def s5_params(a_re, a_im, log_dt, b_re, b_im, c_re, c_im, d, nb):
    G, P, H = b_re.shape
    gpc = LANES // H
    nchunk = G // gpc
    dt = jnp.exp(log_dt)[:, None]
    mag = jnp.exp(dt * a_re)
    ab_re, ab_im = mag * jnp.cos(dt * a_im), mag * jnp.sin(dt * a_im)
    den = a_re * a_re + a_im * a_im
    nr, ni = ab_re - 1.0, ab_im
    cf_re = (nr * a_re + ni * a_im) / den
    cf_im = (ni * a_re - nr * a_im) / den
    bb_re = cf_re[..., None] * b_re - cf_im[..., None] * b_im
    bb_im = cf_re[..., None] * b_im + cf_im[..., None] * b_re
    eye = jnp.eye(gpc, dtype=F32)

    def blk_in(bb):
        w = jnp.einsum("cgph,gk->cghkp", bb.reshape(nchunk, gpc, P, H), eye)
        return w.reshape(nchunk // 2, PAIR_CH, gpc * P).astype(BF16)

    def blk_out(cc):
        w = jnp.einsum("cghp,gk->ckpgh", cc.reshape(nchunk, gpc, H, P), eye)
        w = w.reshape(nchunk // 2, 2, gpc * P, LANES)
        return jnp.swapaxes(w, 1, 2).reshape(nchunk // 2, gpc * P, PAIR_CH).astype(BF16)

    def rows(a):
        a = a.reshape(nchunk // 2, 2, 1, gpc * P)
        return jnp.broadcast_to(a, (nchunk // 2, 2, nb, gpc * P)).reshape(nchunk // 2, 2 * nb, gpc * P)

    return dict(wre=blk_in(bb_re), wim=blk_in(bb_im), cre=blk_out(c_re), cim=blk_out(c_im),
                ar=rows(ab_re), ai=rows(ab_im), d=d[None])


def state_to_rows(h, nb):
    npair = h.shape[1] * h.shape[2] // PAIR_ST // 2
    return jnp.transpose(h.reshape(nb, npair, 2, PAIR_ST), (1, 2, 0, 3)).reshape(npair, 2 * nb, PAIR_ST)


def rows_to_state(h, nb):
    npair = h.shape[0]
    h = jnp.transpose(h.reshape(npair, 2, nb, PAIR_ST), (2, 0, 1, 3))
    return h.reshape(nb, npair * 2 * PAIR_ST // SSM_STATE, SSM_STATE)


COL_CHUNK = 512


def _resident(block_shape, index_map):
    return pl.BlockSpec(block_shape, index_map, pipeline_mode=pl.Buffered(1))


def _glu_kernel(a_ref, w_ref, b_ref, x_ref, o_ref):
    a = a_ref[...]
    d = x_ref.shape[1]
    for c in range(0, d, COL_CHUNK):
        z1 = _dot(a, w_ref[:, c:c + COL_CHUNK]) + b_ref[:, c:c + COL_CHUNK]
        z2 = _dot(a, w_ref[:, d + c:d + c + COL_CHUNK]) + b_ref[:, d + c:d + c + COL_CHUNK]
        o_ref[:, c:c + COL_CHUNK] = x_ref[:, c:c + COL_CHUNK] + z1 * jax.nn.sigmoid(z2)


def glu_call(g2, w_glu, b_glu, x2, li, tm):
    m, d = x2.shape
    return pl.pallas_call(
        _glu_kernel, grid=(m // tm,),
        in_specs=[pl.BlockSpec((tm, d), lambda i: (i, 0)),
                  _resident((None, d, 2 * d), lambda i: (li, 0, 0)),
                  _resident((None, 1, 2 * d), lambda i: (li, 0, 0)),
                  pl.BlockSpec((tm, d), lambda i: (i, 0))],
        out_specs=pl.BlockSpec((tm, d), lambda i: (i, 0)),
        out_shape=jax.ShapeDtypeStruct(x2.shape, F32),
        compiler_params=_cparams(("parallel",)),
        name="s5_glu")(g2, w_glu, b_glu, x2)


def _ffn_kernel(x_ref, g_ref, wg_ref, wu_ref, wd_ref, o_ref, *rest, nf, emit_bf16):
    hn_s, acc_s = rest[-2:]
    f = pl.program_id(1)

    @pl.when(f == 0)
    def _():
        hn_s[...] = _rms(x_ref[...], g_ref[...]).astype(BF16)
        acc_s[...] = jnp.zeros_like(acc_s)

    wg, wu, wd = wg_ref[...].astype(BF16), wu_ref[...].astype(BF16), wd_ref[...].astype(BF16)
    if emit_bf16:
        for ref, val in zip(rest[:3], (wg, wu, wd)):
            ref[...] = val
    h = hn_s[...]
    a = _dot(h, wg)
    u = _dot(h, wu)
    act = (a * jax.nn.sigmoid(a) * u).astype(BF16)
    acc_s[...] += _dot(act, wd)

    @pl.when(f == nf - 1)
    def _():
        o_ref[...] = x_ref[...] + acc_s[...]


def ffn_call(x2, g, wg, wu, wd, li, tm, tf, emit_bf16=False):
    m, d = x2.shape
    ff = wg.shape[-1]
    nf = ff // tf
    if emit_bf16:
        assert m == tm
        w_specs = [pl.BlockSpec((None, d, tf), lambda i, f: (li, 0, f)),
                   pl.BlockSpec((None, d, tf), lambda i, f: (li, 0, f)),
                   pl.BlockSpec((None, tf, d), lambda i, f: (li, f, 0))]
    else:
        w_specs = [pl.BlockSpec((d, tf), lambda i, f: (0, f)),
                   pl.BlockSpec((d, tf), lambda i, f: (0, f)),
                   pl.BlockSpec((tf, d), lambda i, f: (f, 0))]
    out_specs = [pl.BlockSpec((tm, d), lambda i, f: (i, 0))]
    out_shape = [jax.ShapeDtypeStruct(x2.shape, F32)]
    if emit_bf16:
        out_specs += [pl.BlockSpec((d, tf), lambda i, f: (0, f)), pl.BlockSpec((d, tf), lambda i, f: (0, f)),
                      pl.BlockSpec((tf, d), lambda i, f: (f, 0))]
        out_shape += [jax.ShapeDtypeStruct((d, ff), BF16)] * 2 + [jax.ShapeDtypeStruct((ff, d), BF16)]
    return pl.pallas_call(
        functools.partial(_ffn_kernel, nf=nf, emit_bf16=emit_bf16), grid=(m // tm, nf),
        in_specs=[pl.BlockSpec((tm, d), lambda i, f: (i, 0)),
                  pl.BlockSpec((None, 1, d), lambda i, f: (li, 0, 0))] + w_specs,
        out_specs=out_specs, out_shape=out_shape,
        scratch_shapes=[pltpu.VMEM((tm, d), BF16), pltpu.VMEM((tm, d), F32)],
        compiler_params=_cparams(("parallel", "arbitrary")),
        name="ffn")(x2, g, wg, wu, wd)


def _ple_kernel(x_ref, g_ref, p_ref, wg_ref, wp_ref, gf_ref, o_ref, *, final_norm):
    hn = _rms(x_ref[...], g_ref[...]).astype(BF16)
    pe = p_ref[...].astype(BF16)
    for c in range(0, x_ref.shape[1], COL_CHUNK):
        cs = slice(c, c + COL_CHUNK)
        gate = jax.nn.sigmoid(_dot(hn, wg_ref[:, cs]))
        o_ref[:, cs] = x_ref[:, cs] + gate * _dot(pe, wp_ref[:, cs])
    if final_norm:
        o_ref[...] = _rms(o_ref[...], gf_ref[...])


def ple_call(x2, g, p3, wg, wp, gf, li, tm, final_norm):
    m, d = x2.shape
    pd = wp.shape[1]
    return pl.pallas_call(
        functools.partial(_ple_kernel, final_norm=final_norm), grid=(m // tm,),
        in_specs=[pl.BlockSpec((tm, d), lambda i: (i, 0)),
                  _resident((None, 1, d), lambda i: (li, 0, 0)),
                  pl.BlockSpec((None, tm, pd), lambda i: (li, i, 0)),
                  _resident((None, d, d), lambda i: (li, 0, 0)),
                  _resident((None, pd, d), lambda i: (li, 0, 0)),
                  _resident((1, d), lambda i: (0, 0))],
        out_specs=pl.BlockSpec((tm, d), lambda i: (i, 0)),
        out_shape=jax.ShapeDtypeStruct(x2.shape, F32),
        compiler_params=_cparams(("parallel",)),
        name="ple")(x2, g, p3, wg, wp, gf)


def _kv_kernel(x_ref, g_ref, w_ref, cos_ref, sin_ref, *outs, heads_out, nt):
    hn = _rms(x_ref[...], g_ref[...]).astype(BF16)
    tm = x_ref.shape[0]
    if heads_out:
        t_row = (pl.program_id(0) % nt) * tm + lax.broadcasted_iota(jnp.int32, (tm, LANES), 0)
        blk_onehot = jnp.where(lax.broadcasted_iota(jnp.int32, (tm, LANES), 1) == t_row // SLC_BLOCK, 1.0, 0.0)
        for h in range(N_KV_HEADS):
            outs[6][0, h, :, HEAD_DIM:] = blk_onehot.astype(BF16)
    for k in range(6):
        z = _dot(hn, w_ref[:, k * KV_LANES:(k + 1) * KV_LANES])
        for h in range(N_KV_HEADS):
            zh = z[:, h * HEAD_DIM:(h + 1) * HEAD_DIM]
            if k in (2, 4):
                zh = _rope(zh, cos_ref[...], sin_ref[...])
            outs[k][pl.ds(h, tm, stride=N_KV_HEADS), :] = zh
            if heads_out and k == 2:
                outs[6][0, h, :, :HEAD_DIM] = zh.astype(BF16)
            if heads_out and k == 4:
                outs[7][0, h] = zh.astype(BF16)
            if heads_out and k in (3, 5):
                outs[8 + (k == 5)][0, h] = zh.T.astype(BF16)


def kv_call(x2, g, kv_w, cosf, sinf, C, R, tm, heads_out):
    m, d = x2.shape
    nt = R // tm
    out_shape = [jax.ShapeDtypeStruct((m * N_KV_HEADS, HEAD_DIM), F32)] * 6
    out_specs = [pl.BlockSpec((tm * N_KV_HEADS, HEAD_DIM), lambda i: (i, 0))] * 6
    if heads_out:
        assert -(-R // SLC_BLOCK) <= LANES
        for width in (2 * HEAD_DIM, HEAD_DIM):
            out_shape += [jax.ShapeDtypeStruct((C, N_KV_HEADS, R, width), BF16)]
            out_specs += [pl.BlockSpec((1, N_KV_HEADS, tm, width), lambda i: (i // nt, 0, i % nt, 0))]
        out_shape += [jax.ShapeDtypeStruct((C, N_KV_HEADS, HEAD_DIM, R), BF16)] * 2
        out_specs += [pl.BlockSpec((1, N_KV_HEADS, HEAD_DIM, tm), lambda i: (i // nt, 0, 0, i % nt))] * 2
    return pl.pallas_call(
        functools.partial(_kv_kernel, heads_out=heads_out, nt=nt), grid=(m // tm,),
        in_specs=[pl.BlockSpec((tm, d), lambda i: (i, 0)),
                  _resident((1, d), lambda i: (0, 0)),
                  _resident((d, 6 * KV_LANES), lambda i: (0, 0)),
                  pl.BlockSpec((tm, HEAD_DIM), lambda i: (i % nt, 0)),
                  pl.BlockSpec((tm, HEAD_DIM), lambda i: (i % nt, 0))],
        out_specs=out_specs, out_shape=out_shape,
        compiler_params=_cparams(("parallel",)),
        name="kv_proj")(x2, g, kv_w, cosf, sinf)


def _q_kernel(x_ref, g_ref, w_ref, wgt_ref, bgt_ref, cos_ref, sin_ref, q_ref, gate_ref):
    hn = _rms(x_ref[...], g_ref[...]).astype(BF16)
    gate_ref[0] = jax.nn.sigmoid(_dot_nt(wgt_ref[...], hn) + bgt_ref[...])
    hpc = COL_CHUNK // HEAD_DIM
    for c in range(N_HEADS // hpc):
        z = _dot(hn, w_ref[:, c * COL_CHUNK:(c + 1) * COL_CHUNK])
        for h in range(hpc):
            zh = _rope(z[:, h * HEAD_DIM:(h + 1) * HEAD_DIM], cos_ref[...], sin_ref[...])
            q_ref[0, c * hpc + h] = (zh * EXP2_SCALE).astype(BF16)


def q_call(x2, g, w_q, wgt_t, bgt, cosf, sinf, li, wi, C, R, tm):
    m, d = x2.shape
    nt = R // tm
    return pl.pallas_call(
        _q_kernel, grid=(m // tm,),
        in_specs=[pl.BlockSpec((tm, d), lambda i: (i, 0)),
                  _resident((None, 1, d), lambda i: (li, 0, 0)),
                  _resident((None, d, N_HEADS * HEAD_DIM), lambda i: (wi, 0, 0)),
                  _resident((LANES, d), lambda i: (0, 0)),
                  _resident((LANES, 1), lambda i: (0, 0)),
                  pl.BlockSpec((tm, HEAD_DIM), lambda i: (i % nt, 0)),
                  pl.BlockSpec((tm, HEAD_DIM), lambda i: (i % nt, 0))],
        out_specs=[pl.BlockSpec((1, N_HEADS, tm, HEAD_DIM), lambda i: (i // nt, 0, i % nt, 0)),
                   pl.BlockSpec((1, LANES, tm), lambda i: (i // nt, 0, i % nt))],
        out_shape=[jax.ShapeDtypeStruct((C, N_HEADS, R, HEAD_DIM), BF16),
                   jax.ShapeDtypeStruct((C, LANES, R), F32)],
        compiler_params=_cparams(("parallel",)),
        name="q_proj")(x2, g, w_q, wgt_t, bgt, cosf, sinf)


def _oproj_kernel(o_ref, w_ref, x_ref, out_ref):
    o = o_ref[...]
    for c in range(0, x_ref.shape[1], COL_CHUNK):
        out_ref[:, c:c + COL_CHUNK] = x_ref[:, c:c + COL_CHUNK] + _dot(o, w_ref[:, c:c + COL_CHUNK])


def oproj_call(o2, w_o, x2, li, tm):
    m, d = x2.shape
    return pl.pallas_call(
        _oproj_kernel, grid=(m // tm,),
        in_specs=[pl.BlockSpec((tm, d), lambda i: (i, 0)),
                  _resident((None, d, d), lambda i: (li, 0, 0)),
                  pl.BlockSpec((tm, d), lambda i: (i, 0))],
        out_specs=pl.BlockSpec((tm, d), lambda i: (i, 0)),
        out_shape=jax.ShapeDtypeStruct(x2.shape, F32),
        compiler_params=_cparams(("parallel",)),
        name="o_proj")(o2, w_o, x2)


def _compress_kernel(*refs, npage, rope, nslab, upb, paged, transpose_out):
    if paged:
        refs = refs[1:]
    pages = refs[:npage]
    w1_ref, b1_ref, w2_ref, cos_ref, sin_ref, o_ref, p_s = refs[npage:]
    i = pl.program_id(1)
    upp = upb // npage
    unit_rows = CMP_STRIDE * N_KV_HEADS
    swapped = [jnp.swapaxes(pg[0].reshape(upp, unit_rows, HEAD_DIM), 0, 1) for pg in pages]
    for g in range(N_KV_HEADS):
        cols = []
        for s in range(CMP_STRIDE):
            parts = [sw[s * N_KV_HEADS + g] for sw in swapped]
            cols.append(parts[0] if npage == 1 else jnp.concatenate(parts, axis=0))
        lhs = jnp.concatenate(cols, axis=1).astype(BF16)
        p_s[g, pl.ds(pl.multiple_of(i * upb, upb), upb), :] = _dot(lhs, w1_ref[...])

    @pl.when(i == nslab - 1)
    def _():
        nu = nslab * upb
        for g in range(N_KV_HEADS):
            p_s[g, nu:nu + 8, :] = jnp.zeros((8, 2 * HEAD_DIM), F32)
            h = b1_ref[...] + p_s[g, 0:nu, 0:HEAD_DIM] + p_s[g, 1:nu + 1, HEAD_DIM:2 * HEAD_DIM]
            a = jax.nn.gelu(h).astype(BF16)
            if transpose_out:
                o_ref[0, g] = _dot_nt(w2_ref[...], a).astype(BF16)
            else:
                c = _dot(a, w2_ref[...])
                if rope:
                    c = _rope(c, cos_ref[...], sin_ref[...])
                o_ref[0, g] = c.astype(BF16)


def compress_call(src, page_table, w1r, b1, w2, cosf, sinf, rope, nbatch, nslab, slab, transpose_out=False):
    upb = slab // CMP_STRIDE
    nu = nslab * upb
    paged = page_table is not None
    npage = slab // PAGE_SIZE if paged else 1
    const = lambda b, i, *_: (0, 0)
    if paged:
        page_specs = [pl.BlockSpec((1, PAGE_SIZE * N_KV_HEADS, HEAD_DIM),
                                   functools.partial(lambda b, i, pt, r: (pt[b, i * npage + r], 0, 0), r=r))
                      for r in range(npage)]
    else:
        page_specs = [pl.BlockSpec((1, slab * N_KV_HEADS, HEAD_DIM), lambda b, i: (b, i, 0))]
    in_specs = page_specs + [pl.BlockSpec((CMP_STRIDE * HEAD_DIM, 2 * HEAD_DIM), const),
                             pl.BlockSpec((1, HEAD_DIM), const),
                             pl.BlockSpec((HEAD_DIM, HEAD_DIM), const),
                             pl.BlockSpec((nu, HEAD_DIM), const),
                             pl.BlockSpec((nu, HEAD_DIM), const)]
    o_dims = (HEAD_DIM, nu) if transpose_out else (nu, HEAD_DIM)
    out_specs = pl.BlockSpec((1, N_KV_HEADS) + o_dims, lambda b, i, *_: (b, 0, 0, 0))
    kern = functools.partial(_compress_kernel, npage=npage, rope=rope, nslab=nslab, upb=upb,
                             paged=paged, transpose_out=transpose_out)
    scratch = [pltpu.VMEM((N_KV_HEADS, nu + 8, 2 * HEAD_DIM), F32)]
    out_shape = jax.ShapeDtypeStruct((nbatch, N_KV_HEADS) + o_dims, BF16)
    args = [src] * npage + [w1r, b1, w2, cosf, sinf]
    if paged:
        gs = pltpu.PrefetchScalarGridSpec(num_scalar_prefetch=1, grid=(nbatch, nslab), in_specs=in_specs,
                                          out_specs=out_specs, scratch_shapes=scratch)
        return pl.pallas_call(kern, grid_spec=gs, out_shape=out_shape,
                              compiler_params=_cparams(("parallel", "arbitrary")),
                              name="compress_paged")(page_table, *args)
    return pl.pallas_call(kern, grid=(nbatch, nslab), in_specs=in_specs, out_specs=out_specs,
                          out_shape=out_shape, scratch_shapes=scratch,
                          compiler_params=_cparams(("parallel", "arbitrary")),
                          name="compress_rows")(*args)


def cmp_to_slc_matrix(n_cmp, n_slc):
    units = np.arange(n_cmp)[:, None] + np.arange(CMP_PARTS)[None, :]
    sblk = (units * CMP_STRIDE) // SLC_BLOCK
    m = np.zeros((n_cmp, n_slc), np.float32)
    np.add.at(m, (np.repeat(np.arange(n_cmp), CMP_PARTS), sblk.reshape(-1)), 1.0)
    return m


def _split_bf16(x):
    hi = x.astype(BF16)
    return hi, (x - hi.astype(F32)).astype(BF16)


def _softmax_rows(s, mask):
    s = jnp.where(mask, s, NEG_INF)
    m = jnp.max(s, axis=-1, keepdims=True)
    p = jnp.where(mask, jnp.exp2(s - m), 0.0)
    return p / jnp.maximum(jnp.sum(p, axis=-1, keepdims=True), 1e-30)


def _nsa_prompt_kernel(q_ref, kc_ref, vct_ref, sk_ref, svt_ref, wk_ref, wvt_ref, gate_ref, mt_ref, o_ref,
                       *, tq, n_cmp, n_slc):
    g = pl.program_id(1)
    qi = pl.program_id(2)
    t0 = qi * tq
    rows = GQA_REP * tq
    q = q_ref[0].reshape(rows, HEAD_DIM)

    def tile4(a):
        return jnp.concatenate([a] * GQA_REP, axis=1)

    ncp = kc_ref.shape[2]
    n_id = lax.broadcasted_iota(jnp.int32, (ncp, tq), 0)
    tq_pos = t0 + lax.broadcasted_iota(jnp.int32, (ncp, tq), 1)
    cmask = tile4((n_id * CMP_STRIDE + (CMP_BLOCK - 1) <= tq_pos) & (n_id < n_cmp))
    s = jnp.where(cmask, _dot_nt(kc_ref[0, 0], q), NEG_INF)
    m = jnp.max(s, axis=0, keepdims=True)
    p = jnp.where(cmask, jnp.exp2(s - m), 0.0)
    p = p / jnp.maximum(jnp.sum(p, axis=0, keepdims=True), 1e-30)
    o_cmp = _dot(vct_ref[0, 0], p.astype(BF16))
    p_grp = p[:, 0:tq] + p[:, tq:2 * tq] + p[:, 2 * tq:3 * tq] + p[:, 3 * tq:4 * tq]

    p_hi, p_lo = _split_bf16(p_grp)
    nsp = 8 * (-(-n_slc // 8))
    sc =(_dot(mt_ref[...], p_hi) + _dot(mt_ref[...], p_lo))[0:nsp]
    j_id = lax.broadcasted_iota(jnp.int32, (nsp, tq), 0)
    tpos = t0 + lax.broadcasted_iota(jnp.int32, (nsp, tq), 1)
    cur = tpos // SLC_BLOCK
    valid = (j_id * SLC_BLOCK <= tpos) & (j_id < n_slc)
    forced = (j_id == 0) | (j_id == cur) | (j_id == cur - 1)
    sc = jnp.where(valid, jnp.where(forced, FORCED_SCORE, sc), NEG_INF)
    cnt = jnp.zeros((nsp, tq), F32)
    for i in range(n_slc):
        ri = sc[i:i + 1, :]
        beats = (ri > sc) | ((ri == sc) & (j_id > i))
        cnt = cnt + jnp.where(beats, 1.0, 0.0)
    sel_bias = jnp.where((cnt < float(min(N_SELECT, n_slc))) & (sc > 0.5 * NEG_INF), 0.0, NEG_INF)
    if nsp < LANES:
        sel_bias = jnp.concatenate([sel_bias, jnp.zeros((LANES - nsp, tq), F32)], axis=0)
    sel_rows = sel_bias.T.astype(BF16)
    q_slc = jnp.concatenate([q, jnp.concatenate([sel_rows] * GQA_REP, axis=0)], axis=1)

    kp_off = lax.broadcasted_iota(jnp.int32, (tq, tq), 0)
    tq_off = lax.broadcasted_iota(jnp.int32, (tq, tq), 1)

    def slc_bias(k0, diag):
        return jnp.where(kp_off <= tq_off, 0.0, NEG_INF) if diag else None

    def win_bias(k0, diag):
        dist = (t0 + tq_off) - (k0 + kp_off)
        return jnp.where((dist >= 0) & (dist < WINDOW), 0.0, NEG_INF)

    def flash(k_ref, vt_ref, qmat, nblk, bias_fn):
        def update(its, carry, diag):
            m, l, acc = carry
            k0s = [pl.multiple_of((qi - it) * tq, tq) for it in its]
            sbs = []
            for n, k0 in enumerate(k0s):
                sb = _dot_nt(k_ref[0, 0, pl.ds(k0, tq), :], qmat)
                bias = bias_fn(k0, diag and n == 0)
                sbs.append(sb if bias is None else sb + tile4(bias))
            m_new = m
            for sb in sbs:
                m_new = jnp.maximum(m_new, jnp.max(sb, axis=0, keepdims=True))
            alpha = jnp.exp2(m - m_new)
            l = alpha * l
            acc = alpha * acc
            for k0, sb in zip(k0s, sbs):
                pb = jnp.exp2(sb - m_new)
                l = l + jnp.sum(pb, axis=0, keepdims=True)
                acc = acc + _dot(vt_ref[0, 0, :, pl.ds(k0, tq)], pb.astype(BF16))
            return m_new, l, acc

        init = (jnp.full((1, rows), NEG_INF, F32), jnp.zeros((1, rows), F32), jnp.zeros((HEAD_DIM, rows), F32))
        carry = lax.cond(nblk >= 2, lambda c: update([0, 1], c, True), lambda c: update([0], c, True), init)
        rest = jnp.maximum(nblk - 2, 0)
        carry = lax.fori_loop(0, rest // 2, lambda j, c: update([2 + 2 * j, 3 + 2 * j], c, False), carry)
        carry = lax.cond(rest % 2 == 1, lambda c: update([nblk - 1], c, False), lambda c: c, carry)
        _, l, acc = carry
        return acc / jnp.maximum(l, 1e-30)

    o_slc = flash(sk_ref, svt_ref, q_slc, qi + 1, slc_bias)
    o_win = flash(wk_ref, wvt_ref, q, jnp.minimum(qi, WINDOW // tq) + 1, win_bias)

    for r in range(GQA_REP):
        cs = slice(r * tq, (r + 1) * tq)
        c0 = g * GQA_REP + r
        o = (gate_ref[0, pl.ds(c0, 1), :] * o_cmp[:, cs] + gate_ref[0, pl.ds(N_HEADS + c0, 1), :] * o_slc[:, cs]
             + gate_ref[0, pl.ds(2 * N_HEADS + c0, 1), :] * o_win[:, cs])
        o_ref[0, :, r * HEAD_DIM:(r + 1) * HEAD_DIM] = o.T.astype(BF16)


def nsa_prompt_call(q, k_cmp, v_cmp_t, sk, sv_t, wk, wv_t, gates_t, tq):
    B, _, T, _ = q.shape
    ncp = k_cmp.shape[2]
    n_cmp = T // CMP_STRIDE - CMP_PARTS + 1
    n_slc = -(-T // SLC_BLOCK)
    mt = np.zeros((LANES, ncp), np.float32)
    mt[:n_slc, :n_cmp] = cmp_to_slc_matrix(n_cmp, n_slc).T
    mt = jnp.asarray(mt, BF16)
    sk_spec = pl.BlockSpec((1, 1, T, 2 * HEAD_DIM), lambda b, g, i: (b, g, 0, 0))
    k_spec = pl.BlockSpec((1, 1, T, HEAD_DIM), lambda b, g, i: (b, g, 0, 0))
    vt_spec = pl.BlockSpec((1, 1, HEAD_DIM, T), lambda b, g, i: (b, g, 0, 0))
    kern = functools.partial(_nsa_prompt_kernel, tq=tq, n_cmp=n_cmp, n_slc=n_slc)
    return pl.pallas_call(
        kern, grid=(B, N_KV_HEADS, T // tq),
        in_specs=[pl.BlockSpec((1, GQA_REP, tq, HEAD_DIM), lambda b, g, i: (b, g, i, 0)),
                  pl.BlockSpec((1, 1, ncp, HEAD_DIM), lambda b, g, i: (b, g, 0, 0)),
                  pl.BlockSpec((1, 1, HEAD_DIM, ncp), lambda b, g, i: (b, g, 0, 0)),
                  sk_spec, vt_spec, k_spec, vt_spec,
                  pl.BlockSpec((1, LANES, tq), lambda b, g, i: (b, 0, i)),
                  pl.BlockSpec((LANES, ncp), lambda b, g, i: (0, 0))],
        out_specs=pl.BlockSpec((1, tq, GQA_REP * HEAD_DIM), lambda b, g, i: (b, i, g)),
        out_shape=jax.ShapeDtypeStruct((B, T, N_HEADS * HEAD_DIM), BF16),
        compiler_params=_cparams(("parallel", "parallel", "arbitrary")),
        name="nsa_prompt")(q, k_cmp, v_cmp_t, sk, sv_t, wk, wv_t, gates_t, mt)


def _nsa_sample_select_kernel(q_ref, kc_ref, vc_ref, m_ref, ocmp_ref, idx_ref, ok_ref, *, nb, n_cmp, n_slc, qpos):
    rows = GQA_REP * nb
    q = q_ref[0].astype(F32).reshape(rows, HEAD_DIM).astype(BF16)
    ncp = kc_ref.shape[2]
    row_b = lax.broadcasted_iota(jnp.int32, (rows, 1), 0) % nb
    n_id = lax.broadcasted_iota(jnp.int32, (rows, ncp), 1)
    cmask = (n_id * CMP_STRIDE + (CMP_BLOCK - 1) <= qpos) & (n_id < n_cmp)
    o_acc = jnp.zeros((rows, HEAD_DIM), F32)
    pg = jnp.zeros((nb, ncp), F32)
    b_id = lax.broadcasted_iota(jnp.int32, (nb, 1), 0)
    for b in range(nb):
        p = _softmax_rows(_dot_nt(q, kc_ref[b, 0]), cmask)
        o_b = _dot(p.astype(BF16), vc_ref[b, 0])
        o_acc = o_acc + jnp.where(row_b == b, o_b, 0.0)
        p_sum = jnp.sum(p.reshape(GQA_REP, nb, ncp), axis=0)
        pg = pg + jnp.where(b_id == b, p_sum, 0.0)
    ocmp_ref[0] = o_acc.reshape(GQA_REP, nb, HEAD_DIM)

    p_hi, p_lo = _split_bf16(pg)
    sc = _dot(p_hi, m_ref[...]) + _dot(p_lo, m_ref[...])
    nsp = sc.shape[1]
    j_id = lax.broadcasted_iota(jnp.int32, (nb, nsp), 1)
    cur = qpos // SLC_BLOCK
    valid = (j_id * SLC_BLOCK <= qpos) & (j_id < n_slc)
    forced = (j_id == 0) | (j_id == cur) | (j_id == cur - 1)
    sc = jnp.where(valid, jnp.where(forced, FORCED_SCORE, sc), NEG_INF)
    lane = lax.broadcasted_iota(jnp.int32, (nb, LANES), 1)
    idx = jnp.zeros((nb, LANES), jnp.int32)
    okv = jnp.zeros((nb, LANES), jnp.int32)
    for i in range(min(N_SELECT, n_slc)):
        m = jnp.max(sc, axis=-1, keepdims=True)
        first = jnp.min(jnp.where(sc == m, j_id.astype(F32), float(nsp)), axis=-1, keepdims=True).astype(jnp.int32)
        idx = jnp.where(lane == i, first, idx)
        okv = jnp.where(lane == i, jnp.where(m > 0.5 * NEG_INF, 1, 0), okv)
        sc = jnp.where(j_id == first, -3.0e38, sc)
    idx_ref[0] = idx
    ok_ref[0] = okv


def nsa_sample_select_call(q, k_cmp, v_cmp, qpos):
    nb = q.shape[2]
    ncp = k_cmp.shape[2]
    n_cmp = (qpos + 1) // CMP_STRIDE - CMP_PARTS + 1
    n_slc = -(-(qpos + 1) // SLC_BLOCK)
    nsp = LANES * (-(-n_slc // LANES))
    m = np.zeros((ncp, nsp), np.float32)
    m[:n_cmp, :n_slc] = cmp_to_slc_matrix(n_cmp, n_slc)
    m = jnp.asarray(m, BF16)
    cmp_spec = pl.BlockSpec((nb, 1, ncp, HEAD_DIM), lambda g: (0, g, 0, 0))
    kern = functools.partial(_nsa_sample_select_kernel, nb=nb, n_cmp=n_cmp, n_slc=n_slc, qpos=qpos)
    return pl.pallas_call(
        kern, grid=(N_KV_HEADS,),
        in_specs=[pl.BlockSpec((1, GQA_REP, nb, HEAD_DIM), lambda g: (0, g, 0, 0)),
                  cmp_spec, cmp_spec,
                  pl.BlockSpec((ncp, nsp), lambda g: (0, 0))],
        out_specs=[pl.BlockSpec((1, GQA_REP, nb, HEAD_DIM), lambda g: (g, 0, 0, 0)),
                   pl.BlockSpec((1, nb, LANES), lambda g: (g, 0, 0)),
                   pl.BlockSpec((1, nb, LANES), lambda g: (g, 0, 0))],
        out_shape=[jax.ShapeDtypeStruct((N_KV_HEADS, GQA_REP, nb, HEAD_DIM), F32),
                   jax.ShapeDtypeStruct((N_KV_HEADS, nb, LANES), jnp.int32),
                   jax.ShapeDtypeStruct((N_KV_HEADS, nb, LANES), jnp.int32)],
        compiler_params=_cparams(("parallel",)),
        name="nsa_sample_select")(q, k_cmp, v_cmp, m)


def _nsa_sample_attend_kernel(idx_ref, ok_ref, pt_ref, q_ref, *rest, nb, nsel, n_past_blk, qpos):
    kblks, vblks = rest[:nsel], rest[nsel:2 * nsel]
    (wk_ref, wv_ref, skn_ref, svn_ref, wkn_ref, wvn_ref, ocmp_ref, gate_ref, o_ref) = rest[2 * nsel:]
    g = pl.program_id(0)
    b = pl.program_id(1)
    rows = GQA_REP * nb
    qf = q_ref[0].astype(F32).reshape(rows, HEAD_DIM)
    q = qf.astype(BF16)
    row_b = lax.broadcasted_iota(jnp.int32, (rows, 1), 0) % nb

    @pl.when(b == 0)
    def _():
        o_ref[...] = jnp.zeros_like(o_ref)

    head_rows = pl.ds(g, SLC_BLOCK, stride=N_KV_HEADS)
    kcat = jnp.concatenate([kb[0, head_rows, :] for kb in kblks], axis=0).astype(BF16)
    vcat = jnp.concatenate([vb[0, head_rows, :] for vb in vblks], axis=0).astype(BF16)
    nk = nsel * SLC_BLOCK
    slot = lax.broadcasted_iota(jnp.int32, (1, nk), 1) // SLC_BLOCK
    within = lax.broadcasted_iota(jnp.int32, (1, nk), 1) % SLC_BLOCK
    kpos = jnp.zeros((1, nk), jnp.int32)
    kok = jnp.zeros((1, nk), jnp.int32)
    new_ok = jnp.zeros((1, 1), jnp.int32)
    base = (g * nb + b) * LANES
    for s in range(nsel):
        bi = idx_ref[base + s]
        oks = ok_ref[base + s]
        kpos = jnp.where(slot == s, bi * SLC_BLOCK + within, kpos)
        kok = jnp.where(slot == s, jnp.where(bi < n_past_blk, oks, 0), kok)
        new_ok = jnp.maximum(new_ok, jnp.where(bi * SLC_BLOCK <= qpos, jnp.where(bi >= n_past_blk, oks, 0), 0))
    mask = (kok > 0) & (kpos <= qpos)
    s_past = jnp.where(mask, _dot_nt(q, kcat), NEG_INF)
    has_new = new_ok > 0
    s_new = jnp.where(has_new, jnp.sum(qf * skn_ref[0], axis=-1, keepdims=True), NEG_INF)
    m = jnp.maximum(jnp.max(s_past, axis=-1, keepdims=True), s_new)
    p_past = jnp.where(mask, jnp.exp2(s_past - m), 0.0)
    p_new = jnp.where(has_new, jnp.exp2(s_new - m), 0.0)
    den = jnp.maximum(jnp.sum(p_past, axis=-1, keepdims=True) + p_new, 1e-30)
    o_slc = (_dot(p_past.astype(BF16), vcat)
             + p_new * svn_ref[0]) / den

    nw = wk_ref.shape[1] // N_KV_HEADS
    win_rows = pl.ds(g, nw, stride=N_KV_HEADS)
    wpos = qpos - nw + lax.broadcasted_iota(jnp.int32, (1, nw), 1)
    wdist = qpos - wpos
    wmask = (wpos >= 0) & (wdist >= 0) & (wdist < WINDOW)
    s_w = jnp.where(wmask, _dot_nt(q, wk_ref[0, win_rows, :].astype(BF16)), NEG_INF)
    s_wn = jnp.sum(qf * wkn_ref[0], axis=-1, keepdims=True)
    mw = jnp.maximum(jnp.max(s_w, axis=-1, keepdims=True), s_wn)
    p_w = jnp.where(wmask, jnp.exp2(s_w - mw), 0.0)
    p_wn = jnp.exp2(s_wn - mw)
    den_w = jnp.maximum(jnp.sum(p_w, axis=-1, keepdims=True) + p_wn, 1e-30)
    o_win = (_dot(p_w.astype(BF16), wv_ref[0, win_rows, :].astype(BF16))
             + p_wn * wvn_ref[0]) / den_w

    gates = jnp.concatenate([gate_ref[...]] * GQA_REP, axis=0)
    lane = lax.broadcasted_iota(jnp.int32, (rows, LANES), 1)
    row_r = lax.broadcasted_iota(jnp.int32, (rows, 1), 0) // nb

    def gate_col(branch):
        c = branch * N_HEADS + g * GQA_REP + row_r
        return jnp.sum(jnp.where(lane == c, gates, 0.0), axis=-1, keepdims=True)

    o = (gate_col(0) * ocmp_ref[0].reshape(rows, HEAD_DIM) + gate_col(1) * o_slc + gate_col(2) * o_win)
    o_ref[0] += jnp.where(row_b == b, o, 0.0).reshape(GQA_REP, nb, HEAD_DIM)


def nsa_sample_attend_call(idx, ok, page_table, q, pool_k, pool_v, win_k, win_v, sk_new, sv_new, wk_new, wv_new,
                           o_cmp, gates, qpos):
    nb = q.shape[2]
    nsel = min(N_SELECT, -(-(qpos + 1) // SLC_BLOCK))
    n_past_blk = page_table.shape[1] * PAGE_SIZE // SLC_BLOCK
    half = PAGE_SIZE // SLC_BLOCK

    def blk_map(g, b, idx_r, ok_r, pt_r, s):
        bi = jnp.minimum(idx_r[(g * nb + b) * LANES + s], n_past_blk - 1)
        return (pt_r[b, bi // half], bi % half, 0)

    blk_specs = [pl.BlockSpec((1, SLC_BLOCK * N_KV_HEADS, HEAD_DIM), functools.partial(blk_map, s=s))
                 for s in range(nsel)]
    win_spec = pl.BlockSpec((1, win_k.shape[1], HEAD_DIM), lambda g, b, *_: (b, 0, 0))
    new_spec = pl.BlockSpec((1, 1, HEAD_DIM), lambda g, b, *_: (b * N_KV_HEADS + g, 0, 0))
    qo_spec = pl.BlockSpec((1, GQA_REP, nb, HEAD_DIM), lambda g, b, *_: (0, g, 0, 0))
    in_specs = ([qo_spec] + blk_specs + blk_specs + [win_spec, win_spec, new_spec, new_spec, new_spec, new_spec,
                pl.BlockSpec((1, GQA_REP, nb, HEAD_DIM), lambda g, b, *_: (g, 0, 0, 0)),
                pl.BlockSpec((nb, LANES), lambda g, b, *_: (0, 0))])
    gs = pltpu.PrefetchScalarGridSpec(
        num_scalar_prefetch=3, grid=(N_KV_HEADS, nb), in_specs=in_specs,
        out_specs=pl.BlockSpec((1, GQA_REP, nb, HEAD_DIM), lambda g, b, *_: (g, 0, 0, 0)))
    kern = functools.partial(_nsa_sample_attend_kernel, nb=nb, nsel=nsel, n_past_blk=n_past_blk, qpos=qpos)
    return pl.pallas_call(
        kern, grid_spec=gs,
        out_shape=jax.ShapeDtypeStruct((N_KV_HEADS, GQA_REP, nb, HEAD_DIM), F32),
        compiler_params=_cparams(("parallel", "arbitrary")),
        name="nsa_sample_attend")(idx.reshape(-1), ok.reshape(-1), page_table, q,
                                  *([pool_k] * nsel), *([pool_v] * nsel), win_k, win_v,
                                  sk_new, sv_new, wk_new, wv_new, o_cmp, gates)


def _rope_tables(pos):
    half = HEAD_DIM // 2
    inv_freq = ROPE_THETA ** (-jnp.arange(half, dtype=F32) / half)
    ang = pos.astype(F32)[:, None] * inv_freq[None, :]
    cos, sin = jnp.cos(ang), jnp.sin(ang)
    return jnp.concatenate([cos, cos], axis=-1), jnp.concatenate([-sin, sin], axis=-1)


def _trunk(x2, p3, pos, h0r, h0i, w, lay, kv_ctx):
    nb, T, tc, C, R, tm = lay
    depth = w["norm_mix_g"].shape[0]
    n_a = w["ssm_a_re"].shape[0]
    d = x2.shape[1]
    cosf, sinf = _rope_tables(pos)
    ssm_re, ssm_im = [], []
    kv = None
    for i in range(depth):
        if i < n_a:
            prm = s5_params(w["ssm_a_re"][i], w["ssm_a_im"][i], w["ssm_log_dt"][i], w["ssm_b_re"][i],
                            w["ssm_b_im"][i], w["ssm_c_re"][i], w["ssm_c_im"][i], w["ssm_d"][i], nb)
            g3, hr, hi = s5_mixer(x2.reshape(nb, T, d), w["norm_mix_g"][i][None], prm, h0r[i], h0i[i], tc)
            ssm_re.append(rows_to_state(hr, nb))
            ssm_im.append(rows_to_state(hi, nb))
            x2 = glu_call(g3.reshape(nb * T, d), w["ssm_w_glu"], w["ssm_b_glu"], x2, i, tm)
        else:
            j = i - n_a
            q, gates_t = q_call(x2, w["norm_mix_g3"], w["nsa_w_q"], w["nsa_wgt_t"][j], w["nsa_bgt"][j],
                                cosf, sinf, i, j, C, R, tm)
            o2 = kv_ctx["attend"](q, gates_t, kv)
            x2 = oproj_call(o2, w["nsa_w_o"], x2, j, tm)
        x2 = kv_ctx["ffn"](i, x2)
        x2 = ple_call(x2, w["norm_ple_g3"], p3, w["ple_w_gate"], w["ple_w_proj"], w["norm_final_g"][None], i, tm,
                      final_norm=(i == depth - 1))
        if i == n_a - 1:
            kv = kv_ctx["build"](x2, cosf, sinf)
    return x2, jnp.stack(ssm_re), jnp.stack(ssm_im), kv["new"]


def _cmp_weights(w1, b1, w2):
    w1p = w1.reshape(CMP_PARTS, CMP_STRIDE * HEAD_DIM, HEAD_DIM)
    return jnp.concatenate([w1p[0], w1p[1]], axis=1).astype(BF16), b1[None], w2.astype(BF16)


def kernel(x_prompt, x_sample, state_ssm_re, state_ssm_im, cache_cmp_k, cache_cmp_v, cache_slc_k, cache_slc_v, state_win_k, state_win_v, page_table, p_prompt, p_sample, norm_mix_g, norm_ffn_g, norm_ple_g, norm_kv_g, norm_final_g, ssm_a_re, ssm_a_im, ssm_log_dt, ssm_b_re, ssm_b_im, ssm_c_re, ssm_c_im, ssm_d, ssm_w_glu, ssm_b_glu, kv_w, cmp_k_w1, cmp_k_b1, cmp_k_w2, cmp_v_w1, cmp_v_b1, cmp_v_w2, nsa_w_q, nsa_w_gate, nsa_b_gate, nsa_w_o, ffn_w_gate, ffn_w_up, ffn_w_down, ple_w_gate, ple_w_proj):
    B, T, D = x_prompt.shape
    nbs, ts, _ = x_sample.shape
    assert ts == 1
    n_a = ssm_a_re.shape[0]
    past = page_table.shape[1] * PAGE_SIZE
    gate_pad = LANES - nsa_w_gate.shape[-1]
    w = dict(norm_mix_g=norm_mix_g, norm_mix_g3=norm_mix_g[:, None], norm_ffn_g3=norm_ffn_g[:, None],
             norm_ple_g3=norm_ple_g[:, None], norm_final_g=norm_final_g,
             ssm_a_re=ssm_a_re, ssm_a_im=ssm_a_im, ssm_log_dt=ssm_log_dt, ssm_b_re=ssm_b_re, ssm_b_im=ssm_b_im,
             ssm_c_re=ssm_c_re, ssm_c_im=ssm_c_im, ssm_d=ssm_d,
             ssm_w_glu=ssm_w_glu.astype(BF16), ssm_b_glu=ssm_b_glu[:, None],
             nsa_w_q=nsa_w_q.astype(BF16),
             nsa_wgt_t=jnp.pad(jnp.swapaxes(nsa_w_gate, 1, 2), ((0, 0), (0, gate_pad), (0, 0))).astype(BF16),
             nsa_bgt=jnp.pad(nsa_b_gate, ((0, 0), (0, gate_pad)))[..., None],
             nsa_w_o=nsa_w_o.astype(BF16),
             ple_w_gate=ple_w_gate.astype(BF16), ple_w_proj=ple_w_proj.astype(BF16))
    kvw = kv_w.astype(BF16)
    kvg = norm_kv_g[None]
    ck_w = _cmp_weights(cmp_k_w1, cmp_k_b1, cmp_k_w2)
    cv_w = _cmp_weights(cmp_v_w1, cmp_v_b1, cmp_v_w2)

    tm_p = min(ROW_TILE, T)
    tq = min(Q_TILE, T)
    pos_p = jnp.arange(T, dtype=jnp.int32)

    def build_prompt(x2, cosf, sinf):
        ck, cv, sk, sv, wk, wv, skh, wkh, svt, wvt = kv_call(x2, kvg, kvw, cosf, sinf, B, T, tm_p, True)
        slab = min(SLAB_ROWS, T)
        nslab = T // slab
        nu = T // CMP_STRIDE
        ccos, csin = _rope_tables(jnp.arange(nu, dtype=jnp.int32) * CMP_STRIDE + (CMP_BLOCK - 1))
        rows3 = lambda a: a.reshape(B, T * N_KV_HEADS, HEAD_DIM)
        k_cmp = compress_call(rows3(ck), None, *ck_w, ccos, csin, True, B, nslab, slab)
        v_cmp_t = compress_call(rows3(cv), None, cv_w[0], cv_w[1], cv_w[2].T, ccos, csin, False, B, nslab, slab,
                                transpose_out=True)
        r4 = lambda a: a.reshape(B, T, N_KV_HEADS, HEAD_DIM)
        ck, cv, sk, sv, wk, wv = [r4(a) for a in (ck, cv, sk, sv, wk, wv)]
        wb = min(WINDOW, T)
        new = (ck, cv, sk, sv, wk[:, T - wb:], wv[:, T - wb:])
        return dict(k_cmp=k_cmp, v_cmp_t=v_cmp_t, sk=skh, sv_t=svt, wk=wkh, wv_t=wvt, new=new)

    def attend_prompt(q, gates_t, kv):
        o = nsa_prompt_call(q, kv["k_cmp"], kv["v_cmp_t"], kv["sk"], kv["sv_t"], kv["wk"], kv["wv_t"], gates_t, tq)
        return o.reshape(B * T, N_HEADS * HEAD_DIM)

    ffn_bf16 = {}

    def ffn_sample(i, x2):
        x2, *ffn_bf16[i] = ffn_call(x2, w["norm_ffn_g3"], ffn_w_gate, ffn_w_up, ffn_w_down, i, nbs, FFN_TILE,
                                    emit_bf16=True)
        return x2

    def ffn_prompt(i, x2):
        return ffn_call(x2, w["norm_ffn_g3"], *ffn_bf16[i], i, tm_p, FFN_TILE)[0]

    pool3 = lambda a: a.reshape(a.shape[0], -1, HEAD_DIM)
    qpos = past
    pos_s = jnp.full((nbs,), past, jnp.int32)

    def build_sample(x2, cosf, sinf):
        ck, cv, sk, sv, wk, wv = kv_call(x2, kvg, kvw, cosf, sinf, 1, nbs, nbs, False)
        slab = min(PAGED_SLAB_ROWS, past)
        nslab = past // slab
        nu = past // CMP_STRIDE
        ccos, csin = _rope_tables(jnp.arange(nu, dtype=jnp.int32) * CMP_STRIDE + (CMP_BLOCK - 1))
        k_cmp = compress_call(pool3(cache_cmp_k), page_table, *ck_w, ccos, csin, True, nbs, nslab, slab)
        v_cmp = compress_call(pool3(cache_cmp_v), page_table, *cv_w, ccos, csin, False, nbs, nslab, slab)
        r4 = lambda a: a.reshape(nbs, 1, N_KV_HEADS, HEAD_DIM)
        win_k = jnp.concatenate([state_win_k[:, 1:], r4(wk)], axis=1)
        win_v = jnp.concatenate([state_win_v[:, 1:], r4(wv)], axis=1)
        new = (r4(ck), r4(cv), r4(sk), r4(sv), win_k, win_v)
        return dict(k_cmp=k_cmp, v_cmp=v_cmp, sk=sk, sv=sv, wk=wk, wv=wv, new=new)

    def attend_sample(q, gates_t, kv):
        o_cmp, idx, ok = nsa_sample_select_call(q, kv["k_cmp"], kv["v_cmp"], qpos)
        rows1 = lambda a: a.reshape(nbs * N_KV_HEADS, 1, HEAD_DIM)
        o = nsa_sample_attend_call(idx, ok, page_table, q, pool3(cache_slc_k), pool3(cache_slc_v),
                                   pool3(state_win_k), pool3(state_win_v),
                                   rows1(kv["sk"]), rows1(kv["sv"]), rows1(kv["wk"]), rows1(kv["wv"]),
                                   o_cmp, gates_t[0].T, qpos)
        return jnp.transpose(o, (2, 0, 1, 3)).reshape(nbs, N_HEADS * HEAD_DIM).astype(BF16)

    h0r = jnp.stack([state_to_rows(state_ssm_re[i], nbs) for i in range(n_a)])
    h0i = jnp.stack([state_to_rows(state_ssm_im[i], nbs) for i in range(n_a)])
    y_s, sre_s, sim_s, new_s = _trunk(x_sample.reshape(nbs, D), p_sample.reshape(-1, nbs, p_sample.shape[-1]),
                                      pos_s, h0r, h0i, w, (nbs, 1, 1, 1, nbs, nbs),
                                      dict(build=build_sample, attend=attend_sample, ffn=ffn_sample))
    y_s = y_s.reshape(nbs, 1, D)

    h0 = jnp.zeros((n_a, D // PAIR_CH, 2 * B, PAIR_ST), F32)
    y_p, sre_p, sim_p, new_p = _trunk(x_prompt.reshape(B * T, D), p_prompt.reshape(-1, B * T, p_prompt.shape[-1]),
                                      pos_p, h0, h0, w, (B, T, min(SCAN_STEPS, T), B, T, tm_p),
                                      dict(build=build_prompt, attend=attend_prompt, ffn=ffn_prompt))
    y_p = y_p.reshape(B, T, D)

    ck_p, cv_p, sk_p, sv_p, wk_p, wv_p = new_p
    ck_s, cv_s, sk_s, sv_s, wk_s, wv_s = new_s
    return (y_p, y_s, sre_p, sim_p, ck_p, cv_p, sk_p, sv_p, wk_p, wv_p,
            sre_s, sim_s, ck_s, cv_s, sk_s, sv_s, wk_s, wv_s)
```

```python
import functools

import numpy as np
import jax
import jax.numpy as jnp
from jax import lax
from jax.experimental import pallas as pl
from jax.experimental.pallas import tpu as pltpu

F32 = jnp.float32
BF16 = jnp.bfloat16

N_HEADS = 16
N_KV_HEADS = 4
GQA_REP = N_HEADS // N_KV_HEADS
HEAD_DIM = 128
SSM_GROUP = 16
SSM_STATE = 64
CMP_BLOCK = 32
CMP_STRIDE = 16
CMP_PARTS = CMP_BLOCK // CMP_STRIDE
SLC_BLOCK = 64
N_SELECT = 16
WINDOW = 512
PAGE_SIZE = 128
ROPE_THETA = 10000.0
ATTN_SCALE = HEAD_DIM ** -0.5
EXP2_SCALE = ATTN_SCALE * float(np.log2(np.e))
NORM_EPS = 1e-6
NEG_INF = -1e30
FORCED_SCORE = 1e9

LANES = 128
KV_LANES = N_KV_HEADS * HEAD_DIM
PAIR_CH = 2 * LANES
PAIR_ST = (LANES // SSM_GROUP) * SSM_STATE
SLAB_ROWS = 2048
PAGED_SLAB_ROWS = 4096
ROW_TILE = 512
FFN_TILE = 512
Q_TILE = 512
SCAN_STEPS = 128
VMEM_LIMIT = 56 * 1024 * 1024


def _cparams(sem, vmem=VMEM_LIMIT):
    return pltpu.CompilerParams(dimension_semantics=sem, vmem_limit_bytes=vmem)


def _rms(x, g):
    ms = jnp.mean(x * x, axis=-1, keepdims=True)
    return x * lax.rsqrt(ms + NORM_EPS) * g


def _rope(x, cosf, sinf):
    return x * cosf + pltpu.roll(x, HEAD_DIM // 2, axis=1) * sinf


def _dot(a, b):
    return jnp.dot(a, b, preferred_element_type=F32)


def _dot_nt(a, b):
    return lax.dot_general(a, b, (((1,), (1,)), ((), ())), preferred_element_type=F32)


def _s5_kernel(x_ref, gn_ref, wre_ref, wim_ref, cre_ref, cim_ref, ar_ref, ai_ref, d_ref, h0r_ref, h0i_ref,
               g_ref, hr_ref, hi_ref, hn_s, xr_s, xi_s, sr_s, si_s, *, tc, nb, npair):
    @pl.when(pl.program_id(0) == 0)
    def _():
        hr_ref[...] = h0r_ref[...]
        hi_ref[...] = h0i_ref[...]

    r2 = 2 * nb
    rows = tc * r2
    for b in range(nb):
        hn_s[b] = _rms(x_ref[b], gn_ref[...])
    keep = ((lax.broadcasted_iota(jnp.int32, (r2, PAIR_CH), 0) // nb == 1)
            == (lax.broadcasted_iota(jnp.int32, (r2, PAIR_CH), 1) >= LANES))
    lane_hi = lax.broadcasted_iota(jnp.int32, (tc, PAIR_CH), 1) >= LANES

    for k in range(npair):
        cols = slice(k * PAIR_CH, (k + 1) * PAIR_CH)

        if tc > 1:
            hd = jnp.concatenate([hn_s[:, :, cols], hn_s[:, :, cols]], axis=0)
            lhs = jnp.where(keep[None], jnp.swapaxes(hd, 0, 1), 0.0).reshape(rows, PAIR_CH).astype(BF16)
        else:
            v = hn_s[:, 0, cols]
            lhs = jnp.where(keep, jnp.concatenate([v, v], axis=0), 0.0).astype(BF16)
        xr_s[...] = _dot(lhs, wre_ref[k]).reshape(tc, r2, PAIR_ST)
        xi_s[...] = _dot(lhs, wim_ref[k]).reshape(tc, r2, PAIR_ST)
        ar = ar_ref[k]
        ai = ai_ref[k]

        def step(t, carry):
            hr, hi = carry
            nr = ar * hr - ai * hi + xr_s[t]
            ni = ar * hi + ai * hr + xi_s[t]
            sr_s[t] = nr
            si_s[t] = ni
            return nr, ni

        hr, hi = lax.fori_loop(0, tc, step, (hr_ref[k], hi_ref[k]), unroll=min(8, tc))
        hr_ref[k] = hr
        hi_ref[k] = hi
        h_re = sr_s[...].reshape(rows, PAIR_ST).astype(BF16)
        h_im = si_s[...].reshape(rows, PAIR_ST).astype(BF16)
        y = (_dot(h_re, cre_ref[k]) - _dot(h_im, cim_ref[k])).reshape(tc, r2, PAIR_CH)
        if tc > 1:
            y = jnp.swapaxes(y, 0, 1)
            per_row = lambda r: y[r]
        else:
            per_row = lambda r: y[:, r, :]
        for b in range(nb):
            yb = jnp.where(lane_hi, per_row(nb + b), per_row(b))
            yb = yb + d_ref[:, cols] * hn_s[b, :, cols]
            g_ref[b, :, cols] = jax.nn.gelu(yb).astype(BF16)


def s5_mixer(x3, gn, prm, h0r, h0i, tc):
    nb, T, d = x3.shape
    r2 = 2 * nb
    npair = d // PAIR_CH
    full3 = lambda s: (0, 0, 0)
    st_shape = (npair, r2, PAIR_ST)
    kern = functools.partial(_s5_kernel, tc=tc, nb=nb, npair=npair)
    return pl.pallas_call(
        kern, grid=(T // tc,),
        in_specs=[pl.BlockSpec((nb, tc, d), lambda s: (0, s, 0)),
                  pl.BlockSpec((1, d), lambda s: (0, 0)),
                  pl.BlockSpec((npair, PAIR_CH, PAIR_ST), full3),
                  pl.BlockSpec((npair, PAIR_CH, PAIR_ST), full3),
                  pl.BlockSpec((npair, PAIR_ST, PAIR_CH), full3),
                  pl.BlockSpec((npair, PAIR_ST, PAIR_CH), full3),
                  pl.BlockSpec(st_shape, full3),
                  pl.BlockSpec(st_shape, full3),
                  pl.BlockSpec((1, d), lambda s: (0, 0)),
                  pl.BlockSpec(st_shape, full3),
                  pl.BlockSpec(st_shape, full3)],
        out_specs=[pl.BlockSpec((nb, tc, d), lambda s: (0, s, 0)),
                   pl.BlockSpec(st_shape, full3),
                   pl.BlockSpec(st_shape, full3)],
        out_shape=[jax.ShapeDtypeStruct((nb, T, d), BF16),
                   jax.ShapeDtypeStruct(st_shape, F32),
                   jax.ShapeDtypeStruct(st_shape, F32)],
        scratch_shapes=[pltpu.VMEM((nb, tc, d), F32)] + [pltpu.VMEM((tc, r2, PAIR_ST), F32)] * 4,
        compiler_params=_cparams(("arbitrary",)),
        name="s5_mixer")(x3, gn, prm["wre"], prm["wim"], prm["cre"], prm["cim"],
                         prm["ar"], prm["ai"], prm["d"], h0r, h0i)


def s5_params(a_re, a_im, log_dt, b_re, b_im, c_re, c_im, d, nb):
    G, P, H = b_re.shape
    gpc = LANES // H
    nchunk = G // gpc
    dt = jnp.exp(log_dt)[:, None]
    mag = jnp.exp(dt * a_re)
    ab_re, ab_im = mag * jnp.cos(dt * a_im), mag * jnp.sin(dt * a_im)
    den = a_re * a_re + a_im * a_im
    nr, ni = ab_re - 1.0, ab_im
    cf_re = (nr * a_re + ni * a_im) / den
    cf_im = (ni * a_re - nr * a_im) / den
    bb_re = cf_re[..., None] * b_re - cf_im[..., None] * b_im
    bb_im = cf_re[..., None] * b_im + cf_im[..., None] * b_re
    eye = jnp.eye(gpc, dtype=F32)

    def blk_in(bb):
        w = jnp.einsum("cgph,gk->cghkp", bb.reshape(nchunk, gpc, P, H), eye)
        return w.reshape(nchunk // 2, PAIR_CH, gpc * P).astype(BF16)

    def blk_out(cc):
        w = jnp.einsum("cghp,gk->ckpgh", cc.reshape(nchunk, gpc, H, P), eye)
        w = w.reshape(nchunk // 2, 2, gpc * P, LANES)
        return jnp.swapaxes(w, 1, 2).reshape(nchunk // 2, gpc * P, PAIR_CH).astype(BF16)

    def rows(a):
        a = a.reshape(nchunk // 2, 2, 1, gpc * P)
        return jnp.broadcast_to(a, (nchunk // 2, 2, nb, gpc * P)).reshape(nchunk // 2, 2 * nb, gpc * P)

    return dict(wre=blk_in(bb_re), wim=blk_in(bb_im), cre=blk_out(c_re), cim=blk_out(c_im),
                ar=rows(ab_re), ai=rows(ab_im), d=d[None])


def state_to_rows(h, nb):
    npair = h.shape[1] * h.shape[2] // PAIR_ST // 2
    return jnp.transpose(h.reshape(nb, npair, 2, PAIR_ST), (1, 2, 0, 3)).reshape(npair, 2 * nb, PAIR_ST)


def rows_to_state(h, nb):
    npair = h.shape[0]
    h = jnp.transpose(h.reshape(npair, 2, nb, PAIR_ST), (2, 0, 1, 3))
    return h.reshape(nb, npair * 2 * PAIR_ST // SSM_STATE, SSM_STATE)


COL_CHUNK = 512


def _resident(block_shape, index_map):
    return pl.BlockSpec(block_shape, index_map, pipeline_mode=pl.Buffered(1))


def _glu_kernel(a_ref, w_ref, b_ref, x_ref, o_ref):
    a = a_ref[...]
    d = x_ref.shape[1]
    for c in range(0, d, COL_CHUNK):
        z1 = _dot(a, w_ref[:, c:c + COL_CHUNK]) + b_ref[:, c:c + COL_CHUNK]
        z2 = _dot(a, w_ref[:, d + c:d + c + COL_CHUNK]) + b_ref[:, d + c:d + c + COL_CHUNK]
        o_ref[:, c:c + COL_CHUNK] = x_ref[:, c:c + COL_CHUNK] + z1 * jax.nn.sigmoid(z2)


def glu_call(g2, w_glu, b_glu, x2, li, tm):
    m, d = x2.shape
    return pl.pallas_call(
        _glu_kernel, grid=(m // tm,),
        in_specs=[pl.BlockSpec((tm, d), lambda i: (i, 0)),
                  _resident((None, d, 2 * d), lambda i: (li, 0, 0)),
                  _resident((None, 1, 2 * d), lambda i: (li, 0, 0)),
                  pl.BlockSpec((tm, d), lambda i: (i, 0))],
        out_specs=pl.BlockSpec((tm, d), lambda i: (i, 0)),
        out_shape=jax.ShapeDtypeStruct(x2.shape, F32),
        compiler_params=_cparams(("parallel",)),
        name="s5_glu")(g2, w_glu, b_glu, x2)


def _ffn_kernel(x_ref, g_ref, wg_ref, wu_ref, wd_ref, o_ref, *rest, nf, emit_bf16):
    hn_s, acc_s = rest[-2:]
    f = pl.program_id(1)

    @pl.when(f == 0)
    def _():
        hn_s[...] = _rms(x_ref[...], g_ref[...]).astype(BF16)
        acc_s[...] = jnp.zeros_like(acc_s)

    wg, wu, wd = wg_ref[...].astype(BF16), wu_ref[...].astype(BF16), wd_ref[...].astype(BF16)
    if emit_bf16:
        for ref, val in zip(rest[:3], (wg, wu, wd)):
            ref[...] = val
    h = hn_s[...]
    a = _dot(h, wg)
    u = _dot(h, wu)
    act = (a * jax.nn.sigmoid(a) * u).astype(BF16)
    acc_s[...] += _dot(act, wd)

    @pl.when(f == nf - 1)
    def _():
        o_ref[...] = x_ref[...] + acc_s[...]


def ffn_call(x2, g, wg, wu, wd, li, tm, tf, emit_bf16=False):
    m, d = x2.shape
    ff = wg.shape[-1]
    nf = ff // tf
    if emit_bf16:
        assert m == tm
        w_specs = [pl.BlockSpec((None, d, tf), lambda i, f: (li, 0, f)),
                   pl.BlockSpec((None, d, tf), lambda i, f: (li, 0, f)),
                   pl.BlockSpec((None, tf, d), lambda i, f: (li, f, 0))]
    else:
        w_specs = [pl.BlockSpec((d, tf), lambda i, f: (0, f)),
                   pl.BlockSpec((d, tf), lambda i, f: (0, f)),
                   pl.BlockSpec((tf, d), lambda i, f: (f, 0))]
    out_specs = [pl.BlockSpec((tm, d), lambda i, f: (i, 0))]
    out_shape = [jax.ShapeDtypeStruct(x2.shape, F32)]
    if emit_bf16:
        out_specs += [pl.BlockSpec((d, tf), lambda i, f: (0, f)), pl.BlockSpec((d, tf), lambda i, f: (0, f)),
                      pl.BlockSpec((tf, d), lambda i, f: (f, 0))]
        out_shape += [jax.ShapeDtypeStruct((d, ff), BF16)] * 2 + [jax.ShapeDtypeStruct((ff, d), BF16)]
    return pl.pallas_call(
        functools.partial(_ffn_kernel, nf=nf, emit_bf16=emit_bf16), grid=(m // tm, nf),
        in_specs=[pl.BlockSpec((tm, d), lambda i, f: (i, 0)),
                  pl.BlockSpec((None, 1, d), lambda i, f: (li, 0, 0))] + w_specs,
        out_specs=out_specs, out_shape=out_shape,
        scratch_shapes=[pltpu.VMEM((tm, d), BF16), pltpu.VMEM((tm, d), F32)],
        compiler_params=_cparams(("parallel", "arbitrary")),
        name="ffn")(x2, g, wg, wu, wd)


def _ple_kernel(x_ref, g_ref, p_ref, wg_ref, wp_ref, gf_ref, o_ref, *, final_norm):
    hn = _rms(x_ref[...], g_ref[...]).astype(BF16)
    pe = p_ref[...].astype(BF16)
    for c in range(0, x_ref.shape[1], COL_CHUNK):
        cs = slice(c, c + COL_CHUNK)
        gate = jax.nn.sigmoid(_dot(hn, wg_ref[:, cs]))
        o_ref[:, cs] = x_ref[:, cs] + gate * _dot(pe, wp_ref[:, cs])
    if final_norm:
        o_ref[...] = _rms(o_ref[...], gf_ref[...])


def ple_call(x2, g, p3, wg, wp, gf, li, tm, final_norm):
    m, d = x2.shape
    pd = wp.shape[1]
    return pl.pallas_call(
        functools.partial(_ple_kernel, final_norm=final_norm), grid=(m // tm,),
        in_specs=[pl.BlockSpec((tm, d), lambda i: (i, 0)),
                  _resident((None, 1, d), lambda i: (li, 0, 0)),
                  pl.BlockSpec((None, tm, pd), lambda i: (li, i, 0)),
                  _resident((None, d, d), lambda i: (li, 0, 0)),
                  _resident((None, pd, d), lambda i: (li, 0, 0)),
                  _resident((1, d), lambda i: (0, 0))],
        out_specs=pl.BlockSpec((tm, d), lambda i: (i, 0)),
        out_shape=jax.ShapeDtypeStruct(x2.shape, F32),
        compiler_params=_cparams(("parallel",)),
        name="ple")(x2, g, p3, wg, wp, gf)


def _kv_kernel(x_ref, g_ref, w_ref, cos_ref, sin_ref, *outs, heads_out, nt):
    hn = _rms(x_ref[...], g_ref[...]).astype(BF16)
    tm = x_ref.shape[0]
    if heads_out:
        t_row = (pl.program_id(0) % nt) * tm + lax.broadcasted_iota(jnp.int32, (tm, LANES), 0)
        blk_onehot = jnp.where(lax.broadcasted_iota(jnp.int32, (tm, LANES), 1) == t_row // SLC_BLOCK, 1.0, 0.0)
        for h in range(N_KV_HEADS):
            outs[6][0, h, :, HEAD_DIM:] = blk_onehot.astype(BF16)
    for k in range(6):
        z = _dot(hn, w_ref[:, k * KV_LANES:(k + 1) * KV_LANES])
        for h in range(N_KV_HEADS):
            zh = z[:, h * HEAD_DIM:(h + 1) * HEAD_DIM]
            if k in (2, 4):
                zh = _rope(zh, cos_ref[...], sin_ref[...])
            outs[k][pl.ds(h, tm, stride=N_KV_HEADS), :] = zh
            if heads_out and k == 2:
                outs[6][0, h, :, :HEAD_DIM] = zh.astype(BF16)
            if heads_out and k == 4:
                outs[7][0, h] = zh.astype(BF16)
            if heads_out and k in (3, 5):
                outs[8 + (k == 5)][0, h] = zh.T.astype(BF16)


def kv_call(x2, g, kv_w, cosf, sinf, C, R, tm, heads_out):
    m, d = x2.shape
    nt = R // tm
    out_shape = [jax.ShapeDtypeStruct((m * N_KV_HEADS, HEAD_DIM), F32)] * 6
    out_specs = [pl.BlockSpec((tm * N_KV_HEADS, HEAD_DIM), lambda i: (i, 0))] * 6
    if heads_out:
        assert -(-R // SLC_BLOCK) <= LANES
        for width in (2 * HEAD_DIM, HEAD_DIM):
            out_shape += [jax.ShapeDtypeStruct((C, N_KV_HEADS, R, width), BF16)]
            out_specs += [pl.BlockSpec((1, N_KV_HEADS, tm, width), lambda i: (i // nt, 0, i % nt, 0))]
        out_shape += [jax.ShapeDtypeStruct((C, N_KV_HEADS, HEAD_DIM, R), BF16)] * 2
        out_specs += [pl.BlockSpec((1, N_KV_HEADS, HEAD_DIM, tm), lambda i: (i // nt, 0, 0, i % nt))] * 2
    return pl.pallas_call(
        functools.partial(_kv_kernel, heads_out=heads_out, nt=nt), grid=(m // tm,),
        in_specs=[pl.BlockSpec((tm, d), lambda i: (i, 0)),
                  _resident((1, d), lambda i: (0, 0)),
                  _resident((d, 6 * KV_LANES), lambda i: (0, 0)),
                  pl.BlockSpec((tm, HEAD_DIM), lambda i: (i % nt, 0)),
                  pl.BlockSpec((tm, HEAD_DIM), lambda i: (i % nt, 0))],
        out_specs=out_specs, out_shape=out_shape,
        compiler_params=_cparams(("parallel",)),
        name="kv_proj")(x2, g, kv_w, cosf, sinf)


def _q_kernel(x_ref, g_ref, w_ref, wgt_ref, bgt_ref, cos_ref, sin_ref, q_ref, gate_ref):
    hn = _rms(x_ref[...], g_ref[...]).astype(BF16)
    gate_ref[0] = jax.nn.sigmoid(_dot_nt(wgt_ref[...], hn) + bgt_ref[...])
    hpc = COL_CHUNK // HEAD_DIM
    for c in range(N_HEADS // hpc):
        z = _dot(hn, w_ref[:, c * COL_CHUNK:(c + 1) * COL_CHUNK])
        for h in range(hpc):
            zh = _rope(z[:, h * HEAD_DIM:(h + 1) * HEAD_DIM], cos_ref[...], sin_ref[...])
            q_ref[0, c * hpc + h] = (zh * EXP2_SCALE).astype(BF16)


def q_call(x2, g, w_q, wgt_t, bgt, cosf, sinf, li, wi, C, R, tm):
    m, d = x2.shape
    nt = R // tm
    return pl.pallas_call(
        _q_kernel, grid=(m // tm,),
        in_specs=[pl.BlockSpec((tm, d), lambda i: (i, 0)),
                  _resident((None, 1, d), lambda i: (li, 0, 0)),
                  _resident((None, d, N_HEADS * HEAD_DIM), lambda i: (wi, 0, 0)),
                  _resident((LANES, d), lambda i: (0, 0)),
                  _resident((LANES, 1), lambda i: (0, 0)),
                  pl.BlockSpec((tm, HEAD_DIM), lambda i: (i % nt, 0)),
                  pl.BlockSpec((tm, HEAD_DIM), lambda i: (i % nt, 0))],
        out_specs=[pl.BlockSpec((1, N_HEADS, tm, HEAD_DIM), lambda i: (i // nt, 0, i % nt, 0)),
                   pl.BlockSpec((1, LANES, tm), lambda i: (i // nt, 0, i % nt))],
        out_shape=[jax.ShapeDtypeStruct((C, N_HEADS, R, HEAD_DIM), BF16),
                   jax.ShapeDtypeStruct((C, LANES, R), F32)],
        compiler_params=_cparams(("parallel",)),
        name="q_proj")(x2, g, w_q, wgt_t, bgt, cosf, sinf)


def _oproj_kernel(o_ref, w_ref, x_ref, out_ref):
    o = o_ref[...]
    for c in range(0, x_ref.shape[1], COL_CHUNK):
        out_ref[:, c:c + COL_CHUNK] = x_ref[:, c:c + COL_CHUNK] + _dot(o, w_ref[:, c:c + COL_CHUNK])


def oproj_call(o2, w_o, x2, li, tm):
    m, d = x2.shape
    return pl.pallas_call(
        _oproj_kernel, grid=(m // tm,),
        in_specs=[pl.BlockSpec((tm, d), lambda i: (i, 0)),
                  _resident((None, d, d), lambda i: (li, 0, 0)),
                  pl.BlockSpec((tm, d), lambda i: (i, 0))],
        out_specs=pl.BlockSpec((tm, d), lambda i: (i, 0)),
        out_shape=jax.ShapeDtypeStruct(x2.shape, F32),
        compiler_params=_cparams(("parallel",)),
        name="o_proj")(o2, w_o, x2)


def _compress_kernel(*refs, npage, rope, nslab, upb, paged, transpose_out):
    if paged:
        refs = refs[1:]
    pages = refs[:npage]
    w1_ref, b1_ref, w2_ref, cos_ref, sin_ref, o_ref, p_s = refs[npage:]
    i = pl.program_id(1)
    upp = upb // npage
    unit_rows = CMP_STRIDE * N_KV_HEADS
    swapped = [jnp.swapaxes(pg[0].reshape(upp, unit_rows, HEAD_DIM), 0, 1) for pg in pages]
    for g in range(N_KV_HEADS):
        cols = []
        for s in range(CMP_STRIDE):
            parts = [sw[s * N_KV_HEADS + g] for sw in swapped]
            cols.append(parts[0] if npage == 1 else jnp.concatenate(parts, axis=0))
        lhs = jnp.concatenate(cols, axis=1).astype(BF16)
        p_s[g, pl.ds(pl.multiple_of(i * upb, upb), upb), :] = _dot(lhs, w1_ref[...])

    @pl.when(i == nslab - 1)
    def _():
        nu = nslab * upb
        for g in range(N_KV_HEADS):
            p_s[g, nu:nu + 8, :] = jnp.zeros((8, 2 * HEAD_DIM), F32)
            h = b1_ref[...] + p_s[g, 0:nu, 0:HEAD_DIM] + p_s[g, 1:nu + 1, HEAD_DIM:2 * HEAD_DIM]
            a = jax.nn.gelu(h).astype(BF16)
            if transpose_out:
                o_ref[0, g] = _dot_nt(w2_ref[...], a).astype(BF16)
            else:
                c = _dot(a, w2_ref[...])
                if rope:
                    c = _rope(c, cos_ref[...], sin_ref[...])
                o_ref[0, g] = c.astype(BF16)


def compress_call(src, page_table, w1r, b1, w2, cosf, sinf, rope, nbatch, nslab, slab, transpose_out=False):
    upb = slab // CMP_STRIDE
    nu = nslab * upb
    paged = page_table is not None
    npage = slab // PAGE_SIZE if paged else 1
    const = lambda b, i, *_: (0, 0)
    if paged:
        page_specs = [pl.BlockSpec((1, PAGE_SIZE * N_KV_HEADS, HEAD_DIM),
                                   functools.partial(lambda b, i, pt, r: (pt[b, i * npage + r], 0, 0), r=r))
                      for r in range(npage)]
    else:
        page_specs = [pl.BlockSpec((1, slab * N_KV_HEADS, HEAD_DIM), lambda b, i: (b, i, 0))]
    in_specs = page_specs + [pl.BlockSpec((CMP_STRIDE * HEAD_DIM, 2 * HEAD_DIM), const),
                             pl.BlockSpec((1, HEAD_DIM), const),
                             pl.BlockSpec((HEAD_DIM, HEAD_DIM), const),
                             pl.BlockSpec((nu, HEAD_DIM), const),
                             pl.BlockSpec((nu, HEAD_DIM), const)]
    o_dims = (HEAD_DIM, nu) if transpose_out else (nu, HEAD_DIM)
    out_specs = pl.BlockSpec((1, N_KV_HEADS) + o_dims, lambda b, i, *_: (b, 0, 0, 0))
    kern = functools.partial(_compress_kernel, npage=npage, rope=rope, nslab=nslab, upb=upb,
                             paged=paged, transpose_out=transpose_out)
    scratch = [pltpu.VMEM((N_KV_HEADS, nu + 8, 2 * HEAD_DIM), F32)]
    out_shape = jax.ShapeDtypeStruct((nbatch, N_KV_HEADS) + o_dims, BF16)
    args = [src] * npage + [w1r, b1, w2, cosf, sinf]
    if paged:
        gs = pltpu.PrefetchScalarGridSpec(num_scalar_prefetch=1, grid=(nbatch, nslab), in_specs=in_specs,
                                          out_specs=out_specs, scratch_shapes=scratch)
        return pl.pallas_call(kern, grid_spec=gs, out_shape=out_shape,
                              compiler_params=_cparams(("parallel", "arbitrary")),
                              name="compress_paged")(page_table, *args)
    return pl.pallas_call(kern, grid=(nbatch, nslab), in_specs=in_specs, out_specs=out_specs,
                          out_shape=out_shape, scratch_shapes=scratch,
                          compiler_params=_cparams(("parallel", "arbitrary")),
                          name="compress_rows")(*args)


def cmp_to_slc_matrix(n_cmp, n_slc):
    units = np.arange(n_cmp)[:, None] + np.arange(CMP_PARTS)[None, :]
    sblk = (units * CMP_STRIDE) // SLC_BLOCK
    m = np.zeros((n_cmp, n_slc), np.float32)
    np.add.at(m, (np.repeat(np.arange(n_cmp), CMP_PARTS), sblk.reshape(-1)), 1.0)
    return m


def _split_bf16(x):
    hi = x.astype(BF16)
    return hi, (x - hi.astype(F32)).astype(BF16)


def _softmax_rows(s, mask):
    s = jnp.where(mask, s, NEG_INF)
    m = jnp.max(s, axis=-1, keepdims=True)
    p = jnp.where(mask, jnp.exp2(s - m), 0.0)
    return p / jnp.maximum(jnp.sum(p, axis=-1, keepdims=True), 1e-30)


def _nsa_prompt_kernel(q_ref, kc_ref, vct_ref, sk_ref, svt_ref, wk_ref, wvt_ref, gate_ref, mt_ref, o_ref,
                       *, tq, n_cmp, n_slc):
    g = pl.program_id(1)
    qi = pl.program_id(2)
    t0 = qi * tq
    rows = GQA_REP * tq
    q = q_ref[0].reshape(rows, HEAD_DIM)

    def tile4(a):
        return jnp.concatenate([a] * GQA_REP, axis=1)

    ncp = kc_ref.shape[2]
    n_id = lax.broadcasted_iota(jnp.int32, (ncp, tq), 0)
    tq_pos = t0 + lax.broadcasted_iota(jnp.int32, (ncp, tq), 1)
    cmask = tile4((n_id * CMP_STRIDE + (CMP_BLOCK - 1) <= tq_pos) & (n_id < n_cmp))
    s = jnp.where(cmask, _dot_nt(kc_ref[0, 0], q), NEG_INF)
    m = jnp.max(s, axis=0, keepdims=True)
    p = jnp.where(cmask, jnp.exp2(s - m), 0.0)
    p = p / jnp.maximum(jnp.sum(p, axis=0, keepdims=True), 1e-30)
    o_cmp = _dot(vct_ref[0, 0], p.astype(BF16))
    p_grp = p[:, 0:tq] + p[:, tq:2 * tq] + p[:, 2 * tq:3 * tq] + p[:, 3 * tq:4 * tq]

    p_hi, p_lo = _split_bf16(p_grp)
    nsp = 8 * (-(-n_slc // 8))
    sc =(_dot(mt_ref[...], p_hi) + _dot(mt_ref[...], p_lo))[0:nsp]
    j_id = lax.broadcasted_iota(jnp.int32, (nsp, tq), 0)
    tpos = t0 + lax.broadcasted_iota(jnp.int32, (nsp, tq), 1)
    cur = tpos // SLC_BLOCK
    valid = (j_id * SLC_BLOCK <= tpos) & (j_id < n_slc)
    forced = (j_id == 0) | (j_id == cur) | (j_id == cur - 1)
    sc = jnp.where(valid, jnp.where(forced, FORCED_SCORE, sc), NEG_INF)
    cnt = jnp.zeros((nsp, tq), F32)
    for i in range(n_slc):
        ri = sc[i:i + 1, :]
        beats = (ri > sc) | ((ri == sc) & (j_id > i))
        cnt = cnt + jnp.where(beats, 1.0, 0.0)
    sel_bias = jnp.where((cnt < float(min(N_SELECT, n_slc))) & (sc > 0.5 * NEG_INF), 0.0, NEG_INF)
    if nsp < LANES:
        sel_bias = jnp.concatenate([sel_bias, jnp.zeros((LANES - nsp, tq), F32)], axis=0)
    sel_rows = sel_bias.T.astype(BF16)
    q_slc = jnp.concatenate([q, jnp.concatenate([sel_rows] * GQA_REP, axis=0)], axis=1)

    kp_off = lax.broadcasted_iota(jnp.int32, (tq, tq), 0)
    tq_off = lax.broadcasted_iota(jnp.int32, (tq, tq), 1)

    def slc_bias(k0, diag):
        return jnp.where(kp_off <= tq_off, 0.0, NEG_INF) if diag else None

    def win_bias(k0, diag):
        dist = (t0 + tq_off) - (k0 + kp_off)
        return jnp.where((dist >= 0) & (dist < WINDOW), 0.0, NEG_INF)

    def flash(k_ref, vt_ref, qmat, nblk, bias_fn):
        def update(its, carry, diag):
            m, l, acc = carry
            k0s = [pl.multiple_of((qi - it) * tq, tq) for it in its]
            sbs = []
            for n, k0 in enumerate(k0s):
                sb = _dot_nt(k_ref[0, 0, pl.ds(k0, tq), :], qmat)
                bias = bias_fn(k0, diag and n == 0)
                sbs.append(sb if bias is None else sb + tile4(bias))
            m_new = m
            for sb in sbs:
                m_new = jnp.maximum(m_new, jnp.max(sb, axis=0, keepdims=True))
            alpha = jnp.exp2(m - m_new)
            l = alpha * l
            acc = alpha * acc
            for k0, sb in zip(k0s, sbs):
                pb = jnp.exp2(sb - m_new)
                l = l + jnp.sum(pb, axis=0, keepdims=True)
                acc = acc + _dot(vt_ref[0, 0, :, pl.ds(k0, tq)], pb.astype(BF16))
            return m_new, l, acc

        init = (jnp.full((1, rows), NEG_INF, F32), jnp.zeros((1, rows), F32), jnp.zeros((HEAD_DIM, rows), F32))
        carry = lax.cond(nblk >= 2, lambda c: update([0, 1], c, True), lambda c: update([0], c, True), init)
        rest = jnp.maximum(nblk - 2, 0)
        carry = lax.fori_loop(0, rest // 2, lambda j, c: update([2 + 2 * j, 3 + 2 * j], c, False), carry)
        carry = lax.cond(rest % 2 == 1, lambda c: update([nblk - 1], c, False), lambda c: c, carry)
        _, l, acc = carry
        return acc / jnp.maximum(l, 1e-30)

    o_slc = flash(sk_ref, svt_ref, q_slc, qi + 1, slc_bias)
    o_win = flash(wk_ref, wvt_ref, q, jnp.minimum(qi, WINDOW // tq) + 1, win_bias)

    for r in range(GQA_REP):
        cs = slice(r * tq, (r + 1) * tq)
        c0 = g * GQA_REP + r
        o = (gate_ref[0, pl.ds(c0, 1), :] * o_cmp[:, cs] + gate_ref[0, pl.ds(N_HEADS + c0, 1), :] * o_slc[:, cs]
             + gate_ref[0, pl.ds(2 * N_HEADS + c0, 1), :] * o_win[:, cs])
        o_ref[0, :, r * HEAD_DIM:(r + 1) * HEAD_DIM] = o.T.astype(BF16)


def nsa_prompt_call(q, k_cmp, v_cmp_t, sk, sv_t, wk, wv_t, gates_t, tq):
    B, _, T, _ = q.shape
    ncp = k_cmp.shape[2]
    n_cmp = T // CMP_STRIDE - CMP_PARTS + 1
    n_slc = -(-T // SLC_BLOCK)
    mt = np.zeros((LANES, ncp), np.float32)
    mt[:n_slc, :n_cmp] = cmp_to_slc_matrix(n_cmp, n_slc).T
    mt = jnp.asarray(mt, BF16)
    sk_spec = pl.BlockSpec((1, 1, T, 2 * HEAD_DIM), lambda b, g, i: (b, g, 0, 0))
    k_spec = pl.BlockSpec((1, 1, T, HEAD_DIM), lambda b, g, i: (b, g, 0, 0))
    vt_spec = pl.BlockSpec((1, 1, HEAD_DIM, T), lambda b, g, i: (b, g, 0, 0))
    kern = functools.partial(_nsa_prompt_kernel, tq=tq, n_cmp=n_cmp, n_slc=n_slc)
    return pl.pallas_call(
        kern, grid=(B, N_KV_HEADS, T // tq),
        in_specs=[pl.BlockSpec((1, GQA_REP, tq, HEAD_DIM), lambda b, g, i: (b, g, i, 0)),
                  pl.BlockSpec((1, 1, ncp, HEAD_DIM), lambda b, g, i: (b, g, 0, 0)),
                  pl.BlockSpec((1, 1, HEAD_DIM, ncp), lambda b, g, i: (b, g, 0, 0)),
                  sk_spec, vt_spec, k_spec, vt_spec,
                  pl.BlockSpec((1, LANES, tq), lambda b, g, i: (b, 0, i)),
                  pl.BlockSpec((LANES, ncp), lambda b, g, i: (0, 0))],
        out_specs=pl.BlockSpec((1, tq, GQA_REP * HEAD_DIM), lambda b, g, i: (b, i, g)),
        out_shape=jax.ShapeDtypeStruct((B, T, N_HEADS * HEAD_DIM), BF16),
        compiler_params=_cparams(("parallel", "parallel", "arbitrary")),
        name="nsa_prompt")(q, k_cmp, v_cmp_t, sk, sv_t, wk, wv_t, gates_t, mt)


def _nsa_sample_select_kernel(q_ref, kc_ref, vc_ref, m_ref, ocmp_ref, idx_ref, ok_ref, *, nb, n_cmp, n_slc, qpos):
    rows = GQA_REP * nb
    q = q_ref[0].astype(F32).reshape(rows, HEAD_DIM).astype(BF16)
    ncp = kc_ref.shape[2]
    row_b = lax.broadcasted_iota(jnp.int32, (rows, 1), 0) % nb
    n_id = lax.broadcasted_iota(jnp.int32, (rows, ncp), 1)
    cmask = (n_id * CMP_STRIDE + (CMP_BLOCK - 1) <= qpos) & (n_id < n_cmp)
    o_acc = jnp.zeros((rows, HEAD_DIM), F32)
    pg = jnp.zeros((nb, ncp), F32)
    b_id = lax.broadcasted_iota(jnp.int32, (nb, 1), 0)
    for b in range(nb):
        p = _softmax_rows(_dot_nt(q, kc_ref[b, 0]), cmask)
        o_b = _dot(p.astype(BF16), vc_ref[b, 0])
        o_acc = o_acc + jnp.where(row_b == b, o_b, 0.0)
        p_sum = jnp.sum(p.reshape(GQA_REP, nb, ncp), axis=0)
        pg = pg + jnp.where(b_id == b, p_sum, 0.0)
    ocmp_ref[0] = o_acc.reshape(GQA_REP, nb, HEAD_DIM)

    p_hi, p_lo = _split_bf16(pg)
    sc = _dot(p_hi, m_ref[...]) + _dot(p_lo, m_ref[...])
    nsp = sc.shape[1]
    j_id = lax.broadcasted_iota(jnp.int32, (nb, nsp), 1)
    cur = qpos // SLC_BLOCK
    valid = (j_id * SLC_BLOCK <= qpos) & (j_id < n_slc)
    forced = (j_id == 0) | (j_id == cur) | (j_id == cur - 1)
    sc = jnp.where(valid, jnp.where(forced, FORCED_SCORE, sc), NEG_INF)
    lane = lax.broadcasted_iota(jnp.int32, (nb, LANES), 1)
    idx = jnp.zeros((nb, LANES), jnp.int32)
    okv = jnp.zeros((nb, LANES), jnp.int32)
    for i in range(min(N_SELECT, n_slc)):
        m = jnp.max(sc, axis=-1, keepdims=True)
        first = jnp.min(jnp.where(sc == m, j_id.astype(F32), float(nsp)), axis=-1, keepdims=True).astype(jnp.int32)
        idx = jnp.where(lane == i, first, idx)
        okv = jnp.where(lane == i, jnp.where(m > 0.5 * NEG_INF, 1, 0), okv)
        sc = jnp.where(j_id == first, -3.0e38, sc)
    idx_ref[0] = idx
    ok_ref[0] = okv


def nsa_sample_select_call(q, k_cmp, v_cmp, qpos):
    nb = q.shape[2]
    ncp = k_cmp.shape[2]
    n_cmp = (qpos + 1) // CMP_STRIDE - CMP_PARTS + 1
    n_slc = -(-(qpos + 1) // SLC_BLOCK)
    nsp = LANES * (-(-n_slc // LANES))
    m = np.zeros((ncp, nsp), np.float32)
    m[:n_cmp, :n_slc] = cmp_to_slc_matrix(n_cmp, n_slc)
    m = jnp.asarray(m, BF16)
    cmp_spec = pl.BlockSpec((nb, 1, ncp, HEAD_DIM), lambda g: (0, g, 0, 0))
    kern = functools.partial(_nsa_sample_select_kernel, nb=nb, n_cmp=n_cmp, n_slc=n_slc, qpos=qpos)
    return pl.pallas_call(
        kern, grid=(N_KV_HEADS,),
        in_specs=[pl.BlockSpec((1, GQA_REP, nb, HEAD_DIM), lambda g: (0, g, 0, 0)),
                  cmp_spec, cmp_spec,
                  pl.BlockSpec((ncp, nsp), lambda g: (0, 0))],
        out_specs=[pl.BlockSpec((1, GQA_REP, nb, HEAD_DIM), lambda g: (g, 0, 0, 0)),
                   pl.BlockSpec((1, nb, LANES), lambda g: (g, 0, 0)),
                   pl.BlockSpec((1, nb, LANES), lambda g: (g, 0, 0))],
        out_shape=[jax.ShapeDtypeStruct((N_KV_HEADS, GQA_REP, nb, HEAD_DIM), F32),
                   jax.ShapeDtypeStruct((N_KV_HEADS, nb, LANES), jnp.int32),
                   jax.ShapeDtypeStruct((N_KV_HEADS, nb, LANES), jnp.int32)],
        compiler_params=_cparams(("parallel",)),
        name="nsa_sample_select")(q, k_cmp, v_cmp, m)


def _nsa_sample_attend_kernel(idx_ref, ok_ref, pt_ref, q_ref, *rest, nb, nsel, n_past_blk, qpos):
    kblks, vblks = rest[:nsel], rest[nsel:2 * nsel]
    (wk_ref, wv_ref, skn_ref, svn_ref, wkn_ref, wvn_ref, ocmp_ref, gate_ref, o_ref) = rest[2 * nsel:]
    g = pl.program_id(0)
    b = pl.program_id(1)
    rows = GQA_REP * nb
    qf = q_ref[0].astype(F32).reshape(rows, HEAD_DIM)
    q = qf.astype(BF16)
    row_b = lax.broadcasted_iota(jnp.int32, (rows, 1), 0) % nb

    @pl.when(b == 0)
    def _():
        o_ref[...] = jnp.zeros_like(o_ref)

    head_rows = pl.ds(g, SLC_BLOCK, stride=N_KV_HEADS)
    kcat = jnp.concatenate([kb[0, head_rows, :] for kb in kblks], axis=0).astype(BF16)
    vcat = jnp.concatenate([vb[0, head_rows, :] for vb in vblks], axis=0).astype(BF16)
    nk = nsel * SLC_BLOCK
    slot = lax.broadcasted_iota(jnp.int32, (1, nk), 1) // SLC_BLOCK
    within = lax.broadcasted_iota(jnp.int32, (1, nk), 1) % SLC_BLOCK
    kpos = jnp.zeros((1, nk), jnp.int32)
    kok = jnp.zeros((1, nk), jnp.int32)
    new_ok = jnp.zeros((1, 1), jnp.int32)
    base = (g * nb + b) * LANES
    for s in range(nsel):
        bi = idx_ref[base + s]
        oks = ok_ref[base + s]
        kpos = jnp.where(slot == s, bi * SLC_BLOCK + within, kpos)
        kok = jnp.where(slot == s, jnp.where(bi < n_past_blk, oks, 0), kok)
        new_ok = jnp.maximum(new_ok, jnp.where(bi * SLC_BLOCK <= qpos, jnp.where(bi >= n_past_blk, oks, 0), 0))
    mask = (kok > 0) & (kpos <= qpos)
    s_past = jnp.where(mask, _dot_nt(q, kcat), NEG_INF)
    has_new = new_ok > 0
    s_new = jnp.where(has_new, jnp.sum(qf * skn_ref[0], axis=-1, keepdims=True), NEG_INF)
    m = jnp.maximum(jnp.max(s_past, axis=-1, keepdims=True), s_new)
    p_past = jnp.where(mask, jnp.exp2(s_past - m), 0.0)
    p_new = jnp.where(has_new, jnp.exp2(s_new - m), 0.0)
    den = jnp.maximum(jnp.sum(p_past, axis=-1, keepdims=True) + p_new, 1e-30)
    o_slc = (_dot(p_past.astype(BF16), vcat)
             + p_new * svn_ref[0]) / den

    nw = wk_ref.shape[1] // N_KV_HEADS
    win_rows = pl.ds(g, nw, stride=N_KV_HEADS)
    wpos = qpos - nw + lax.broadcasted_iota(jnp.int32, (1, nw), 1)
    wdist = qpos - wpos
    wmask = (wpos >= 0) & (wdist >= 0) & (wdist < WINDOW)
    s_w = jnp.where(wmask, _dot_nt(q, wk_ref[0, win_rows, :].astype(BF16)), NEG_INF)
    s_wn = jnp.sum(qf * wkn_ref[0], axis=-1, keepdims=True)
    mw = jnp.maximum(jnp.max(s_w, axis=-1, keepdims=True), s_wn)
    p_w = jnp.where(wmask, jnp.exp2(s_w - mw), 0.0)
    p_wn = jnp.exp2(s_wn - mw)
    den_w = jnp.maximum(jnp.sum(p_w, axis=-1, keepdims=True) + p_wn, 1e-30)
    o_win = (_dot(p_w.astype(BF16), wv_ref[0, win_rows, :].astype(BF16))
             + p_wn * wvn_ref[0]) / den_w

    gates = jnp.concatenate([gate_ref[...]] * GQA_REP, axis=0)
    lane = lax.broadcasted_iota(jnp.int32, (rows, LANES), 1)
    row_r = lax.broadcasted_iota(jnp.int32, (rows, 1), 0) // nb

    def gate_col(branch):
        c = branch * N_HEADS + g * GQA_REP + row_r
        return jnp.sum(jnp.where(lane == c, gates, 0.0), axis=-1, keepdims=True)

    o = (gate_col(0) * ocmp_ref[0].reshape(rows, HEAD_DIM) + gate_col(1) * o_slc + gate_col(2) * o_win)
    o_ref[0] += jnp.where(row_b == b, o, 0.0).reshape(GQA_REP, nb, HEAD_DIM)


def nsa_sample_attend_call(idx, ok, page_table, q, pool_k, pool_v, win_k, win_v, sk_new, sv_new, wk_new, wv_new,
                           o_cmp, gates, qpos):
    nb = q.shape[2]
    nsel = min(N_SELECT, -(-(qpos + 1) // SLC_BLOCK))
    n_past_blk = page_table.shape[1] * PAGE_SIZE // SLC_BLOCK
    half = PAGE_SIZE // SLC_BLOCK

    def blk_map(g, b, idx_r, ok_r, pt_r, s):
        bi = jnp.minimum(idx_r[(g * nb + b) * LANES + s], n_past_blk - 1)
        return (pt_r[b, bi // half], bi % half, 0)

    blk_specs = [pl.BlockSpec((1, SLC_BLOCK * N_KV_HEADS, HEAD_DIM), functools.partial(blk_map, s=s))
                 for s in range(nsel)]
    win_spec = pl.BlockSpec((1, win_k.shape[1], HEAD_DIM), lambda g, b, *_: (b, 0, 0))
    new_spec = pl.BlockSpec((1, 1, HEAD_DIM), lambda g, b, *_: (b * N_KV_HEADS + g, 0, 0))
    qo_spec = pl.BlockSpec((1, GQA_REP, nb, HEAD_DIM), lambda g, b, *_: (0, g, 0, 0))
    in_specs = ([qo_spec] + blk_specs + blk_specs + [win_spec, win_spec, new_spec, new_spec, new_spec, new_spec,
                pl.BlockSpec((1, GQA_REP, nb, HEAD_DIM), lambda g, b, *_: (g, 0, 0, 0)),
                pl.BlockSpec((nb, LANES), lambda g, b, *_: (0, 0))])
    gs = pltpu.PrefetchScalarGridSpec(
        num_scalar_prefetch=3, grid=(N_KV_HEADS, nb), in_specs=in_specs,
        out_specs=pl.BlockSpec((1, GQA_REP, nb, HEAD_DIM), lambda g, b, *_: (g, 0, 0, 0)))
    kern = functools.partial(_nsa_sample_attend_kernel, nb=nb, nsel=nsel, n_past_blk=n_past_blk, qpos=qpos)
    return pl.pallas_call(
        kern, grid_spec=gs,
        out_shape=jax.ShapeDtypeStruct((N_KV_HEADS, GQA_REP, nb, HEAD_DIM), F32),
        compiler_params=_cparams(("parallel", "arbitrary")),
        name="nsa_sample_attend")(idx.reshape(-1), ok.reshape(-1), page_table, q,
                                  *([pool_k] * nsel), *([pool_v] * nsel), win_k, win_v,
                                  sk_new, sv_new, wk_new, wv_new, o_cmp, gates)


def _rope_tables(pos):
    half = HEAD_DIM // 2
    inv_freq = ROPE_THETA ** (-jnp.arange(half, dtype=F32) / half)
    ang = pos.astype(F32)[:, None] * inv_freq[None, :]
    cos, sin = jnp.cos(ang), jnp.sin(ang)
    return jnp.concatenate([cos, cos], axis=-1), jnp.concatenate([-sin, sin], axis=-1)


def _trunk(x2, p3, pos, h0r, h0i, w, lay, kv_ctx):
    nb, T, tc, C, R, tm = lay
    depth = w["norm_mix_g"].shape[0]
    n_a = w["ssm_a_re"].shape[0]
    d = x2.shape[1]
    cosf, sinf = _rope_tables(pos)
    ssm_re, ssm_im = [], []
    kv = None
    for i in range(depth):
        if i < n_a:
            prm = s5_params(w["ssm_a_re"][i], w["ssm_a_im"][i], w["ssm_log_dt"][i], w["ssm_b_re"][i],
                            w["ssm_b_im"][i], w["ssm_c_re"][i], w["ssm_c_im"][i], w["ssm_d"][i], nb)
            g3, hr, hi = s5_mixer(x2.reshape(nb, T, d), w["norm_mix_g"][i][None], prm, h0r[i], h0i[i], tc)
            ssm_re.append(rows_to_state(hr, nb))
            ssm_im.append(rows_to_state(hi, nb))
            x2 = glu_call(g3.reshape(nb * T, d), w["ssm_w_glu"], w["ssm_b_glu"], x2, i, tm)
        else:
            j = i - n_a
            q, gates_t = q_call(x2, w["norm_mix_g3"], w["nsa_w_q"], w["nsa_wgt_t"][j], w["nsa_bgt"][j],
                                cosf, sinf, i, j, C, R, tm)
            o2 = kv_ctx["attend"](q, gates_t, kv)
            x2 = oproj_call(o2, w["nsa_w_o"], x2, j, tm)
        x2 = kv_ctx["ffn"](i, x2)
        x2 = ple_call(x2, w["norm_ple_g3"], p3, w["ple_w_gate"], w["ple_w_proj"], w["norm_final_g"][None], i, tm,
                      final_norm=(i == depth - 1))
        if i == n_a - 1:
            kv = kv_ctx["build"](x2, cosf, sinf)
    return x2, jnp.stack(ssm_re), jnp.stack(ssm_im), kv["new"]


def _cmp_weights(w1, b1, w2):
    w1p = w1.reshape(CMP_PARTS, CMP_STRIDE * HEAD_DIM, HEAD_DIM)
    return jnp.concatenate([w1p[0], w1p[1]], axis=1).astype(BF16), b1[None], w2.astype(BF16)


def kernel(x_prompt, x_sample, state_ssm_re, state_ssm_im, cache_cmp_k, cache_cmp_v, cache_slc_k, cache_slc_v, state_win_k, state_win_v, page_table, p_prompt, p_sample, norm_mix_g, norm_ffn_g, norm_ple_g, norm_kv_g, norm_final_g, ssm_a_re, ssm_a_im, ssm_log_dt, ssm_b_re, ssm_b_im, ssm_c_re, ssm_c_im, ssm_d, ssm_w_glu, ssm_b_glu, kv_w, cmp_k_w1, cmp_k_b1, cmp_k_w2, cmp_v_w1, cmp_v_b1, cmp_v_w2, nsa_w_q, nsa_w_gate, nsa_b_gate, nsa_w_o, ffn_w_gate, ffn_w_up, ffn_w_down, ple_w_gate, ple_w_proj):
    B, T, D = x_prompt.shape
    nbs, ts, _ = x_sample.shape
    assert ts == 1
    n_a = ssm_a_re.shape[0]
    past = page_table.shape[1] * PAGE_SIZE
    gate_pad = LANES - nsa_w_gate.shape[-1]
    w = dict(norm_mix_g=norm_mix_g, norm_mix_g3=norm_mix_g[:, None], norm_ffn_g3=norm_ffn_g[:, None],
             norm_ple_g3=norm_ple_g[:, None], norm_final_g=norm_final_g,
             ssm_a_re=ssm_a_re, ssm_a_im=ssm_a_im, ssm_log_dt=ssm_log_dt, ssm_b_re=ssm_b_re, ssm_b_im=ssm_b_im,
             ssm_c_re=ssm_c_re, ssm_c_im=ssm_c_im, ssm_d=ssm_d,
             ssm_w_glu=ssm_w_glu.astype(BF16), ssm_b_glu=ssm_b_glu[:, None],
             nsa_w_q=nsa_w_q.astype(BF16),
             nsa_wgt_t=jnp.pad(jnp.swapaxes(nsa_w_gate, 1, 2), ((0, 0), (0, gate_pad), (0, 0))).astype(BF16),
             nsa_bgt=jnp.pad(nsa_b_gate, ((0, 0), (0, gate_pad)))[..., None],
             nsa_w_o=nsa_w_o.astype(BF16),
             ple_w_gate=ple_w_gate.astype(BF16), ple_w_proj=ple_w_proj.astype(BF16))
    kvw = kv_w.astype(BF16)
    kvg = norm_kv_g[None]
    ck_w = _cmp_weights(cmp_k_w1, cmp_k_b1, cmp_k_w2)
    cv_w = _cmp_weights(cmp_v_w1, cmp_v_b1, cmp_v_w2)

    tm_p = min(ROW_TILE, T)
    tq = min(Q_TILE, T)
    pos_p = jnp.arange(T, dtype=jnp.int32)

    def build_prompt(x2, cosf, sinf):
        ck, cv, sk, sv, wk, wv, skh, wkh, svt, wvt = kv_call(x2, kvg, kvw, cosf, sinf, B, T, tm_p, True)
        slab = min(SLAB_ROWS, T)
        nslab = T // slab
        nu = T // CMP_STRIDE
        ccos, csin = _rope_tables(jnp.arange(nu, dtype=jnp.int32) * CMP_STRIDE + (CMP_BLOCK - 1))
        rows3 = lambda a: a.reshape(B, T * N_KV_HEADS, HEAD_DIM)
        k_cmp = compress_call(rows3(ck), None, *ck_w, ccos, csin, True, B, nslab, slab)
        v_cmp_t = compress_call(rows3(cv), None, cv_w[0], cv_w[1], cv_w[2].T, ccos, csin, False, B, nslab, slab,
                                transpose_out=True)
        r4 = lambda a: a.reshape(B, T, N_KV_HEADS, HEAD_DIM)
        ck, cv, sk, sv, wk, wv = [r4(a) for a in (ck, cv, sk, sv, wk, wv)]
        wb = min(WINDOW, T)
        new = (ck, cv, sk, sv, wk[:, T - wb:], wv[:, T - wb:])
        return dict(k_cmp=k_cmp, v_cmp_t=v_cmp_t, sk=skh, sv_t=svt, wk=wkh, wv_t=wvt, new=new)

    def attend_prompt(q, gates_t, kv):
        o = nsa_prompt_call(q, kv["k_cmp"], kv["v_cmp_t"], kv["sk"], kv["sv_t"], kv["wk"], kv["wv_t"], gates_t, tq)
        return o.reshape(B * T, N_HEADS * HEAD_DIM)

    ffn_bf16 = {}

    def ffn_sample(i, x2):
        x2, *ffn_bf16[i] = ffn_call(x2, w["norm_ffn_g3"], ffn_w_gate, ffn_w_up, ffn_w_down, i, nbs, FFN_TILE,
                                    emit_bf16=True)
        return x2

    def ffn_prompt(i, x2):
        return ffn_call(x2, w["norm_ffn_g3"], *ffn_bf16[i], i, tm_p, FFN_TILE)[0]

    pool3 = lambda a: a.reshape(a.shape[0], -1, HEAD_DIM)
    qpos = past
    pos_s = jnp.full((nbs,), past, jnp.int32)

    def build_sample(x2, cosf, sinf):
        ck, cv, sk, sv, wk, wv = kv_call(x2, kvg, kvw, cosf, sinf, 1, nbs, nbs, False)
        slab = min(PAGED_SLAB_ROWS, past)
        nslab = past // slab
        nu = past // CMP_STRIDE
        ccos, csin = _rope_tables(jnp.arange(nu, dtype=jnp.int32) * CMP_STRIDE + (CMP_BLOCK - 1))
        k_cmp = compress_call(pool3(cache_cmp_k), page_table, *ck_w, ccos, csin, True, nbs, nslab, slab)
        v_cmp = compress_call(pool3(cache_cmp_v), page_table, *cv_w, ccos, csin, False, nbs, nslab, slab)
        r4 = lambda a: a.reshape(nbs, 1, N_KV_HEADS, HEAD_DIM)
        win_k = jnp.concatenate([state_win_k[:, 1:], r4(wk)], axis=1)
        win_v = jnp.concatenate([state_win_v[:, 1:], r4(wv)], axis=1)
        new = (r4(ck), r4(cv), r4(sk), r4(sv), win_k, win_v)
        return dict(k_cmp=k_cmp, v_cmp=v_cmp, sk=sk, sv=sv, wk=wk, wv=wv, new=new)

    def attend_sample(q, gates_t, kv):
        o_cmp, idx, ok = nsa_sample_select_call(q, kv["k_cmp"], kv["v_cmp"], qpos)
        rows1 = lambda a: a.reshape(nbs * N_KV_HEADS, 1, HEAD_DIM)
        o = nsa_sample_attend_call(idx, ok, page_table, q, pool3(cache_slc_k), pool3(cache_slc_v),
                                   pool3(state_win_k), pool3(state_win_v),
                                   rows1(kv["sk"]), rows1(kv["sv"]), rows1(kv["wk"]), rows1(kv["wv"]),
                                   o_cmp, gates_t[0].T, qpos)
        return jnp.transpose(o, (2, 0, 1, 3)).reshape(nbs, N_HEADS * HEAD_DIM).astype(BF16)

    h0r = jnp.stack([state_to_rows(state_ssm_re[i], nbs) for i in range(n_a)])
    h0i = jnp.stack([state_to_rows(state_ssm_im[i], nbs) for i in range(n_a)])
    y_s, sre_s, sim_s, new_s = _trunk(x_sample.reshape(nbs, D), p_sample.reshape(-1, nbs, p_sample.shape[-1]),
                                      pos_s, h0r, h0i, w, (nbs, 1, 1, 1, nbs, nbs),
                                      dict(build=build_sample, attend=attend_sample, ffn=ffn_sample))
    y_s = y_s.reshape(nbs, 1, D)

    h0 = jnp.zeros((n_a, D // PAIR_CH, 2 * B, PAIR_ST), F32)
    y_p, sre_p, sim_p, new_p = _trunk(x_prompt.reshape(B * T, D), p_prompt.reshape(-1, B * T, p_prompt.shape[-1]),
                                      pos_p, h0, h0, w, (B, T, min(SCAN_STEPS, T), B, T, tm_p),
                                      dict(build=build_prompt, attend=attend_prompt, ffn=ffn_prompt))
    y_p = y_p.reshape(B, T, D)

    ck_p, cv_p, sk_p, sv_p, wk_p, wv_p = new_p
    ck_s, cv_s, sk_s, sv_s, wk_s, wv_s = new_s
    return (y_p, y_s, sre_p, sim_p, ck_p, cv_p, sk_p, sv_p, wk_p, wv_p,
            sre_s, sim_s, ck_s, cv_s, sk_s, sv_s, wk_s, wv_s)
```

```python
import functools

import numpy as np
import jax
import jax.numpy as jnp
from jax import lax
from jax.experimental import pallas as pl
from jax.experimental.pallas import tpu as pltpu

F32 = jnp.float32
BF16 = jnp.bfloat16

N_HEADS = 16
N_KV_HEADS = 4
GQA_REP = N_HEADS // N_KV_HEADS
HEAD_DIM = 128
SSM_GROUP = 16
SSM_STATE = 64
CMP_BLOCK = 32
CMP_STRIDE = 16
CMP_PARTS = CMP_BLOCK // CMP_STRIDE
SLC_BLOCK = 64
N_SELECT = 16
WINDOW = 512
PAGE_SIZE = 128
ROPE_THETA = 10000.0
ATTN_SCALE = HEAD_DIM ** -0.5
EXP2_SCALE = ATTN_SCALE * float(np.log2(np.e))
NORM_EPS = 1e-6
NEG_INF = -1e30
FORCED_SCORE = 1e9

LANES = 128
KV_LANES = N_KV_HEADS * HEAD_DIM
PAIR_CH = 2 * LANES
PAIR_ST = (LANES // SSM_GROUP) * SSM_STATE
SLAB_ROWS = 2048
PAGED_SLAB_ROWS = 4096
ROW_TILE = 512
FFN_TILE = 512
Q_TILE = 512
SCAN_STEPS = 128
VMEM_LIMIT = 56 * 1024 * 1024


def _cparams(sem, vmem=VMEM_LIMIT):
    return pltpu.CompilerParams(dimension_semantics=sem, vmem_limit_bytes=vmem)


def _rms(x, g):
    ms = jnp.mean(x * x, axis=-1, keepdims=True)
    return x * lax.rsqrt(ms + NORM_EPS) * g


def _rope(x, cosf, sinf):
    return x * cosf + pltpu.roll(x, HEAD_DIM // 2, axis=1) * sinf


def _dot(a, b):
    return jnp.dot(a, b, preferred_element_type=F32)


def _dot_nt(a, b):
    return lax.dot_general(a, b, (((1,), (1,)), ((), ())), preferred_element_type=F32)


def _s5_kernel(x_ref, gn_ref, wre_ref, wim_ref, cre_ref, cim_ref, ar_ref, ai_ref, d_ref, h0r_ref, h0i_ref,
               g_ref, hr_ref, hi_ref, hn_s, xr_s, xi_s, sr_s, si_s, *, tc, nb, npair):
    @pl.when(pl.program_id(0) == 0)
    def _():
        hr_ref[...] = h0r_ref[...]
        hi_ref[...] = h0i_ref[...]

    r2 = 2 * nb
    rows = tc * r2
    for b in range(nb):
        hn_s[b] = _rms(x_ref[b], gn_ref[...])
    keep = ((lax.broadcasted_iota(jnp.int32, (r2, PAIR_CH), 0) // nb == 1)
            == (lax.broadcasted_iota(jnp.int32, (r2, PAIR_CH), 1) >= LANES))
    lane_hi = lax.broadcasted_iota(jnp.int32, (tc, PAIR_CH), 1) >= LANES

    for k in range(npair):
        cols = slice(k * PAIR_CH, (k + 1) * PAIR_CH)

        if tc > 1:
            hd = jnp.concatenate([hn_s[:, :, cols], hn_s[:, :, cols]], axis=0)
            lhs = jnp.where(keep[None], jnp.swapaxes(hd, 0, 1), 0.0).reshape(rows, PAIR_CH).astype(BF16)
        else:
            v = hn_s[:, 0, cols]
            lhs = jnp.where(keep, jnp.concatenate([v, v], axis=0), 0.0).astype(BF16)
        xr_s[...] = _dot(lhs, wre_ref[k]).reshape(tc, r2, PAIR_ST)
        xi_s[...] = _dot(lhs, wim_ref[k]).reshape(tc, r2, PAIR_ST)
        ar = ar_ref[k]
        ai = ai_ref[k]

        def step(t, carry):
            hr, hi = carry
            nr = ar * hr - ai * hi + xr_s[t]
            ni = ar * hi + ai * hr + xi_s[t]
            sr_s[t] = nr
            si_s[t] = ni
            return nr, ni

        hr, hi = lax.fori_loop(0, tc, step, (hr_ref[k], hi_ref[k]), unroll=min(8, tc))
        hr_ref[k] = hr
        hi_ref[k] = hi
        h_re = sr_s[...].reshape(rows, PAIR_ST).astype(BF16)
        h_im = si_s[...].reshape(rows, PAIR_ST).astype(BF16)
        y = (_dot(h_re, cre_ref[k]) - _dot(h_im, cim_ref[k])).reshape(tc, r2, PAIR_CH)
        if tc > 1:
            y = jnp.swapaxes(y, 0, 1)
            per_row = lambda r: y[r]
        else:
            per_row = lambda r: y[:, r, :]
        for b in range(nb):
            yb = jnp.where(lane_hi, per_row(nb + b), per_row(b))
            yb = yb + d_ref[:, cols] * hn_s[b, :, cols]
            g_ref[b, :, cols] = jax.nn.gelu(yb).astype(BF16)


def s5_mixer(x3, gn, prm, h0r, h0i, tc):
    nb, T, d = x3.shape
    r2 = 2 * nb
    npair = d // PAIR_CH
    full3 = lambda s: (0, 0, 0)
    st_shape = (npair, r2, PAIR_ST)
    kern = functools.partial(_s5_kernel, tc=tc, nb=nb, npair=npair)
    return pl.pallas_call(
        kern, grid=(T // tc,),
        in_specs=[pl.BlockSpec((nb, tc, d), lambda s: (0, s, 0)),
                  pl.BlockSpec((1, d), lambda s: (0, 0)),
                  pl.BlockSpec((npair, PAIR_CH, PAIR_ST), full3),
                  pl.BlockSpec((npair, PAIR_CH, PAIR_ST), full3),
                  pl.BlockSpec((npair, PAIR_ST, PAIR_CH), full3),
                  pl.BlockSpec((npair, PAIR_ST, PAIR_CH), full3),
                  pl.BlockSpec(st_shape, full3),
                  pl.BlockSpec(st_shape, full3),
                  pl.BlockSpec((1, d), lambda s: (0, 0)),
                  pl.BlockSpec(st_shape, full3),
                  pl.BlockSpec(st_shape, full3)],
        out_specs=[pl.BlockSpec((nb, tc, d), lambda s: (0, s, 0)),
                   pl.BlockSpec(st_shape, full3),
                   pl.BlockSpec(st_shape, full3)],
        out_shape=[jax.ShapeDtypeStruct((nb, T, d), BF16),
                   jax.ShapeDtypeStruct(st_shape, F32),
                   jax.ShapeDtypeStruct(st_shape, F32)],
        scratch_shapes=[pltpu.VMEM((nb, tc, d), F32)] + [pltpu.VMEM((tc, r2, PAIR_ST), F32)] * 4,
        compiler_params=_cparams(("arbitrary",)),
        name="s5_mixer")(x3, gn, prm["wre"], prm["wim"], prm["cre"], prm["cim"],
                         prm["ar"], prm["ai"], prm["d"], h0r, h0i)


def s5_params(a_re, a_im, log_dt, b_re, b_im, c_re, c_im, d, nb):
    G, P, H = b_re.shape
    gpc = LANES // H
    nchunk = G // gpc
    dt = jnp.exp(log_dt)[:, None]
    mag = jnp.exp(dt * a_re)
    ab_re, ab_im = mag * jnp.cos(dt * a_im), mag * jnp.sin(dt * a_im)
    den = a_re * a_re + a_im * a_im
    nr, ni = ab_re - 1.0, ab_im
    cf_re = (nr * a_re + ni * a_im) / den
    cf_im = (ni * a_re - nr * a_im) / den
    bb_re = cf_re[..., None] * b_re - cf_im[..., None] * b_im
    bb_im = cf_re[..., None] * b_im + cf_im[..., None] * b_re
    eye = jnp.eye(gpc, dtype=F32)

    def blk_in(bb):
        w = jnp.einsum("cgph,gk->cghkp", bb.reshape(nchunk, gpc, P, H), eye)
        return w.reshape(nchunk // 2, PAIR_CH, gpc * P).astype(BF16)

    def blk_out(cc):
        w = jnp.einsum("cghp,gk->ckpgh", cc.reshape(nchunk, gpc, H, P), eye)
        w = w.reshape(nchunk // 2, 2, gpc * P, LANES)
        return jnp.swapaxes(w, 1, 2).reshape(nchunk // 2, gpc * P, PAIR_CH).astype(BF16)

    def rows(a):
        a = a.reshape(nchunk // 2, 2, 1, gpc * P)
        return jnp.broadcast_to(a, (nchunk // 2, 2, nb, gpc * P)).reshape(nchunk // 2, 2 * nb, gpc * P)

    return dict(wre=blk_in(bb_re), wim=blk_in(bb_im), cre=blk_out(c_re), cim=blk_out(c_im),
                ar=rows(ab_re), ai=rows(ab_im), d=d[None])


def state_to_rows(h, nb):
    npair = h.shape[1] * h.shape[2] // PAIR_ST // 2
    return jnp.transpose(h.reshape(nb, npair, 2, PAIR_ST), (1, 2, 0, 3)).reshape(npair, 2 * nb, PAIR_ST)


def rows_to_state(h, nb):
    npair = h.shape[0]
    h = jnp.transpose(h.reshape(npair, 2, nb, PAIR_ST), (2, 0, 1, 3))
    return h.reshape(nb, npair * 2 * PAIR_ST // SSM_STATE, SSM_STATE)


COL_CHUNK = 512


def _resident(block_shape, index_map):
    return pl.BlockSpec(block_shape, index_map, pipeline_mode=pl.Buffered(1))


def _glu_kernel(a_ref, w_ref, b_ref, x_ref, o_ref):
    a = a_ref[...]
    d = x_ref.shape[1]
    for c in range(0, d, COL_CHUNK):
        z1 = _dot(a, w_ref[:, c:c + COL_CHUNK]) + b_ref[:, c:c + COL_CHUNK]
        z2 = _dot(a, w_ref[:, d + c:d + c + COL_CHUNK]) + b_ref[:, d + c:d + c + COL_CHUNK]
        o_ref[:, c:c + COL_CHUNK] = x_ref[:, c:c + COL_CHUNK] + z1 * jax.nn.sigmoid(z2)


def glu_call(g2, w_glu, b_glu, x2, li, tm):
    m, d = x2.shape
    return pl.pallas_call(
        _glu_kernel, grid=(m // tm,),
        in_specs=[pl.BlockSpec((tm, d), lambda i: (i, 0)),
                  _resident((None, d, 2 * d), lambda i: (li, 0, 0)),
                  _resident((None, 1, 2 * d), lambda i: (li, 0, 0)),
                  pl.BlockSpec((tm, d), lambda i: (i, 0))],
        out_specs=pl.BlockSpec((tm, d), lambda i: (i, 0)),
        out_shape=jax.ShapeDtypeStruct(x2.shape, F32),
        compiler_params=_cparams(("parallel",)),
        name="s5_glu")(g2, w_glu, b_glu, x2)


def _ffn_kernel(x_ref, g_ref, wg_ref, wu_ref, wd_ref, o_ref, *rest, nf, emit_bf16):
    hn_s, acc_s = rest[-2:]
    f = pl.program_id(1)

    @pl.when(f == 0)
    def _():
        hn_s[...] = _rms(x_ref[...], g_ref[...]).astype(BF16)
        acc_s[...] = jnp.zeros_like(acc_s)

    wg, wu, wd = wg_ref[...].astype(BF16), wu_ref[...].astype(BF16), wd_ref[...].astype(BF16)
    if emit_bf16:
        for ref, val in zip(rest[:3], (wg, wu, wd)):
            ref[...] = val
    h = hn_s[...]
    a = _dot(h, wg)
    u = _dot(h, wu)
    act = (a * jax.nn.sigmoid(a) * u).astype(BF16)
    acc_s[...] += _dot(act, wd)

    @pl.when(f == nf - 1)
    def _():
        o_ref[...] = x_ref[...] + acc_s[...]


def ffn_call(x2, g, wg, wu, wd, li, tm, tf, emit_bf16=False):
    m, d = x2.shape
    ff = wg.shape[-1]
    nf = ff // tf
    if emit_bf16:
        assert m == tm
        w_specs = [pl.BlockSpec((None, d, tf), lambda i, f: (li, 0, f)),
                   pl.BlockSpec((None, d, tf), lambda i, f: (li, 0, f)),
                   pl.BlockSpec((None, tf, d), lambda i, f: (li, f, 0))]
    else:
        w_specs = [pl.BlockSpec((d, tf), lambda i, f: (0, f)),
                   pl.BlockSpec((d, tf), lambda i, f: (0, f)),
                   pl.BlockSpec((tf, d), lambda i, f: (f, 0))]
    out_specs = [pl.BlockSpec((tm, d), lambda i, f: (i, 0))]
    out_shape = [jax.ShapeDtypeStruct(x2.shape, F32)]
    if emit_bf16:
        out_specs += [pl.BlockSpec((d, tf), lambda i, f: (0, f)), pl.BlockSpec((d, tf), lambda i, f: (0, f)),
                      pl.BlockSpec((tf, d), lambda i, f: (f, 0))]
        out_shape += [jax.ShapeDtypeStruct((d, ff), BF16)] * 2 + [jax.ShapeDtypeStruct((ff, d), BF16)]
    return pl.pallas_call(
        functools.partial(_ffn_kernel, nf=nf, emit_bf16=emit_bf16), grid=(m // tm, nf),
        in_specs=[pl.BlockSpec((tm, d), lambda i, f: (i, 0)),
                  pl.BlockSpec((None, 1, d), lambda i, f: (li, 0, 0))] + w_specs,
        out_specs=out_specs, out_shape=out_shape,
        scratch_shapes=[pltpu.VMEM((tm, d), BF16), pltpu.VMEM((tm, d), F32)],
        compiler_params=_cparams(("parallel", "arbitrary")),
        name="ffn")(x2, g, wg, wu, wd)


def _ple_kernel(x_ref, g_ref, p_ref, wg_ref, wp_ref, gf_ref, o_ref, *, final_norm):
    hn = _rms(x_ref[...], g_ref[...]).astype(BF16)
    pe = p_ref[...].astype(BF16)
    for c in range(0, x_ref.shape[1], COL_CHUNK):
        cs = slice(c, c + COL_CHUNK)
        gate = jax.nn.sigmoid(_dot(hn, wg_ref[:, cs]))
        o_ref[:, cs] = x_ref[:, cs] + gate * _dot(pe, wp_ref[:, cs])
    if final_norm:
        o_ref[...] = _rms(o_ref[...], gf_ref[...])


def ple_call(x2, g, p3, wg, wp, gf, li, tm, final_norm):
    m, d = x2.shape
    pd = wp.shape[1]
    return pl.pallas_call(
        functools.partial(_ple_kernel, final_norm=final_norm), grid=(m // tm,),
        in_specs=[pl.BlockSpec((tm, d), lambda i: (i, 0)),
                  _resident((None, 1, d), lambda i: (li, 0, 0)),
                  pl.BlockSpec((None, tm, pd), lambda i: (li, i, 0)),
                  _resident((None, d, d), lambda i: (li, 0, 0)),
                  _resident((None, pd, d), lambda i: (li, 0, 0)),
                  _resident((1, d), lambda i: (0, 0))],
        out_specs=pl.BlockSpec((tm, d), lambda i: (i, 0)),
        out_shape=jax.ShapeDtypeStruct(x2.shape, F32),
        compiler_params=_cparams(("parallel",)),
        name="ple")(x2, g, p3, wg, wp, gf)


def _kv_kernel(x_ref, g_ref, w_ref, cos_ref, sin_ref, *outs, heads_out, nt):
    hn = _rms(x_ref[...], g_ref[...]).astype(BF16)
    tm = x_ref.shape[0]
    if heads_out:
        t_row = (pl.program_id(0) % nt) * tm + lax.broadcasted_iota(jnp.int32, (tm, LANES), 0)
        blk_onehot = jnp.where(lax.broadcasted_iota(jnp.int32, (tm, LANES), 1) == t_row // SLC_BLOCK, 1.0, 0.0)
        for h in range(N_KV_HEADS):
            outs[6][0, h, :, HEAD_DIM:] = blk_onehot.astype(BF16)
    for k in range(6):
        z = _dot(hn, w_ref[:, k * KV_LANES:(k + 1) * KV_LANES])
        for h in range(N_KV_HEADS):
            zh = z[:, h * HEAD_DIM:(h + 1) * HEAD_DIM]
            if k in (2, 4):
                zh = _rope(zh, cos_ref[...], sin_ref[...])
            outs[k][pl.ds(h, tm, stride=N_KV_HEADS), :] = zh
            if heads_out and k == 2:
                outs[6][0, h, :, :HEAD_DIM] = zh.astype(BF16)
            if heads_out and k == 4:
                outs[7][0, h] = zh.astype(BF16)
            if heads_out and k in (3, 5):
                outs[8 + (k == 5)][0, h] = zh.T.astype(BF16)


def kv_call(x2, g, kv_w, cosf, sinf, C, R, tm, heads_out):
    m, d = x2.shape
    nt = R // tm
    out_shape = [jax.ShapeDtypeStruct((m * N_KV_HEADS, HEAD_DIM), F32)] * 6
    out_specs = [pl.BlockSpec((tm * N_KV_HEADS, HEAD_DIM), lambda i: (i, 0))] * 6
    if heads_out:
        assert -(-R // SLC_BLOCK) <= LANES
        for width in (2 * HEAD_DIM, HEAD_DIM):
            out_shape += [jax.ShapeDtypeStruct((C, N_KV_HEADS, R, width), BF16)]
            out_specs += [pl.BlockSpec((1, N_KV_HEADS, tm, width), lambda i: (i // nt, 0, i % nt, 0))]
        out_shape += [jax.ShapeDtypeStruct((C, N_KV_HEADS, HEAD_DIM, R), BF16)] * 2
        out_specs += [pl.BlockSpec((1, N_KV_HEADS, HEAD_DIM, tm), lambda i: (i // nt, 0, 0, i % nt))] * 2
    return pl.pallas_call(
        functools.partial(_kv_kernel, heads_out=heads_out, nt=nt), grid=(m // tm,),
        in_specs=[pl.BlockSpec((tm, d), lambda i: (i, 0)),
                  _resident((1, d), lambda i: (0, 0)),
                  _resident((d, 6 * KV_LANES), lambda i: (0, 0)),
                  pl.BlockSpec((tm, HEAD_DIM), lambda i: (i % nt, 0)),
                  pl.BlockSpec((tm, HEAD_DIM), lambda i: (i % nt, 0))],
        out_specs=out_specs, out_shape=out_shape,
        compiler_params=_cparams(("parallel",)),
        name="kv_proj")(x2, g, kv_w, cosf, sinf)


def _q_kernel(x_ref, g_ref, w_ref, wgt_ref, bgt_ref, cos_ref, sin_ref, q_ref, gate_ref):
    hn = _rms(x_ref[...], g_ref[...]).astype(BF16)
    gate_ref[0] = jax.nn.sigmoid(_dot_nt(wgt_ref[...], hn) + bgt_ref[...])
    hpc = COL_CHUNK // HEAD_DIM
    for c in range(N_HEADS // hpc):
        z = _dot(hn, w_ref[:, c * COL_CHUNK:(c + 1) * COL_CHUNK])
        for h in range(hpc):
            zh = _rope(z[:, h * HEAD_DIM:(h + 1) * HEAD_DIM], cos_ref[...], sin_ref[...])
            q_ref[0, c * hpc + h] = (zh * EXP2_SCALE).astype(BF16)


def q_call(x2, g, w_q, wgt_t, bgt, cosf, sinf, li, wi, C, R, tm):
    m, d = x2.shape
    nt = R // tm
    return pl.pallas_call(
        _q_kernel, grid=(m // tm,),
        in_specs=[pl.BlockSpec((tm, d), lambda i: (i, 0)),
                  _resident((None, 1, d), lambda i: (li, 0, 0)),
                  _resident((None, d, N_HEADS * HEAD_DIM), lambda i: (wi, 0, 0)),
                  _resident((LANES, d), lambda i: (0, 0)),
                  _resident((LANES, 1), lambda i: (0, 0)),
                  pl.BlockSpec((tm, HEAD_DIM), lambda i: (i % nt, 0)),
                  pl.BlockSpec((tm, HEAD_DIM), lambda i: (i % nt, 0))],
        out_specs=[pl.BlockSpec((1, N_HEADS, tm, HEAD_DIM), lambda i: (i // nt, 0, i % nt, 0)),
                   pl.BlockSpec((1, LANES, tm), lambda i: (i // nt, 0, i % nt))],
        out_shape=[jax.ShapeDtypeStruct((C, N_HEADS, R, HEAD_DIM), BF16),
                   jax.ShapeDtypeStruct((C, LANES, R), F32)],
        compiler_params=_cparams(("parallel",)),
        name="q_proj")(x2, g, w_q, wgt_t, bgt, cosf, sinf)


def _oproj_kernel(o_ref, w_ref, x_ref, out_ref):
    o = o_ref[...]
    for c in range(0, x_ref.shape[1], COL_CHUNK):
        out_ref[:, c:c + COL_CHUNK] = x_ref[:, c:c + COL_CHUNK] + _dot(o, w_ref[:, c:c + COL_CHUNK])


def oproj_call(o2, w_o, x2, li, tm):
    m, d = x2.shape
    return pl.pallas_call(
        _oproj_kernel, grid=(m // tm,),
        in_specs=[pl.BlockSpec((tm, d), lambda i: (i, 0)),
                  _resident((None, d, d), lambda i: (li, 0, 0)),
                  pl.BlockSpec((tm, d), lambda i: (i, 0))],
        out_specs=pl.BlockSpec((tm, d), lambda i: (i, 0)),
        out_shape=jax.ShapeDtypeStruct(x2.shape, F32),
        compiler_params=_cparams(("parallel",)),
        name="o_proj")(o2, w_o, x2)


def _compress_kernel(*refs, npage, rope, nslab, upb, paged, transpose_out):
    if paged:
        refs = refs[1:]
    pages = refs[:npage]
    w1_ref, b1_ref, w2_ref, cos_ref, sin_ref, o_ref, p_s = refs[npage:]
    i = pl.program_id(1)
    upp = upb // npage
    unit_rows = CMP_STRIDE * N_KV_HEADS
    swapped = [jnp.swapaxes(pg[0].reshape(upp, unit_rows, HEAD_DIM), 0, 1) for pg in pages]
    for g in range(N_KV_HEADS):
        cols = []
        for s in range(CMP_STRIDE):
            parts = [sw[s * N_KV_HEADS + g] for sw in swapped]
            cols.append(parts[0] if npage == 1 else jnp.concatenate(parts, axis=0))
        lhs = jnp.concatenate(cols, axis=1).astype(BF16)
        p_s[g, pl.ds(pl.multiple_of(i * upb, upb), upb), :] = _dot(lhs, w1_ref[...])

    @pl.when(i == nslab - 1)
    def _():
        nu = nslab * upb
        for g in range(N_KV_HEADS):
            p_s[g, nu:nu + 8, :] = jnp.zeros((8, 2 * HEAD_DIM), F32)
            h = b1_ref[...] + p_s[g, 0:nu, 0:HEAD_DIM] + p_s[g, 1:nu + 1, HEAD_DIM:2 * HEAD_DIM]
            a = jax.nn.gelu(h).astype(BF16)
            if transpose_out:
                o_ref[0, g] = _dot_nt(w2_ref[...], a).astype(BF16)
            else:
                c = _dot(a, w2_ref[...])
                if rope:
                    c = _rope(c, cos_ref[...], sin_ref[...])
                o_ref[0, g] = c.astype(BF16)


def compress_call(src, page_table, w1r, b1, w2, cosf, sinf, rope, nbatch, nslab, slab, transpose_out=False):
    upb = slab // CMP_STRIDE
    nu = nslab * upb
    paged = page_table is not None
    npage = slab // PAGE_SIZE if paged else 1
    const = lambda b, i, *_: (0, 0)
    if paged:
        page_specs = [pl.BlockSpec((1, PAGE_SIZE * N_KV_HEADS, HEAD_DIM),
                                   functools.partial(lambda b, i, pt, r: (pt[b, i * npage + r], 0, 0), r=r))
                      for r in range(npage)]
    else:
        page_specs = [pl.BlockSpec((1, slab * N_KV_HEADS, HEAD_DIM), lambda b, i: (b, i, 0))]
    in_specs = page_specs + [pl.BlockSpec((CMP_STRIDE * HEAD_DIM, 2 * HEAD_DIM), const),
                             pl.BlockSpec((1, HEAD_DIM), const),
                             pl.BlockSpec((HEAD_DIM, HEAD_DIM), const),
                             pl.BlockSpec((nu, HEAD_DIM), const),
                             pl.BlockSpec((nu, HEAD_DIM), const)]
    o_dims = (HEAD_DIM, nu) if transpose_out else (nu, HEAD_DIM)
    out_specs = pl.BlockSpec((1, N_KV_HEADS) + o_dims, lambda b, i, *_: (b, 0, 0, 0))
    kern = functools.partial(_compress_kernel, npage=npage, rope=rope, nslab=nslab, upb=upb,
                             paged=paged, transpose_out=transpose_out)
    scratch = [pltpu.VMEM((N_KV_HEADS, nu + 8, 2 * HEAD_DIM), F32)]
    out_shape = jax.ShapeDtypeStruct((nbatch, N_KV_HEADS) + o_dims, BF16)
    args = [src] * npage + [w1r, b1, w2, cosf, sinf]
    if paged:
        gs = pltpu.PrefetchScalarGridSpec(num_scalar_prefetch=1, grid=(nbatch, nslab), in_specs=in_specs,
                                          out_specs=out_specs, scratch_shapes=scratch)
        return pl.pallas_call(kern, grid_spec=gs, out_shape=out_shape,
                              compiler_params=_cparams(("parallel", "arbitrary")),
                              name="compress_paged")(page_table, *args)
    return pl.pallas_call(kern, grid=(nbatch, nslab), in_specs=in_specs, out_specs=out_specs,
                          out_shape=out_shape, scratch_shapes=scratch,
                          compiler_params=_cparams(("parallel", "arbitrary")),
                          name="compress_rows")(*args)


def cmp_to_slc_matrix(n_cmp, n_slc):
    units = np.arange(n_cmp)[:, None] + np.arange(CMP_PARTS)[None, :]
    sblk = (units * CMP_STRIDE) // SLC_BLOCK
    m = np.zeros((n_cmp, n_slc), np.float32)
    np.add.at(m, (np.repeat(np.arange(n_cmp), CMP_PARTS), sblk.reshape(-1)), 1.0)
    return m


def _split_bf16(x):
    hi = x.astype(BF16)
    return hi, (x - hi.astype(F32)).astype(BF16)


def _softmax_rows(s, mask):
    s = jnp.where(mask, s, NEG_INF)
    m = jnp.max(s, axis=-1, keepdims=True)
    p = jnp.where(mask, jnp.exp2(s - m), 0.0)
    return p / jnp.maximum(jnp.sum(p, axis=-1, keepdims=True), 1e-30)


def _nsa_prompt_kernel(q_ref, kc_ref, vct_ref, sk_ref, svt_ref, wk_ref, wvt_ref, gate_ref, mt_ref, o_ref,
                       *, tq, n_cmp, n_slc):
    g = pl.program_id(1)
    qi = pl.program_id(2)
    t0 = qi * tq
    rows = GQA_REP * tq
    q = q_ref[0].reshape(rows, HEAD_DIM)

    def tile4(a):
        return jnp.concatenate([a] * GQA_REP, axis=1)

    ncp = kc_ref.shape[2]
    n_id = lax.broadcasted_iota(jnp.int32, (ncp, tq), 0)
    tq_pos = t0 + lax.broadcasted_iota(jnp.int32, (ncp, tq), 1)
    cmask = tile4((n_id * CMP_STRIDE + (CMP_BLOCK - 1) <= tq_pos) & (n_id < n_cmp))
    s = jnp.where(cmask, _dot_nt(kc_ref[0, 0], q), NEG_INF)
    m = jnp.max(s, axis=0, keepdims=True)
    p = jnp.where(cmask, jnp.exp2(s - m), 0.0)
    p = p / jnp.maximum(jnp.sum(p, axis=0, keepdims=True), 1e-30)
    o_cmp = _dot(vct_ref[0, 0], p.astype(BF16))
    p_grp = p[:, 0:tq] + p[:, tq:2 * tq] + p[:, 2 * tq:3 * tq] + p[:, 3 * tq:4 * tq]

    p_hi, p_lo = _split_bf16(p_grp)
    nsp = 8 * (-(-n_slc // 8))
    sc =(_dot(mt_ref[...], p_hi) + _dot(mt_ref[...], p_lo))[0:nsp]
    j_id = lax.broadcasted_iota(jnp.int32, (nsp, tq), 0)
    tpos = t0 + lax.broadcasted_iota(jnp.int32, (nsp, tq), 1)
    cur = tpos // SLC_BLOCK
    valid = (j_id * SLC_BLOCK <= tpos) & (j_id < n_slc)
    forced = (j_id == 0) | (j_id == cur) | (j_id == cur - 1)
    sc = jnp.where(valid, jnp.where(forced, FORCED_SCORE, sc), NEG_INF)
    cnt = jnp.zeros((nsp, tq), F32)
    for i in range(n_slc):
        ri = sc[i:i + 1, :]
        beats = (ri > sc) | ((ri == sc) & (j_id > i))
        cnt = cnt + jnp.where(beats, 1.0, 0.0)
    sel_bias = jnp.where((cnt < float(min(N_SELECT, n_slc))) & (sc > 0.5 * NEG_INF), 0.0, NEG_INF)
    if nsp < LANES:
        sel_bias = jnp.concatenate([sel_bias, jnp.zeros((LANES - nsp, tq), F32)], axis=0)
    sel_rows = sel_bias.T.astype(BF16)
    q_slc = jnp.concatenate([q, jnp.concatenate([sel_rows] * GQA_REP, axis=0)], axis=1)

    kp_off = lax.broadcasted_iota(jnp.int32, (tq, tq), 0)
    tq_off = lax.broadcasted_iota(jnp.int32, (tq, tq), 1)

    def slc_bias(k0, diag):
        return jnp.where(kp_off <= tq_off, 0.0, NEG_INF) if diag else None

    def win_bias(k0, diag):
        dist = (t0 + tq_off) - (k0 + kp_off)
        return jnp.where((dist >= 0) & (dist < WINDOW), 0.0, NEG_INF)

    def flash(k_ref, vt_ref, qmat, nblk, bias_fn):
        def update(its, carry, diag):
            m, l, acc = carry
            k0s = [pl.multiple_of((qi - it) * tq, tq) for it in its]
            sbs = []
            for n, k0 in enumerate(k0s):
                sb = _dot_nt(k_ref[0, 0, pl.ds(k0, tq), :], qmat)
                bias = bias_fn(k0, diag and n == 0)
                sbs.append(sb if bias is None else sb + tile4(bias))
            m_new = m
            for sb in sbs:
                m_new = jnp.maximum(m_new, jnp.max(sb, axis=0, keepdims=True))
            alpha = jnp.exp2(m - m_new)
            l = alpha * l
            acc = alpha * acc
            for k0, sb in zip(k0s, sbs):
                pb = jnp.exp2(sb - m_new)
                l = l + jnp.sum(pb, axis=0, keepdims=True)
                acc = acc + _dot(vt_ref[0, 0, :, pl.ds(k0, tq)], pb.astype(BF16))
            return m_new, l, acc

        init = (jnp.full((1, rows), NEG_INF, F32), jnp.zeros((1, rows), F32), jnp.zeros((HEAD_DIM, rows), F32))
        carry = lax.cond(nblk >= 2, lambda c: update([0, 1], c, True), lambda c: update([0], c, True), init)
        rest = jnp.maximum(nblk - 2, 0)
        carry = lax.fori_loop(0, rest // 2, lambda j, c: update([2 + 2 * j, 3 + 2 * j], c, False), carry)
        carry = lax.cond(rest % 2 == 1, lambda c: update([nblk - 1], c, False), lambda c: c, carry)
        _, l, acc = carry
        return acc / jnp.maximum(l, 1e-30)

    o_slc = flash(sk_ref, svt_ref, q_slc, qi + 1, slc_bias)
    o_win = flash(wk_ref, wvt_ref, q, jnp.minimum(qi, WINDOW // tq) + 1, win_bias)

    for r in range(GQA_REP):
        cs = slice(r * tq, (r + 1) * tq)
        c0 = g * GQA_REP + r
        o = (gate_ref[0, pl.ds(c0, 1), :] * o_cmp[:, cs] + gate_ref[0, pl.ds(N_HEADS + c0, 1), :] * o_slc[:, cs]
             + gate_ref[0, pl.ds(2 * N_HEADS + c0, 1), :] * o_win[:, cs])
        o_ref[0, :, r * HEAD_DIM:(r + 1) * HEAD_DIM] = o.T.astype(BF16)


def nsa_prompt_call(q, k_cmp, v_cmp_t, sk, sv_t, wk, wv_t, gates_t, tq):
    B, _, T, _ = q.shape
    ncp = k_cmp.shape[2]
    n_cmp = T // CMP_STRIDE - CMP_PARTS + 1
    n_slc = -(-T // SLC_BLOCK)
    mt = np.zeros((LANES, ncp), np.float32)
    mt[:n_slc, :n_cmp] = cmp_to_slc_matrix(n_cmp, n_slc).T
    mt = jnp.asarray(mt, BF16)
    sk_spec = pl.BlockSpec((1, 1, T, 2 * HEAD_DIM), lambda b, g, i: (b, g, 0, 0))
    k_spec = pl.BlockSpec((1, 1, T, HEAD_DIM), lambda b, g, i: (b, g, 0, 0))
    vt_spec = pl.BlockSpec((1, 1, HEAD_DIM, T), lambda b, g, i: (b, g, 0, 0))
    kern = functools.partial(_nsa_prompt_kernel, tq=tq, n_cmp=n_cmp, n_slc=n_slc)
    return pl.pallas_call(
        kern, grid=(B, N_KV_HEADS, T // tq),
        in_specs=[pl.BlockSpec((1, GQA_REP, tq, HEAD_DIM), lambda b, g, i: (b, g, i, 0)),
                  pl.BlockSpec((1, 1, ncp, HEAD_DIM), lambda b, g, i: (b, g, 0, 0)),
                  pl.BlockSpec((1, 1, HEAD_DIM, ncp), lambda b, g, i: (b, g, 0, 0)),
                  sk_spec, vt_spec, k_spec, vt_spec,
                  pl.BlockSpec((1, LANES, tq), lambda b, g, i: (b, 0, i)),
                  pl.BlockSpec((LANES, ncp), lambda b, g, i: (0, 0))],
        out_specs=pl.BlockSpec((1, tq, GQA_REP * HEAD_DIM), lambda b, g, i: (b, i, g)),
        out_shape=jax.ShapeDtypeStruct((B, T, N_HEADS * HEAD_DIM), BF16),
        compiler_params=_cparams(("parallel", "parallel", "arbitrary")),
        name="nsa_prompt")(q, k_cmp, v_cmp_t, sk, sv_t, wk, wv_t, gates_t, mt)


def _nsa_sample_select_kernel(q_ref, kc_ref, vc_ref, m_ref, ocmp_ref, idx_ref, ok_ref, *, nb, n_cmp, n_slc, qpos):
    rows = GQA_REP * nb
    q = q_ref[0].astype(F32).reshape(rows, HEAD_DIM).astype(BF16)
    ncp = kc_ref.shape[2]
    row_b = lax.broadcasted_iota(jnp.int32, (rows, 1), 0) % nb
    n_id = lax.broadcasted_iota(jnp.int32, (rows, ncp), 1)
    cmask = (n_id * CMP_STRIDE + (CMP_BLOCK - 1) <= qpos) & (n_id < n_cmp)
    o_acc = jnp.zeros((rows, HEAD_DIM), F32)
    pg = jnp.zeros((nb, ncp), F32)
    b_id = lax.broadcasted_iota(jnp.int32, (nb, 1), 0)
    for b in range(nb):
        p = _softmax_rows(_dot_nt(q, kc_ref[b, 0]), cmask)
        o_b = _dot(p.astype(BF16), vc_ref[b, 0])
        o_acc = o_acc + jnp.where(row_b == b, o_b, 0.0)
        p_sum = jnp.sum(p.reshape(GQA_REP, nb, ncp), axis=0)
        pg = pg + jnp.where(b_id == b, p_sum, 0.0)
    ocmp_ref[0] = o_acc.reshape(GQA_REP, nb, HEAD_DIM)

    p_hi, p_lo = _split_bf16(pg)
    sc = _dot(p_hi, m_ref[...]) + _dot(p_lo, m_ref[...])
    nsp = sc.shape[1]
    j_id = lax.broadcasted_iota(jnp.int32, (nb, nsp), 1)
    cur = qpos // SLC_BLOCK
    valid = (j_id * SLC_BLOCK <= qpos) & (j_id < n_slc)
    forced = (j_id == 0) | (j_id == cur) | (j_id == cur - 1)
    sc = jnp.where(valid, jnp.where(forced, FORCED_SCORE, sc), NEG_INF)
    lane = lax.broadcasted_iota(jnp.int32, (nb, LANES), 1)
    idx = jnp.zeros((nb, LANES), jnp.int32)
    okv = jnp.zeros((nb, LANES), jnp.int32)
    for i in range(min(N_SELECT, n_slc)):
        m = jnp.max(sc, axis=-1, keepdims=True)
        first = jnp.min(jnp.where(sc == m, j_id.astype(F32), float(nsp)), axis=-1, keepdims=True).astype(jnp.int32)
        idx = jnp.where(lane == i, first, idx)
        okv = jnp.where(lane == i, jnp.where(m > 0.5 * NEG_INF, 1, 0), okv)
        sc = jnp.where(j_id == first, -3.0e38, sc)
    idx_ref[0] = idx
    ok_ref[0] = okv


def nsa_sample_select_call(q, k_cmp, v_cmp, qpos):
    nb = q.shape[2]
    ncp = k_cmp.shape[2]
    n_cmp = (qpos + 1) // CMP_STRIDE - CMP_PARTS + 1
    n_slc = -(-(qpos + 1) // SLC_BLOCK)
    nsp = LANES * (-(-n_slc // LANES))
    m = np.zeros((ncp, nsp), np.float32)
    m[:n_cmp, :n_slc] = cmp_to_slc_matrix(n_cmp, n_slc)
    m = jnp.asarray(m, BF16)
    cmp_spec = pl.BlockSpec((nb, 1, ncp, HEAD_DIM), lambda g: (0, g, 0, 0))
    kern = functools.partial(_nsa_sample_select_kernel, nb=nb, n_cmp=n_cmp, n_slc=n_slc, qpos=qpos)
    return pl.pallas_call(
        kern, grid=(N_KV_HEADS,),
        in_specs=[pl.BlockSpec((1, GQA_REP, nb, HEAD_DIM), lambda g: (0, g, 0, 0)),
                  cmp_spec, cmp_spec,
                  pl.BlockSpec((ncp, nsp), lambda g: (0, 0))],
        out_specs=[pl.BlockSpec((1, GQA_REP, nb, HEAD_DIM), lambda g: (g, 0, 0, 0)),
                   pl.BlockSpec((1, nb, LANES), lambda g: (g, 0, 0)),
                   pl.BlockSpec((1, nb, LANES), lambda g: (g, 0, 0))],
        out_shape=[jax.ShapeDtypeStruct((N_KV_HEADS, GQA_REP, nb, HEAD_DIM), F32),
                   jax.ShapeDtypeStruct((N_KV_HEADS, nb, LANES), jnp.int32),
                   jax.ShapeDtypeStruct((N_KV_HEADS, nb, LANES), jnp.int32)],
        compiler_params=_cparams(("parallel",)),
        name="nsa_sample_select")(q, k_cmp, v_cmp, m)


def _nsa_sample_attend_kernel(idx_ref, ok_ref, pt_ref, q_ref, *rest, nb, nsel, n_past_blk, qpos):
    kblks, vblks = rest[:nsel], rest[nsel:2 * nsel]
    (wk_ref, wv_ref, skn_ref, svn_ref, wkn_ref, wvn_ref, ocmp_ref, gate_ref, o_ref) = rest[2 * nsel:]
    g = pl.program_id(0)
    b = pl.program_id(1)
    rows = GQA_REP * nb
    qf = q_ref[0].astype(F32).reshape(rows, HEAD_DIM)
    q = qf.astype(BF16)
    row_b = lax.broadcasted_iota(jnp.int32, (rows, 1), 0) % nb

    @pl.when(b == 0)
    def _():
        o_ref[...] = jnp.zeros_like(o_ref)

    head_rows = pl.ds(g, SLC_BLOCK, stride=N_KV_HEADS)
    kcat = jnp.concatenate([kb[0, head_rows, :] for kb in kblks], axis=0).astype(BF16)
    vcat = jnp.concatenate([vb[0, head_rows, :] for vb in vblks], axis=0).astype(BF16)
    nk = nsel * SLC_BLOCK
    slot = lax.broadcasted_iota(jnp.int32, (1, nk), 1) // SLC_BLOCK
    within = lax.broadcasted_iota(jnp.int32, (1, nk), 1) % SLC_BLOCK
    kpos = jnp.zeros((1, nk), jnp.int32)
    kok = jnp.zeros((1, nk), jnp.int32)
    new_ok = jnp.zeros((1, 1), jnp.int32)
    base = (g * nb + b) * LANES
    for s in range(nsel):
        bi = idx_ref[base + s]
        oks = ok_ref[base + s]
        kpos = jnp.where(slot == s, bi * SLC_BLOCK + within, kpos)
        kok = jnp.where(slot == s, jnp.where(bi < n_past_blk, oks, 0), kok)
        new_ok = jnp.maximum(new_ok, jnp.where(bi * SLC_BLOCK <= qpos, jnp.where(bi >= n_past_blk, oks, 0), 0))
    mask = (kok > 0) & (kpos <= qpos)
    s_past = jnp.where(mask, _dot_nt(q, kcat), NEG_INF)
    has_new = new_ok > 0
    s_new = jnp.where(has_new, jnp.sum(qf * skn_ref[0], axis=-1, keepdims=True), NEG_INF)
    m = jnp.maximum(jnp.max(s_past, axis=-1, keepdims=True), s_new)
    p_past = jnp.where(mask, jnp.exp2(s_past - m), 0.0)
    p_new = jnp.where(has_new, jnp.exp2(s_new - m), 0.0)
    den = jnp.maximum(jnp.sum(p_past, axis=-1, keepdims=True) + p_new, 1e-30)
    o_slc = (_dot(p_past.astype(BF16), vcat)
             + p_new * svn_ref[0]) / den

    nw = wk_ref.shape[1] // N_KV_HEADS
    win_rows = pl.ds(g, nw, stride=N_KV_HEADS)
    wpos = qpos - nw + lax.broadcasted_iota(jnp.int32, (1, nw), 1)
    wdist = qpos - wpos
    wmask = (wpos >= 0) & (wdist >= 0) & (wdist < WINDOW)
    s_w = jnp.where(wmask, _dot_nt(q, wk_ref[0, win_rows, :].astype(BF16)), NEG_INF)
    s_wn = jnp.sum(qf * wkn_ref[0], axis=-1, keepdims=True)
    mw = jnp.maximum(jnp.max(s_w, axis=-1, keepdims=True), s_wn)
    p_w = jnp.where(wmask, jnp.exp2(s_w - mw), 0.0)
    p_wn = jnp.exp2(s_wn - mw)
    den_w = jnp.maximum(jnp.sum(p_w, axis=-1, keepdims=True) + p_wn, 1e-30)
    o_win = (_dot(p_w.astype(BF16), wv_ref[0, win_rows, :].astype(BF16))
             + p_wn * wvn_ref[0]) / den_w

    gates = jnp.concatenate([gate_ref[...]] * GQA_REP, axis=0)
    lane = lax.broadcasted_iota(jnp.int32, (rows, LANES), 1)
    row_r = lax.broadcasted_iota(jnp.int32, (rows, 1), 0) // nb

    def gate_col(branch):
        c = branch * N_HEADS + g * GQA_REP + row_r
        return jnp.sum(jnp.where(lane == c, gates, 0.0), axis=-1, keepdims=True)

    o = (gate_col(0) * ocmp_ref[0].reshape(rows, HEAD_DIM) + gate_col(1) * o_slc + gate_col(2) * o_win)
    o_ref[0] += jnp.where(row_b == b, o, 0.0).reshape(GQA_REP, nb, HEAD_DIM)


def nsa_sample_attend_call(idx, ok, page_table, q, pool_k, pool_v, win_k, win_v, sk_new, sv_new, wk_new, wv_new,
                           o_cmp, gates, qpos):
    nb = q.shape[2]
    nsel = min(N_SELECT, -(-(qpos + 1) // SLC_BLOCK))
    n_past_blk = page_table.shape[1] * PAGE_SIZE // SLC_BLOCK
    half = PAGE_SIZE // SLC_BLOCK

    def blk_map(g, b, idx_r, ok_r, pt_r, s):
        bi = jnp.minimum(idx_r[(g * nb + b) * LANES + s], n_past_blk - 1)
        return (pt_r[b, bi // half], bi % half, 0)

    blk_specs = [pl.BlockSpec((1, SLC_BLOCK * N_KV_HEADS, HEAD_DIM), functools.partial(blk_map, s=s))
                 for s in range(nsel)]
    win_spec = pl.BlockSpec((1, win_k.shape[1], HEAD_DIM), lambda g, b, *_: (b, 0, 0))
    new_spec = pl.BlockSpec((1, 1, HEAD_DIM), lambda g, b, *_: (b * N_KV_HEADS + g, 0, 0))
    qo_spec = pl.BlockSpec((1, GQA_REP, nb, HEAD_DIM), lambda g, b, *_: (0, g, 0, 0))
    in_specs = ([qo_spec] + blk_specs + blk_specs + [win_spec, win_spec, new_spec, new_spec, new_spec, new_spec,
                pl.BlockSpec((1, GQA_REP, nb, HEAD_DIM), lambda g, b, *_: (g, 0, 0, 0)),
                pl.BlockSpec((nb, LANES), lambda g, b, *_: (0, 0))])
    gs = pltpu.PrefetchScalarGridSpec(
        num_scalar_prefetch=3, grid=(N_KV_HEADS, nb), in_specs=in_specs,
        out_specs=pl.BlockSpec((1, GQA_REP, nb, HEAD_DIM), lambda g, b, *_: (g, 0, 0, 0)))
    kern = functools.partial(_nsa_sample_attend_kernel, nb=nb, nsel=nsel, n_past_blk=n_past_blk, qpos=qpos)
    return pl.pallas_call(
        kern, grid_spec=gs,
        out_shape=jax.ShapeDtypeStruct((N_KV_HEADS, GQA_REP, nb, HEAD_DIM), F32),
        compiler_params=_cparams(("parallel", "arbitrary")),
        name="nsa_sample_attend")(idx.reshape(-1), ok.reshape(-1), page_table, q,
                                  *([pool_k] * nsel), *([pool_v] * nsel), win_k, win_v,
                                  sk_new, sv_new, wk_new, wv_new, o_cmp, gates)


def _rope_tables(pos):
    half = HEAD_DIM // 2
    inv_freq = ROPE_THETA ** (-jnp.arange(half, dtype=F32) / half)
    ang = pos.astype(F32)[:, None] * inv_freq[None, :]
    cos, sin = jnp.cos(ang), jnp.sin(ang)
    return jnp.concatenate([cos, cos], axis=-1), jnp.concatenate([-sin, sin], axis=-1)


def _trunk(x2, p3, pos, h0r, h0i, w, lay, kv_ctx):
    nb, T, tc, C, R, tm = lay
    depth = w["norm_mix_g"].shape[0]
    n_a = w["ssm_a_re"].shape[0]
    d = x2.shape[1]
    cosf, sinf = _rope_tables(pos)
    ssm_re, ssm_im = [], []
    kv = None
    for i in range(depth):
        if i < n_a:
            prm = s5_params(w["ssm_a_re"][i], w["ssm_a_im"][i], w["ssm_log_dt"][i], w["ssm_b_re"][i],
                            w["ssm_b_im"][i], w["ssm_c_re"][i], w["ssm_c_im"][i], w["ssm_d"][i], nb)
            g3, hr, hi = s5_mixer(x2.reshape(nb, T, d), w["norm_mix_g"][i][None], prm, h0r[i], h0i[i], tc)
            ssm_re.append(rows_to_state(hr, nb))
            ssm_im.append(rows_to_state(hi, nb))
            x2 = glu_call(g3.reshape(nb * T, d), w["ssm_w_glu"], w["ssm_b_glu"], x2, i, tm)
        else:
            j = i - n_a
            q, gates_t = q_call(x2, w["norm_mix_g3"], w["nsa_w_q"], w["nsa_wgt_t"][j], w["nsa_bgt"][j],
                                cosf, sinf, i, j, C, R, tm)
            o2 = kv_ctx["attend"](q, gates_t, kv)
            x2 = oproj_call(o2, w["nsa_w_o"], x2, j, tm)
        x2 = kv_ctx["ffn"](i, x2)
        x2 = ple_call(x2, w["norm_ple_g3"], p3, w["ple_w_gate"], w["ple_w_proj"], w["norm_final_g"][None], i, tm,
                      final_norm=(i == depth - 1))
        if i == n_a - 1:
            kv = kv_ctx["build"](x2, cosf, sinf)
    return x2, jnp.stack(ssm_re), jnp.stack(ssm_im), kv["new"]


def _cmp_weights(w1, b1, w2):
    w1p = w1.reshape(CMP_PARTS, CMP_STRIDE * HEAD_DIM, HEAD_DIM)
    return jnp.concatenate([w1p[0], w1p[1]], axis=1).astype(BF16), b1[None], w2.astype(BF16)


def kernel(x_prompt, x_sample, state_ssm_re, state_ssm_im, cache_cmp_k, cache_cmp_v, cache_slc_k, cache_slc_v, state_win_k, state_win_v, page_table, p_prompt, p_sample, norm_mix_g, norm_ffn_g, norm_ple_g, norm_kv_g, norm_final_g, ssm_a_re, ssm_a_im, ssm_log_dt, ssm_b_re, ssm_b_im, ssm_c_re, ssm_c_im, ssm_d, ssm_w_glu, ssm_b_glu, kv_w, cmp_k_w1, cmp_k_b1, cmp_k_w2, cmp_v_w1, cmp_v_b1, cmp_v_w2, nsa_w_q, nsa_w_gate, nsa_b_gate, nsa_w_o, ffn_w_gate, ffn_w_up, ffn_w_down, ple_w_gate, ple_w_proj):
    B, T, D = x_prompt.shape
    nbs, ts, _ = x_sample.shape
    assert ts == 1
    n_a = ssm_a_re.shape[0]
    past = page_table.shape[1] * PAGE_SIZE
    gate_pad = LANES - nsa_w_gate.shape[-1]
    w = dict(norm_mix_g=norm_mix_g, norm_mix_g3=norm_mix_g[:, None], norm_ffn_g3=norm_ffn_g[:, None],
             norm_ple_g3=norm_ple_g[:, None], norm_final_g=norm_final_g,
             ssm_a_re=ssm_a_re, ssm_a_im=ssm_a_im, ssm_log_dt=ssm_log_dt, ssm_b_re=ssm_b_re, ssm_b_im=ssm_b_im,
             ssm_c_re=ssm_c_re, ssm_c_im=ssm_c_im, ssm_d=ssm_d,
             ssm_w_glu=ssm_w_glu.astype(BF16), ssm_b_glu=ssm_b_glu[:, None],
             nsa_w_q=nsa_w_q.astype(BF16),
             nsa_wgt_t=jnp.pad(jnp.swapaxes(nsa_w_gate, 1, 2), ((0, 0), (0, gate_pad), (0, 0))).astype(BF16),
             nsa_bgt=jnp.pad(nsa_b_gate, ((0, 0), (0, gate_pad)))[..., None],
             nsa_w_o=nsa_w_o.astype(BF16),
             ple_w_gate=ple_w_gate.astype(BF16), ple_w_proj=ple_w_proj.astype(BF16))
    kvw = kv_w.astype(BF16)
    kvg = norm_kv_g[None]
    ck_w = _cmp_weights(cmp_k_w1, cmp_k_b1, cmp_k_w2)
    cv_w = _cmp_weights(cmp_v_w1, cmp_v_b1, cmp_v_w2)

    tm_p = min(ROW_TILE, T)
    tq = min(Q_TILE, T)
    pos_p = jnp.arange(T, dtype=jnp.int32)

    def build_prompt(x2, cosf, sinf):
        ck, cv, sk, sv, wk, wv, skh, wkh, svt, wvt = kv_call(x2, kvg, kvw, cosf, sinf, B, T, tm_p, True)
        slab = min(SLAB_ROWS, T)
        nslab = T // slab
        nu = T // CMP_STRIDE
        ccos, csin = _rope_tables(jnp.arange(nu, dtype=jnp.int32) * CMP_STRIDE + (CMP_BLOCK - 1))
        rows3 = lambda a: a.reshape(B, T * N_KV_HEADS, HEAD_DIM)
        k_cmp = compress_call(rows3(ck), None, *ck_w, ccos, csin, True, B, nslab, slab)
        v_cmp_t = compress_call(rows3(cv), None, cv_w[0], cv_w[1], cv_w[2].T, ccos, csin, False, B, nslab, slab,
                                transpose_out=True)
        r4 = lambda a: a.reshape(B, T, N_KV_HEADS, HEAD_DIM)
        ck, cv, sk, sv, wk, wv = [r4(a) for a in (ck, cv, sk, sv, wk, wv)]
        wb = min(WINDOW, T)
        new = (ck, cv, sk, sv, wk[:, T - wb:], wv[:, T - wb:])
        return dict(k_cmp=k_cmp, v_cmp_t=v_cmp_t, sk=skh, sv_t=svt, wk=wkh, wv_t=wvt, new=new)

    def attend_prompt(q, gates_t, kv):
        o = nsa_prompt_call(q, kv["k_cmp"], kv["v_cmp_t"], kv["sk"], kv["sv_t"], kv["wk"], kv["wv_t"], gates_t, tq)
        return o.reshape(B * T, N_HEADS * HEAD_DIM)

    ffn_bf16 = {}

    def ffn_sample(i, x2):
        x2, *ffn_bf16[i] = ffn_call(x2, w["norm_ffn_g3"], ffn_w_gate, ffn_w_up, ffn_w_down, i, nbs, FFN_TILE,
                                    emit_bf16=True)
        return x2

    def ffn_prompt(i, x2):
        return ffn_call(x2, w["norm_ffn_g3"], *ffn_bf16[i], i, 2 * tm_p, FFN_TILE // 2)[0]

    pool3 = lambda a: a.reshape(a.shape[0], -1, HEAD_DIM)
    qpos = past
    pos_s = jnp.full((nbs,), past, jnp.int32)

    def build_sample(x2, cosf, sinf):
        ck, cv, sk, sv, wk, wv = kv_call(x2, kvg, kvw, cosf, sinf, 1, nbs, nbs, False)
        slab = min(PAGED_SLAB_ROWS, past)
        nslab = past // slab
        nu = past // CMP_STRIDE
        ccos, csin = _rope_tables(jnp.arange(nu, dtype=jnp.int32) * CMP_STRIDE + (CMP_BLOCK - 1))
        k_cmp = compress_call(pool3(cache_cmp_k), page_table, *ck_w, ccos, csin, True, nbs, nslab, slab)
        v_cmp = compress_call(pool3(cache_cmp_v), page_table, *cv_w, ccos, csin, False, nbs, nslab, slab)
        r4 = lambda a: a.reshape(nbs, 1, N_KV_HEADS, HEAD_DIM)
        win_k = jnp.concatenate([state_win_k[:, 1:], r4(wk)], axis=1)
        win_v = jnp.concatenate([state_win_v[:, 1:], r4(wv)], axis=1)
        new = (r4(ck), r4(cv), r4(sk), r4(sv), win_k, win_v)
        return dict(k_cmp=k_cmp, v_cmp=v_cmp, sk=sk, sv=sv, wk=wk, wv=wv, new=new)

    def attend_sample(q, gates_t, kv):
        o_cmp, idx, ok = nsa_sample_select_call(q, kv["k_cmp"], kv["v_cmp"], qpos)
        rows1 = lambda a: a.reshape(nbs * N_KV_HEADS, 1, HEAD_DIM)
        o = nsa_sample_attend_call(idx, ok, page_table, q, pool3(cache_slc_k), pool3(cache_slc_v),
                                   pool3(state_win_k), pool3(state_win_v),
                                   rows1(kv["sk"]), rows1(kv["sv"]), rows1(kv["wk"]), rows1(kv["wv"]),
                                   o_cmp, gates_t[0].T, qpos)
        return jnp.transpose(o, (2, 0, 1, 3)).reshape(nbs, N_HEADS * HEAD_DIM).astype(BF16)

    h0r = jnp.stack([state_to_rows(state_ssm_re[i], nbs) for i in range(n_a)])
    h0i = jnp.stack([state_to_rows(state_ssm_im[i], nbs) for i in range(n_a)])
    y_s, sre_s, sim_s, new_s = _trunk(x_sample.reshape(nbs, D), p_sample.reshape(-1, nbs, p_sample.shape[-1]),
                                      pos_s, h0r, h0i, w, (nbs, 1, 1, 1, nbs, nbs),
                                      dict(build=build_sample, attend=attend_sample, ffn=ffn_sample))
    y_s = y_s.reshape(nbs, 1, D)

    h0 = jnp.zeros((n_a, D // PAIR_CH, 2 * B, PAIR_ST), F32)
    y_p, sre_p, sim_p, new_p = _trunk(x_prompt.reshape(B * T, D), p_prompt.reshape(-1, B * T, p_prompt.shape[-1]),
                                      pos_p, h0, h0, w, (B, T, min(SCAN_STEPS, T), B, T, tm_p),
                                      dict(build=build_prompt, attend=attend_prompt, ffn=ffn_prompt))
    y_p = y_p.reshape(B, T, D)

    ck_p, cv_p, sk_p, sv_p, wk_p, wv_p = new_p
    ck_s, cv_s, sk_s, sv_s, wk_s, wv_s = new_s
    return (y_p, y_s, sre_p, sim_p, ck_p, cv_p, sk_p, sv_p, wk_p, wv_p,
            sre_s, sim_s, ck_s, cv_s, sk_s, sv_s, wk_s, wv_s)
```
